```python
import jax
import jax.numpy as jnp
from jax import lax
import numpy as np

D_MODEL = 2048
BATCH = 4
SEQ = 2048
DEPTH = 4
DEC_BATCH = 8
DEC_SEQ = 4
PAST_LEN = 16384
PAGE_SIZE = 128

N_A = DEPTH // 2
N_B = DEPTH - N_A
RW_HEAD = 64
RW_HEADS = D_MODEL // RW_HEAD
DECAY_LORA = 96
AAA_LORA = 96
MV_LORA = 64
GATE_LORA = 256
GN_EPS = 64e-5
N_HEADS = 16
HEAD_DIM = 128
N_KV = 4
Q_PER_KV = N_HEADS // N_KV
CMP_BLOCK = 32
CMP_HIDDEN = 2 * HEAD_DIM
SEL_BLOCK = 64
N_SEL = 16
WINDOW = 512
Q_BLOCK = 64
FORCED_BONUS = 100.0
N_EXPERTS = 64
N_GROUPS = 8
TOPK_GROUPS = 4
TOP_K = 8
D_EXPERT = 512
D_SHARED = 512
ROUTED_SCALE = 2.5
PLE_DIM = 256
LN_EPS = 1e-5
ALPHA = (2 * DEPTH) ** 0.25
BETA = (8 * DEPTH) ** -0.25

kernel_name = 'yoco_rwkv7_nsa_moe_step'

F32 = jnp.float32


def layer_norm(x, g, b, eps=LN_EPS):
    xf = x.astype(F32)
    mu = jnp.mean(xf, -1, keepdims=True)
    xc = xf - mu
    var = jnp.mean(xc * xc, -1, keepdims=True)
    return (xc * lax.rsqrt(var + eps) * g + b).astype(x.dtype)


def masked_softmax(s, mask):
    s = jnp.where(mask, s.astype(F32), -1e30)
    m = jnp.max(s, -1, keepdims=True)
    e = jnp.where(mask, jnp.exp(s - m), 0.0)
    return e / jnp.maximum(jnp.sum(e, -1, keepdims=True), 1e-30)


def swiglu(x, w_in, w_out):
    f = w_out.shape[0]
    hu = x @ w_in
    return (jax.nn.silu(hu[..., :f]) * hu[..., f:]) @ w_out


def wkv7_scan(s0, r, w, k, v, a, b):
    def step(s, inp):
        r_t, w_t, k_t, v_t, a_t, b_t = inp
        sa = jnp.einsum('bhij,bhj->bhi', s, a_t)
        s = s * w_t[:, :, None, :] + sa[..., None] * b_t[:, :, None, :] + v_t[..., None] * k_t[:, :, None, :]
        return s, jnp.einsum('bhij,bhj->bhi', s, r_t)
    xs = tuple(jnp.moveaxis(z.astype(F32), 1, 0) for z in (r, w, k, v, a, b))
    s, ys = lax.scan(step, s0.astype(F32), xs)
    return s, jnp.moveaxis(ys, 0, 1)


def rwkv7_time_mix(x, shift_prev, wkv_prev, v_first, mix, w_rkv, w0, w1, w2, a0, a1, a2, vres,
                   g1, g2, k_k, k_a, r_k, lnx_g, lnx_b, w_o):
    B, T, D = x.shape
    xx = jnp.concatenate([shift_prev[:, None, :].astype(x.dtype), x[:, :-1]], axis=1) - x
    xr, xw, xk, xv, xa, xg = (x + xx * mix[j] for j in range(6))
    r = xr @ w_rkv[0]
    k = xk @ w_rkv[1]
    v = xv @ w_rkv[2]
    logw = -jax.nn.softplus(-(w0 + jnp.tanh(xw @ w1) @ w2)) - 0.5
    if vres is None:
        v_first = v
    else:
        v0, v1, v2 = vres
        v = v + (v_first - v) * jax.nn.sigmoid(v0 + (xv @ v1) @ v2)
    a = jax.nn.sigmoid(a0 + (xa @ a1) @ a2)
    g = jax.nn.sigmoid(xg @ g1) @ g2
    heads = lambda z: z.reshape(B, T, RW_HEADS, RW_HEAD)
    kk = heads(k * k_k).astype(F32)
    kk = kk * lax.rsqrt(jnp.maximum(jnp.sum(kk * kk, -1, keepdims=True), 1e-24))
    k = k * (1.0 + (a - 1.0) * k_a)
    decay = jnp.exp(-jnp.exp(logw.astype(F32)))
    rh, kh, vh, ah = heads(r), heads(k), heads(v), heads(a)
    s_new, y = wkv7_scan(wkv_prev, rh, heads(decay), kh, vh, -kk, kk * ah)
    mu = jnp.mean(y, -1, keepdims=True)
    yc = y - mu
    var = jnp.mean(yc * yc, -1, keepdims=True)
    yn = (yc * lax.rsqrt(var + GN_EPS)).reshape(B, T, D) * lnx_g + lnx_b
    bonus = jnp.sum(rh * kh * r_k, -1, keepdims=True) * vh
    y = yn.astype(x.dtype) + bonus.reshape(B, T, D)
    return (y * g) @ w_o, v_first, x[:, -1], s_new


def compress_blocks(rows, pos, w1, w2):
    B, Tk, G, Dh = rows.shape
    nc = Tk // CMP_BLOCK
    blk = rows[:, :nc * CMP_BLOCK].reshape(B, nc, CMP_BLOCK, G, Dh) + pos[:, None, :]
    blk = jnp.moveaxis(blk, 3, 2).reshape(B, nc, G, CMP_BLOCK * Dh)
    return jax.nn.gelu(blk @ w1) @ w2


def build_shared_kv(h, past, kv_ln_g, kv_ln_b, w_kv, cmp_pos, cmp_w1, cmp_w2):
    B, T, _ = h.shape
    rows = (layer_norm(h, kv_ln_g, kv_ln_b) @ w_kv).reshape(B, T, 6, N_KV, HEAD_DIM)
    new_paged = rows[:, :, :4]
    new_win = rows[:, :, 4:]
    if past is None:
        full = new_paged
        q_offset = 0
        kw = jnp.pad(new_win, ((0, 0), (WINDOW, 0), (0, 0), (0, 0), (0, 0)))
        kw_pos0 = -WINDOW
        win_state = new_win[:, max(T - WINDOW, 0):]
    else:
        pages, page_table, win_buf = past
        n_pages = page_table.shape[1]
        past_len = n_pages * pages.shape[1]
        past_rows = pages[page_table].reshape(B, past_len, 4, N_KV, HEAD_DIM)
        full = jnp.concatenate([past_rows, new_paged], axis=1)
        q_offset = past_len
        kw = jnp.concatenate([win_buf, new_win], axis=1)
        kw_pos0 = past_len - win_buf.shape[1]
        win_state = kw[:, T:]
    tk = full.shape[1]
    kc = compress_blocks(full[:, :, 0], cmp_pos[0], cmp_w1[0], cmp_w2[0])
    vc = compress_blocks(full[:, :, 1], cmp_pos[1], cmp_w1[1], cmp_w2[1])
    ns = -(-tk // SEL_BLOCK)
    sel = jnp.pad(full[:, :, 2:4], ((0, 0), (0, ns * SEL_BLOCK - tk), (0, 0), (0, 0), (0, 0)))
    sel = sel.reshape(B, ns, SEL_BLOCK, 2, N_KV, HEAD_DIM)
    ks = jnp.transpose(sel[:, :, :, 0], (0, 3, 1, 2, 4))
    vs = jnp.transpose(sel[:, :, :, 1], (0, 3, 1, 2, 4))
    ctx = dict(chunked=past is None, q_offset=q_offset, kc=kc, vc=vc, ks=ks, vs=vs,
               kw=kw[:, :, 0], vw=kw[:, :, 1], kw_pos0=kw_pos0)
    return ctx, new_paged, win_state


def nsa_core(q, gates, qpos, kc, vc, ks, vs, kw, vw, kwpos):
    B, Q, G, R, Dh = q.shape
    nc = kc.shape[1]
    ns = ks.shape[2]
    scale = Dh ** -0.5
    s_c = jnp.einsum('bqgrd,bcgd->bgrqc', q, kc).astype(F32) * scale
    c_end = (jnp.arange(nc) + 1) * CMP_BLOCK - 1
    p_c = masked_softmax(s_c, c_end[None, :] <= qpos[:, None])
    o_c = jnp.einsum('bgrqc,bcgd->bqgrd', p_c.astype(vc.dtype), vc)
    ratio = SEL_BLOCK // CMP_BLOCK
    imp = jnp.pad(jnp.sum(p_c, 2), ((0, 0), (0, 0), (0, 0), (0, ns * ratio - nc)))
    imp = jnp.sum(imp.reshape(B, G, Q, ns, ratio), -1)
    blk = jnp.arange(ns)[None, :]
    cur = (qpos // SEL_BLOCK)[:, None]
    forced = (blk == 0) | (blk == cur) | (blk == cur - 1)
    valid = blk * SEL_BLOCK <= qpos[:, None]
    score = jnp.where(valid, imp + FORCED_BONUS * forced.astype(F32), -1.0)
    n_top = min(N_SEL, ns)
    idx = lax.top_k(score, n_top)[1]
    bi = jnp.arange(B)[:, None, None, None]
    gi = jnp.arange(G)[None, :, None, None]
    k_sel = ks[bi, gi, idx].reshape(B, G, Q, n_top * SEL_BLOCK, Dh)
    v_sel = vs[bi, gi, idx].reshape(B, G, Q, n_top * SEL_BLOCK, Dh)
    kpos = (idx[..., None] * SEL_BLOCK + jnp.arange(SEL_BLOCK)).reshape(B, G, Q, n_top * SEL_BLOCK)
    s_s = jnp.einsum('bqgrd,bgqkd->bgrqk', q, k_sel).astype(F32) * scale
    p_s = masked_softmax(s_s, (kpos <= qpos[:, None])[:, :, None])
    o_s = jnp.einsum('bgrqk,bgqkd->bqgrd', p_s.astype(v_sel.dtype), v_sel)
    s_w = jnp.einsum('bqgrd,bwgd->bgrqw', q, kw).astype(F32) * scale
    dist = qpos[:, None] - kwpos[None, :]
    p_w = masked_softmax(s_w, (dist >= 0) & (dist < WINDOW) & (kwpos[None, :] >= 0))
    o_w = jnp.einsum('bgrqw,bwgd->bqgrd', p_w.astype(vw.dtype), vw)
    return gates[..., 0:1] * o_c + gates[..., 1:2] * o_s + gates[..., 2:3] * o_w


def nsa_time_mix(x, ctx, w_qg, b_g, w_o):
    B, T, _ = x.shape
    qg = x @ w_qg
    q = qg[..., :N_HEADS * HEAD_DIM].reshape(B, T, N_KV, Q_PER_KV, HEAD_DIM)
    gates = jax.nn.sigmoid((qg[..., N_HEADS * HEAD_DIM:] + b_g).astype(F32))
    gates = gates.reshape(B, T, N_KV, Q_PER_KV, 3).astype(x.dtype)
    kc, vc, ks, vs, kw, vw = ctx['kc'], ctx['vc'], ctx['ks'], ctx['vs'], ctx['kw'], ctx['vw']
    if ctx['chunked']:
        nch = T // Q_BLOCK
        qs = jnp.moveaxis(q.reshape(B, nch, Q_BLOCK, N_KV, Q_PER_KV, HEAD_DIM), 1, 0)
        gs = jnp.moveaxis(gates.reshape(B, nch, Q_BLOCK, N_KV, Q_PER_KV, 3), 1, 0)

        def query_block(args):
            qc, gc, c = args
            start = c * Q_BLOCK
            kwc = lax.dynamic_slice_in_dim(kw, start, WINDOW + Q_BLOCK, axis=1)
            vwc = lax.dynamic_slice_in_dim(vw, start, WINDOW + Q_BLOCK, axis=1)
            qpos = ctx['q_offset'] + start + jnp.arange(Q_BLOCK)
            kwpos = start + ctx['kw_pos0'] + jnp.arange(WINDOW + Q_BLOCK)
            return nsa_core(qc, gc, qpos, kc, vc, ks, vs, kwc, vwc, kwpos)

        o = lax.map(query_block, (qs, gs, jnp.arange(nch)))
        o = jnp.moveaxis(o, 0, 1).reshape(B, T, N_HEADS * HEAD_DIM)
    else:
        qpos = ctx['q_offset'] + jnp.arange(T)
        kwpos = ctx['kw_pos0'] + jnp.arange(kw.shape[1])
        o = nsa_core(q, gates, qpos, kc, vc, ks, vs, kw, vw, kwpos).reshape(B, T, N_HEADS * HEAD_DIM)
    return o @ w_o


def routed_experts(xf, e_idx, e_w, w_in, w_out):
    n, d = xf.shape
    k = e_idx.shape[1]
    n_exp = w_in.shape[0]
    f = w_out.shape[1]
    nk = n * k
    bm = max(8, min(128, nk // n_exp))
    flat_e = e_idx.reshape(-1)
    order = jnp.argsort(flat_e)
    sorted_e = flat_e[order]
    counts = jnp.bincount(flat_e, length=n_exp)
    padded = (counts + bm - 1) // bm * bm
    pad_end = jnp.cumsum(padded)
    pad_start = pad_end - padded
    start = jnp.cumsum(counts) - counts
    dest_sorted = pad_start[sorted_e] + (jnp.arange(nk) - start[sorted_e])
    n_blocks = (nk + bm - 1) // bm + n_exp
    n_rows = n_blocks * bm
    token_of_row = jnp.full((n_rows,), n, jnp.int32).at[dest_sorted].set((order // k).astype(jnp.int32))
    blk_expert = jnp.minimum(jnp.searchsorted(pad_end, jnp.arange(n_blocks) * bm, side='right'), n_exp - 1)
    xb = jnp.concatenate([xf, jnp.zeros((1, d), xf.dtype)], axis=0)[token_of_row].reshape(n_blocks, bm, d)

    def expert_block(args):
        xblk, e = args
        hu = xblk @ w_in[e]
        return (jax.nn.silu(hu[:, :f]) * hu[:, f:]) @ w_out[e]

    yb = lax.map(expert_block, (xb, blk_expert)).reshape(n_rows, d)
    row_of_assign = jnp.zeros((nk,), jnp.int32).at[order].set(dest_sorted.astype(jnp.int32))
    y = yb[row_of_assign].reshape(n, k, d)
    return jnp.einsum('nkd,nk->nd', y, e_w)


def moe_ffn(x, router, bias, w_in, w_out, ws_in, ws_out):
    B, T, D = x.shape
    xf = x.reshape(B * T, D)
    n = B * T
    s = jax.nn.sigmoid((xf @ router).astype(F32))
    sb = s + bias.astype(F32)
    per = N_EXPERTS // N_GROUPS
    g_score = jnp.sum(lax.top_k(sb.reshape(n, N_GROUPS, per), 2)[0], -1)
    g_idx = lax.top_k(g_score, TOPK_GROUPS)[1]
    g_keep = jnp.any(g_idx[:, :, None] == jnp.arange(N_GROUPS), axis=1)
    e_keep = jnp.repeat(g_keep, per, axis=1)
    e_idx = lax.top_k(jnp.where(e_keep, sb, -1e9), TOP_K)[1]
    e_w = jnp.take_along_axis(s, e_idx, axis=1)
    e_w = e_w / jnp.sum(e_w, -1, keepdims=True) * ROUTED_SCALE
    y = routed_experts(xf, e_idx, e_w.astype(x.dtype), w_in, w_out) + swiglu(xf, ws_in, ws_out)
    return y.reshape(B, T, D)


def decoder_group(x, pe, shift0, wkv0, past, P):
    v_first = None
    shifts, wkvs = [], []
    ctx, kv_rows, win_state = None, None, None
    for i in range(DEPTH):
        if i < N_A:
            vres = None if i == 0 else (P['rw_v0'][i - 1], P['rw_v1'][i - 1], P['rw_v2'][i - 1])
            h, v_first, sh, s_new = rwkv7_time_mix(
                x, shift0[i], wkv0[i], v_first, P['rw_mix'][i], P['rw_w_rkv'][i],
                P['rw_w0'][i], P['rw_w1'][i], P['rw_w2'][i], P['rw_a0'][i], P['rw_a1'][i], P['rw_a2'][i],
                vres, P['rw_g1'][i], P['rw_g2'][i], P['rw_k_k'][i], P['rw_k_a'][i], P['rw_r_k'][i],
                P['rw_lnx_g'][i], P['rw_lnx_b'][i], P['rw_w_o'][i])
            shifts.append(sh)
            wkvs.append(s_new)
        else:
            if ctx is None:
                ctx, kv_rows, win_state = build_shared_kv(
                    x, past, P['kv_ln_g'], P['kv_ln_b'], P['w_kv'], P['cmp_pos'], P['cmp_w1'], P['cmp_w2'])
            j = i - N_A
            h = nsa_time_mix(x, ctx, P['nsa_w_qg'][j], P['nsa_b_g'][j], P['nsa_w_o'][j])
        x = layer_norm(ALPHA * x + h, P['ln_g'][i, 0], P['ln_b'][i, 0])
        m = moe_ffn(x, P['moe_router'][i], P['moe_bias'][i], P['moe_w_in'][i], P['moe_w_out'][i],
                    P['moe_ws_in'][i], P['moe_ws_out'][i])
        x = layer_norm(ALPHA * x + m, P['ln_g'][i, 1], P['ln_b'][i, 1])
        x = x + jax.nn.sigmoid(x @ P['ple_gate'][i] + P['ple_gate_b'][i]) * (pe[i] @ P['ple_proj'][i])
    return x, kv_rows, win_state, jnp.stack(wkvs), jnp.stack(shifts)


def setup_inputs(seed: int = 0) -> dict:
    key = jax.random.key(seed)
    keys = iter(jax.random.split(key, 80))

    def nrm(shape, scale=1.0):
        return jax.random.normal(next(keys), shape, jnp.float32) * scale

    def unif(shape, lo, hi):
        return jax.random.uniform(next(keys), shape, jnp.float32, lo, hi)

    n_pages = PAST_LEN // PAGE_SIZE
    n_used = DEC_BATCH * n_pages
    n_phys = n_used + max(1, n_used // 4)
    win_buf = min(WINDOW, PAST_LEN)
    d_q = N_HEADS * HEAD_DIM
    D = D_MODEL
    page_table = jax.random.permutation(next(keys), n_phys)[:n_used].reshape(DEC_BATCH, n_pages).astype(jnp.int32)
    return {
        'x_prompt': nrm((BATCH, SEQ, D)),
        'x_sample': nrm((DEC_BATCH, DEC_SEQ, D)),
        'cache_kv_pages': nrm((n_phys, PAGE_SIZE, 4, N_KV, HEAD_DIM)),
        'cache_win_kv': nrm((DEC_BATCH, win_buf, 2, N_KV, HEAD_DIM)),
        'state_wkv': nrm((N_A, DEC_BATCH, RW_HEADS, RW_HEAD, RW_HEAD), 0.3),
        'state_shift': nrm((N_A, DEC_BATCH, D)),
        'page_table': page_table,
        'p_prompt': nrm((DEPTH, BATCH, SEQ, PLE_DIM)),
        'p_sample': nrm((DEPTH, DEC_BATCH, DEC_SEQ, PLE_DIM)),
        'ln_g': 1.0 + nrm((DEPTH, 2, D), 0.02),
        'ln_b': nrm((DEPTH, 2, D), 0.02),
        'rw_mix': unif((N_A, 6, D), 0.0, 1.0),
        'rw_w_rkv': nrm((N_A, 3, D, D), D ** -0.5),
        'rw_w0': unif((N_A, D), -6.0, -1.0),
        'rw_w1': nrm((N_A, D, DECAY_LORA), D ** -0.5),
        'rw_w2': nrm((N_A, DECAY_LORA, D), 0.5 * DECAY_LORA ** -0.5),
        'rw_a0': nrm((N_A, D), 0.1),
        'rw_a1': nrm((N_A, D, AAA_LORA), D ** -0.5),
        'rw_a2': nrm((N_A, AAA_LORA, D), AAA_LORA ** -0.5),
        'rw_v0': nrm((N_A - 1, D), 0.1),
        'rw_v1': nrm((N_A - 1, D, MV_LORA), D ** -0.5),
        'rw_v2': nrm((N_A - 1, MV_LORA, D), MV_LORA ** -0.5),
        'rw_g1': nrm((N_A, D, GATE_LORA), D ** -0.5),
        'rw_g2': nrm((N_A, GATE_LORA, D), GATE_LORA ** -0.5),
        'rw_k_k': 0.85 + nrm((N_A, D), 0.02),
        'rw_k_a': 1.0 + nrm((N_A, D), 0.02),
        'rw_r_k': nrm((N_A, RW_HEADS, RW_HEAD), 0.1),
        'rw_lnx_g': 1.0 + nrm((N_A, D), 0.02),
        'rw_lnx_b': nrm((N_A, D), 0.02),
        'rw_w_o': nrm((N_A, D, D), BETA * D ** -0.5),
        'kv_ln_g': 1.0 + nrm((D,), 0.02),
        'kv_ln_b': nrm((D,), 0.02),
        'w_kv': nrm((D, 6 * N_KV * HEAD_DIM), D ** -0.5),
        'cmp_pos': nrm((2, CMP_BLOCK, HEAD_DIM), 0.1),
        'cmp_w1': nrm((2, CMP_BLOCK * HEAD_DIM, CMP_HIDDEN), (CMP_BLOCK * HEAD_DIM) ** -0.5),
        'cmp_w2': nrm((2, CMP_HIDDEN, HEAD_DIM), CMP_HIDDEN ** -0.5),
        'nsa_w_qg': nrm((N_B, D, d_q + 3 * N_HEADS), D ** -0.5),
        'nsa_b_g': nrm((N_B, 3 * N_HEADS), 0.1),
        'nsa_w_o': nrm((N_B, d_q, D), BETA * d_q ** -0.5),
        'moe_router': nrm((DEPTH, D, N_EXPERTS), D ** -0.5),
        'moe_bias': nrm((DEPTH, N_EXPERTS), 0.01),
        'moe_w_in': nrm((DEPTH, N_EXPERTS, D, 2 * D_EXPERT), D ** -0.5),
        'moe_w_out': nrm((DEPTH, N_EXPERTS, D_EXPERT, D), BETA * D_EXPERT ** -0.5),
        'moe_ws_in': nrm((DEPTH, D, 2 * D_SHARED), D ** -0.5),
        'moe_ws_out': nrm((DEPTH, D_SHARED, D), BETA * D_SHARED ** -0.5),
        'ple_proj': nrm((DEPTH, PLE_DIM, D), PLE_DIM ** -0.5),
        'ple_gate': nrm((DEPTH, D, D), D ** -0.5),
        'ple_gate_b': nrm((DEPTH, D), 0.1),
    }


def reference(x_prompt, x_sample, cache_kv_pages, cache_win_kv, state_wkv, state_shift, page_table,
              p_prompt, p_sample, ln_g, ln_b, rw_mix, rw_w_rkv, rw_w0, rw_w1, rw_w2, rw_a0, rw_a1, rw_a2,
              rw_v0, rw_v1, rw_v2, rw_g1, rw_g2, rw_k_k, rw_k_a, rw_r_k, rw_lnx_g, rw_lnx_b, rw_w_o,
              kv_ln_g, kv_ln_b, w_kv, cmp_pos, cmp_w1, cmp_w2, nsa_w_qg, nsa_b_g, nsa_w_o,
              moe_router, moe_bias, moe_w_in, moe_w_out, moe_ws_in, moe_ws_out,
              ple_proj, ple_gate, ple_gate_b):
    P = dict(ln_g=ln_g, ln_b=ln_b, rw_mix=rw_mix, rw_w_rkv=rw_w_rkv, rw_w0=rw_w0, rw_w1=rw_w1, rw_w2=rw_w2,
             rw_a0=rw_a0, rw_a1=rw_a1, rw_a2=rw_a2, rw_v0=rw_v0, rw_v1=rw_v1, rw_v2=rw_v2,
             rw_g1=rw_g1, rw_g2=rw_g2, rw_k_k=rw_k_k, rw_k_a=rw_k_a, rw_r_k=rw_r_k,
             rw_lnx_g=rw_lnx_g, rw_lnx_b=rw_lnx_b, rw_w_o=rw_w_o, kv_ln_g=kv_ln_g, kv_ln_b=kv_ln_b,
             w_kv=w_kv, cmp_pos=cmp_pos, cmp_w1=cmp_w1, cmp_w2=cmp_w2, nsa_w_qg=nsa_w_qg,
             nsa_b_g=nsa_b_g, nsa_w_o=nsa_w_o, moe_router=moe_router, moe_bias=moe_bias,
             moe_w_in=moe_w_in, moe_w_out=moe_w_out, moe_ws_in=moe_ws_in, moe_ws_out=moe_ws_out,
             ple_proj=ple_proj, ple_gate=ple_gate, ple_gate_b=ple_gate_b)
    bp = x_prompt.shape[0]
    shift0 = jnp.zeros((N_A, bp, D_MODEL), x_prompt.dtype)
    wkv0 = jnp.zeros((N_A, bp, RW_HEADS, RW_HEAD, RW_HEAD), F32)
    y_prompt, kv_rows_prompt, win_prompt, wkv_prompt, shift_prompt = decoder_group(
        x_prompt, p_prompt, shift0, wkv0, None, P)
    y_sample, kv_rows_sample, win_sample, wkv_sample, shift_sample = decoder_group(
        x_sample, p_sample, state_shift, state_wkv, (cache_kv_pages, page_table, cache_win_kv), P)
    return (y_prompt, y_sample, kv_rows_prompt, kv_rows_sample, win_prompt, win_sample,
            wkv_prompt, wkv_sample, shift_prompt, shift_sample)
```

```python
import functools

import jax
import jax.numpy as jnp
from jax import lax
from jax.experimental import pallas as pl
from jax.experimental.pallas import tpu as pltpu

F32 = jnp.float32
BF16 = jnp.bfloat16

RW_HEAD = 64
GN_EPS = 64e-5
N_KV = 4
HEAD_DIM = 128
CMP_BLOCK = 32
SEL_BLOCK = 64
N_SEL = 16
WINDOW = 512
FORCED_BONUS = 100.0
N_GROUPS = 8
TOPK_GROUPS = 4
TOP_K = 8
ROUTED_SCALE = 2.5
LN_EPS = 1e-5
NEG = -1e30

LANES = 128
SUBLANES = 8
PAGE = 128
VMEM_LIMIT = 56 * 1024 * 1024


def _cparams(n_axes):
    return pltpu.CompilerParams(dimension_semantics=("arbitrary",) * n_axes, vmem_limit_bytes=VMEM_LIMIT)


def _split(x):
    hi = x.astype(BF16)
    lo = (x - hi.astype(F32)).astype(BF16)
    return hi, lo


_NN = (((1,), (0,)), ((), ()))
_NT = (((1,), (1,)), ((), ()))
_TN = (((0,), (0,)), ((), ()))


def _dg(a, b, dims):
    return lax.dot_general(a, b, dims, preferred_element_type=F32)


def _dot(a, b, passes=1, dims=_NN):
    if passes == 1:
        return _dg(a.astype(BF16), b.astype(BF16), dims)
    ah, al = _split(a)
    bh, bl = _split(b)
    return _dg(ah, bh, dims) + (_dg(ah, bl, dims) + _dg(al, bh, dims))


def _round_up(x, m):
    return (x + m - 1) // m * m


def _pick_tile(n, prefs):
    for t in prefs:
        if n % t == 0:
            return t
    return n


def _linear_kernel(*refs, passes, has_mix, glu, n_row, n_full, post):
    it = iter(refs)
    x_ref = next(it)
    xp_ref = next(it) if has_mix else None
    mix_ref = next(it) if has_mix else None
    w_refs = [next(it) for _ in range(2 if glu else 1)]
    row_refs = [next(it) for _ in range(n_row)]
    full_refs = [next(it) for _ in range(n_full)]
    o_ref = next(it)
    scr = list(it)
    per = 2 if passes == 3 else 1

    @pl.when(pl.program_id(1) == 0)
    def _():
        for wi, w_ref in enumerate(w_refs):
            w = w_ref[...]
            if passes == 3:
                hi, lo = _split(w)
                scr[wi * per][...] = hi
                scr[wi * per + 1][...] = lo
            else:
                scr[wi * per][...] = w.astype(BF16)

    x = x_ref[...]
    if has_mix:
        x = x + (xp_ref[...] - x) * mix_ref[...]
    if passes == 3:
        xh, xl = _split(x)
    else:
        xh = x.astype(BF16)
    accs = []
    for wi in range(len(w_refs)):
        wh = scr[wi * per][...]
        acc = jnp.dot(xh, wh, preferred_element_type=F32)
        if passes == 3:
            wl = scr[wi * per + 1][...]
            acc = acc + (jnp.dot(xh, wl, preferred_element_type=F32) + jnp.dot(xl, wh, preferred_element_type=F32))
        accs.append(acc)
    extras = [r[...] for r in row_refs] + [r[...] for r in full_refs]
    o_ref[...] = post(*accs, *extras)


def _identity(acc):
    return acc


def linear(x, w, *, passes=1, mix=None, glu=False, rows=(), fulls=(), post=_identity, tm=None, tn=None):
    m, k = x.shape
    n = w.shape[1] // (2 if glu else 1)
    tm = tm or _pick_tile(m, (768, 512, 384, 256, 128, 64, 32, 16, 8))
    if mix is not None:
        tm = _pick_tile(m, (384, 256, 128, 64, 32, 16, 8))
    tn = tn or _pick_tile(n, (512, 256, 128))
    nb = n // tn
    grid = (nb, m // tm)
    x_spec = pl.BlockSpec((tm, k), lambda j, i: (i, 0))
    in_specs = [x_spec]
    args = [x]
    if mix is not None:
        in_specs += [x_spec, pl.BlockSpec((1, k), lambda j, i: (0, 0))]
        args += [mix[0], mix[1].reshape(1, k)]
    in_specs.append(pl.BlockSpec((k, tn), lambda j, i: (0, j)))
    args.append(w)
    if glu:
        in_specs.append(pl.BlockSpec((k, tn), lambda j, i: (0, j + nb)))
        args.append(w)
    for r in rows:
        in_specs.append(pl.BlockSpec((1, tn), lambda j, i: (0, j)))
        args.append(r.reshape(1, n))
    for f in fulls:
        in_specs.append(pl.BlockSpec((tm, tn), lambda j, i: (i, j)))
        args.append(f)
    n_w = 2 if glu else 1
    scratch = [pltpu.VMEM((k, tn), BF16) for _ in range(n_w * (2 if passes == 3 else 1))]
    kern = functools.partial(_linear_kernel, passes=passes, has_mix=mix is not None, glu=glu,
                             n_row=len(rows), n_full=len(fulls), post=post)
    return pl.pallas_call(
        kern,
        grid=grid,
        in_specs=in_specs,
        out_specs=pl.BlockSpec((tm, tn), lambda j, i: (i, j)),
        out_shape=jax.ShapeDtypeStruct((m, n), F32),
        scratch_shapes=scratch,
        compiler_params=_cparams(2),
    )(*args)


def _ln_kernel(*refs, alpha, has_h):
    if has_h:
        x_ref, h_ref, g_ref, b_ref, o_ref = refs
        x = alpha * x_ref[...] + h_ref[...]
    else:
        x_ref, g_ref, b_ref, o_ref = refs
        x = x_ref[...]
    mu = jnp.mean(x, -1, keepdims=True)
    xc = x - mu
    var = jnp.mean(xc * xc, -1, keepdims=True)
    o_ref[...] = xc * lax.rsqrt(var + LN_EPS) * g_ref[...] + b_ref[...]


def layer_norm(x, g, b, h=None, alpha=1.0):
    m, d = x.shape
    tm = _pick_tile(m, (384, 256, 128, 64, 32, 16, 8))
    spec = pl.BlockSpec((tm, d), lambda i: (i, 0))
    vspec = pl.BlockSpec((1, d), lambda i: (0, 0))
    args = [x] + ([h] if h is not None else []) + [g.reshape(1, d), b.reshape(1, d)]
    in_specs = [spec] + ([spec] if h is not None else []) + [vspec, vspec]
    return pl.pallas_call(
        functools.partial(_ln_kernel, alpha=alpha, has_h=h is not None),
        grid=(m // tm,),
        in_specs=in_specs,
        out_specs=spec,
        out_shape=jax.ShapeDtypeStruct((m, d), F32),
        compiler_params=_cparams(1),
    )(*args)


def _bf16r(x):
    return x.astype(BF16).astype(F32)


def _matvec_bf16(s, vec_row):
    return jnp.sum(_bf16r(s) * _bf16r(vec_row), axis=1, keepdims=True)


def _row_from_col(col, eye):
    return jnp.sum(eye * col, axis=0, keepdims=True)


def _col_from_row(row, eye):
    return jnp.sum(eye * row, axis=1, keepdims=True)


def _wkv_out(y, v_h, rk_h, lg, lb):
    mu = jnp.mean(y, -1, keepdims=True)
    yc = y - mu
    var = jnp.mean(yc * yc, -1, keepdims=True)
    return yc * lax.rsqrt(var + GN_EPS) * lg + lb + jnp.sum(rk_h, -1, keepdims=True) * v_h


def _wkv_seq_kernel(r_ref, lw_ref, k_ref, v_ref, a_ref, g_ref, kk_ref, ka_ref, rk_ref, lg_ref, lb_ref, s0_ref,
                    y_ref, s_ref, *, steps, heads):
    N = RW_HEAD
    r = r_ref[0]
    k = k_ref[0]
    v = v_ref[0]
    a = a_ref[0]
    w = jnp.exp(lw_ref[0])
    kkw = k * kk_ref[...]
    k2 = k * (1.0 + (a - 1.0) * ka_ref[...])
    rk = r * k2 * rk_ref[...]
    eye = (lax.broadcasted_iota(jnp.int32, (N, N), 0) == lax.broadcasted_iota(jnp.int32, (N, N), 1)).astype(F32)
    tpad = r.shape[0]
    outs = []
    for h in range(heads):
        sl = slice(h * N, (h + 1) * N)
        kk = kkw[:, sl]
        kk = kk * lax.rsqrt(jnp.maximum(jnp.sum(kk * kk, -1, keepdims=True), 1e-24))
        b_h = kk * a[:, sl]
        s = s0_ref[0, h]
        ys = []
        for t in range(steps):
            sa = _matvec_bf16(s, -kk[t:t + 1])
            s = s * w[t:t + 1, sl] + sa * b_h[t:t + 1] + _col_from_row(v[t:t + 1, sl], eye) * k2[t:t + 1, sl]
            ys.append(_row_from_col(_matvec_bf16(s, r[t:t + 1, sl]), eye))
        s_ref[0, h] = s
        y = jnp.concatenate(ys + [jnp.zeros((tpad - steps, N), F32)], axis=0)
        outs.append(_wkv_out(y, v[:, sl], rk[:, sl], lg_ref[:, sl], lb_ref[:, sl]))
    y_ref[0] = jnp.concatenate(outs, axis=1) * g_ref[0]


def wkv7_seq(r, lw, k, v, a, g, k_k, k_a, r_k, lnx_g, lnx_b, s0, *, steps):
    bsz, t, d = r.shape
    n = RW_HEAD
    hh = d // n
    heads = 4 if hh % 4 == 0 else 2
    wide = heads * n
    seq = pl.BlockSpec((1, t, wide), lambda b, hb: (b, 0, hb))
    vec = pl.BlockSpec((1, wide), lambda b, hb: (0, hb))
    st = pl.BlockSpec((1, heads, n, n), lambda b, hb: (b, hb, 0, 0))
    vecs = [z.reshape(1, d) for z in (k_k, k_a, r_k, lnx_g, lnx_b)]
    return pl.pallas_call(
        functools.partial(_wkv_seq_kernel, steps=steps, heads=heads),
        grid=(bsz, hh // heads),
        in_specs=[seq] * 6 + [vec] * 5 + [st],
        out_specs=[seq, st],
        out_shape=[jax.ShapeDtypeStruct((bsz, t, d), F32), jax.ShapeDtypeStruct((bsz, hh, n, n), F32)],
        compiler_params=_cparams(2),
    )(r, lw, k, v, a, g, *vecs, s0)


def _wkv_kernel(r_ref, lw_ref, k_ref, v_ref, a_ref, g_ref, kk_ref, ka_ref, rk_ref, lg_ref, lb_ref, s0_ref,
                y_ref, s_ref, *, chunk, heads, passes):
    L, N = chunk, RW_HEAD

    @pl.when(pl.program_id(2) == 0)
    def _():
        s_ref[...] = s0_ref[...]

    r = r_ref[0]
    lw = lw_ref[0]
    k = k_ref[0]
    v = v_ref[0]
    a = a_ref[0]
    kkw = k * kk_ref[...]
    k2 = k * (1.0 + (a - 1.0) * ka_ref[...])
    rk = r * k2 * rk_ref[...]

    row = lax.broadcasted_iota(jnp.int32, (L, L), 0)
    col = lax.broadcasted_iota(jnp.int32, (L, L), 1)
    tri_incl = (col <= row)
    tri_strict = (col < row)
    tri = tri_incl.astype(BF16)
    l1 = lw.astype(BF16)
    rem = lw - l1.astype(F32)
    l2 = rem.astype(BF16)
    l3 = (rem - l2.astype(F32)).astype(BF16)
    cs = (jnp.dot(tri, l1, preferred_element_type=F32) + jnp.dot(tri, l2, preferred_element_type=F32)
          + jnp.dot(tri, l3, preferred_element_type=F32))
    cs_last = cs[L - 1:L, :]
    e_prev = jnp.exp(cs - lw)
    e_inv = jnp.exp(-cs)
    e_fwd = jnp.exp(cs)
    e_tail = jnp.exp(cs_last - cs)
    gam = jnp.exp(cs_last)

    n_dbl = max(1, (L - 1).bit_length())
    eye = (lax.broadcasted_iota(jnp.int32, (N, N), 0) == lax.broadcasted_iota(jnp.int32, (N, N), 1)).astype(F32)
    outs = []
    for h in range(heads):
        sl = slice(h * N, (h + 1) * N)
        kk = kkw[:, sl]
        kk = kk * lax.rsqrt(jnp.maximum(jnp.sum(kk * kk, -1, keepdims=True), 1e-24))
        a_h = -kk
        b_h = kk * a[:, sl]
        k_h = k2[:, sl]
        v_h = v[:, sl]
        at = a_h * e_prev[:, sl]
        rt = r[:, sl] * e_fwd[:, sl]
        bt = b_h * e_inv[:, sl]
        kt = k_h * e_inv[:, sl]
        lhs2 = jnp.concatenate([at, rt], axis=0)
        rhs2 = jnp.concatenate([bt, kt], axis=0)
        p = _dot(lhs2, rhs2, passes, _NT)
        a_b = jnp.where(tri_strict, p[:L, :L], 0.0)
        a_k = jnp.where(tri_strict, p[:L, L:], 0.0)
        r_b = jnp.where(tri_incl, p[L:, :L], 0.0)
        r_k = jnp.where(tri_incl, p[L:, L:], 0.0)
        s0 = s_ref[0, h]
        q = _dot(lhs2, s0, passes, _NT)
        x = q[:L] + _dot(a_k, v_h, passes)
        pk = a_b
        for d in range(n_dbl):
            x = x + _dot(pk, x, passes)
            if d + 1 < n_dbl:
                pk = _dot(pk, pk, passes)
        uv = jnp.concatenate([x, v_h], axis=0)
        y = q[L:] + _dot(jnp.concatenate([r_b, r_k], axis=1), uv, passes)
        bk = jnp.concatenate([b_h * e_tail[:, sl], k_h * e_tail[:, sl]], axis=0)
        s_new = s0 * gam[:, sl] + _dot(uv, bk, passes, _TN)
        s_ref[0, h] = s_new
        y_last = _row_from_col(_matvec_bf16(s_new, r[L - 1:L, sl]), eye)
        y = jnp.where(lax.broadcasted_iota(jnp.int32, (L, 1), 0) == L - 1, y_last, y)
        outs.append(_wkv_out(y, v_h, rk[:, sl], lg_ref[:, sl], lb_ref[:, sl]))
    y_ref[0] = jnp.concatenate(outs, axis=1) * g_ref[0]


def wkv7(r, lw, k, v, a, g, k_k, k_a, r_k, lnx_g, lnx_b, s0, *, passes):
    bsz, t, d = r.shape
    n = RW_HEAD
    hh = d // n
    heads = 4 if hh % 4 == 0 else 2
    chunk = 64 if t % 64 == 0 else t
    wide = heads * n
    seq = pl.BlockSpec((1, chunk, wide), lambda b, hb, c: (b, c, hb))
    vec = pl.BlockSpec((1, wide), lambda b, hb, c: (0, hb))
    st = pl.BlockSpec((1, heads, n, n), lambda b, hb, c: (b, hb, 0, 0))
    vecs = [z.reshape(1, d) for z in (k_k, k_a, r_k, lnx_g, lnx_b)]
    return pl.pallas_call(
        functools.partial(_wkv_kernel, chunk=chunk, heads=heads, passes=passes),
        grid=(bsz, hh // heads, t // chunk),
        in_specs=[seq] * 6 + [vec] * 5 + [st],
        out_specs=[seq, st],
        out_shape=[jax.ShapeDtypeStruct((bsz, t, d), F32), jax.ShapeDtypeStruct((bsz, hh, n, n), F32)],
        compiler_params=_cparams(3),
    )(r, lw, k, v, a, g, *vecs, s0)


def _route_kernel(x_ref, rt_ref, bias_ref, w_ref, sel_ref):
    ne = rt_ref.shape[0]
    per = ne // N_GROUPS
    logits = _dot(rt_ref[...], x_ref[...], 1, _NT)
    s = jax.nn.sigmoid(logits)
    sb = s + bias_ref[...]
    tm = sb.shape[1]
    sub = lax.broadcasted_iota(jnp.int32, (per, tm), 0)
    gs_rows = []
    for gi in range(N_GROUPS):
        tile = sb[gi * per:(gi + 1) * per, :]
        m1 = jnp.max(tile, axis=0, keepdims=True)
        first = jnp.min(jnp.where(tile == m1, sub, per), axis=0, keepdims=True)
        m2 = jnp.max(jnp.where(sub == first, -jnp.inf, tile), axis=0, keepdims=True)
        gs_rows.append(m1 + m2)
    keep_rows = []
    for gi in range(N_GROUPS):
        cnt = jnp.zeros((1, tm), jnp.int32)
        for gj in range(N_GROUPS):
            if gj == gi:
                continue
            beats = (gs_rows[gj] > gs_rows[gi]) if gj > gi else (gs_rows[gj] >= gs_rows[gi])
            cnt = cnt + beats.astype(jnp.int32)
        keep_rows.append(jnp.broadcast_to(cnt, (per, tm)))
    masked = jnp.where(jnp.concatenate(keep_rows, axis=0) < TOPK_GROUPS, sb, -1e9)
    eidx = lax.broadcasted_iota(jnp.int32, (ne, tm), 0)
    rank = jnp.zeros((ne, tm), jnp.int32)
    for ej in range(ne):
        rowv = masked[ej:ej + 1, :]
        rank = rank + jnp.where(eidx > ej, (rowv >= masked).astype(jnp.int32), (rowv > masked).astype(jnp.int32))
    sel = rank < TOP_K
    den = jnp.sum(jnp.where(sel, s, 0.0), axis=0, keepdims=True)
    w_ref[...] = jnp.where(sel, s / den * ROUTED_SCALE, 0.0)
    sel_ref[...] = sel.astype(F32)


def route(x, router, bias):
    m, d = x.shape
    ne = router.shape[1]
    tm = _pick_tile(m, (256, 128))
    out = pl.BlockSpec((ne, tm), lambda i: (0, i))
    return pl.pallas_call(
        _route_kernel,
        grid=(m // tm,),
        in_specs=[pl.BlockSpec((tm, d), lambda i: (i, 0)), pl.BlockSpec((ne, d), lambda i: (0, 0)),
                  pl.BlockSpec((ne, 1), lambda i: (0, 0))],
        out_specs=[out, out],
        out_shape=[jax.ShapeDtypeStruct((ne, m), F32)] * 2,
        compiler_params=_cparams(1),
    )(x, router.T, bias.reshape(ne, 1))


def _experts_kernel(be_ref, xs_ref, win_ref, wout_ref, rw_ref, o_ref, win_s, wout_s):
    i = pl.program_id(0)
    f = wout_ref.shape[1]
    changed = jnp.logical_or(i == 0, be_ref[i] != be_ref[jnp.maximum(i - 1, 0)])

    @pl.when(changed)
    def _():
        win_s[...] = win_ref[0].astype(BF16)
        wout_s[...] = wout_ref[0].astype(BF16)

    hu = jnp.dot(xs_ref[...].astype(BF16), win_s[...], preferred_element_type=F32)
    hid = jax.nn.silu(hu[:, :f]) * hu[:, f:]
    y = jnp.dot(hid.astype(BF16), wout_s[...], preferred_element_type=F32)
    d = y.shape[1]
    o_ref[...] = y.astype(BF16).astype(F32) * jnp.tile(rw_ref[...].astype(BF16).astype(F32), (1, d // LANES))


def experts(xs, blk_expert, w_in, w_out, row_w, tm):
    rws, d = xs.shape
    ne, _, f2 = w_in.shape
    f = f2 // 2
    grid_spec = pltpu.PrefetchScalarGridSpec(
        num_scalar_prefetch=1,
        grid=(rws // tm,),
        in_specs=[
            pl.BlockSpec((tm, d), lambda i, be: (i, 0)),
            pl.BlockSpec((1, d, f2), lambda i, be: (be[i], 0, 0)),
            pl.BlockSpec((1, f, d), lambda i, be: (be[i], 0, 0)),
            pl.BlockSpec((tm, LANES), lambda i, be: (i, 0)),
        ],
        out_specs=pl.BlockSpec((tm, d), lambda i, be: (i, 0)),
        scratch_shapes=[pltpu.VMEM((d, f2), BF16), pltpu.VMEM((f, d), BF16)],
    )
    return pl.pallas_call(
        _experts_kernel,
        grid_spec=grid_spec,
        out_shape=jax.ShapeDtypeStruct((rws, d), F32),
        compiler_params=_cparams(1),
    )(blk_expert, xs, w_in, w_out, row_w)


def moe_ffn(x, router, bias, w_in, w_out, ws_in, ws_out):
    m, d = x.shape
    ne = router.shape[1]
    wts, sel = route(x, router, bias)
    tm = 256 if m * TOP_K // ne >= 256 else _round_up(max(m * TOP_K // ne, SUBLANES), SUBLANES)
    nk = m * TOP_K
    e_sorted, n_sorted = jnp.nonzero(sel > 0.5, size=nk)
    counts = jnp.sum(sel, axis=1).astype(jnp.int32)
    padded = (counts + tm - 1) // tm * tm
    pad_end = jnp.cumsum(padded)
    pad_start = pad_end - padded
    start = jnp.cumsum(counts) - counts
    dest = (pad_start[e_sorted] + (jnp.arange(nk, dtype=jnp.int32) - start[e_sorted])).astype(jnp.int32)
    n_blocks = (nk + tm - 1) // tm + ne
    n_rows = n_blocks * tm
    token_of_row = jnp.zeros((n_rows,), jnp.int32).at[dest].set(n_sorted.astype(jnp.int32))
    w_of_row = jnp.zeros((n_rows,), F32).at[dest].set(wts[e_sorted, n_sorted])
    blk_expert = jnp.minimum(jnp.searchsorted(pad_end, jnp.arange(n_blocks, dtype=jnp.int32) * tm, side='right'),
                             ne - 1).astype(jnp.int32)
    order = jnp.argsort(n_sorted, stable=True)
    rows_of_token = dest[order].reshape(m, TOP_K)
    xs = jnp.take(x, token_of_row, axis=0)
    yb = experts(xs, blk_expert, w_in, w_out, jnp.broadcast_to(w_of_row[:, None], (n_rows, LANES)), tm)
    routed = jnp.sum(jnp.take(yb, rows_of_token, axis=0), axis=1)
    hid = linear(x, ws_in, glu=True, post=lambda a, b: jax.nn.silu(a) * b)
    return linear(hid, ws_out, fulls=(routed,), post=lambda acc, rt: acc + rt)


def _gelu_tanh(x):
    return 0.5 * x * (1.0 + jnp.tanh(0.7978845608028654 * (x + 0.044715 * x * x * x)))


def _compress_kernel(tbl_ref, pg_ref, pos_ref, w1_ref, w2_ref, o_ref, *slabs, group):
    p = pl.program_id(2)
    pin = p % group
    for g in range(N_KV):
        slabs[g][pl.ds(pl.multiple_of(pin * PAGE, PAGE), PAGE), :] = pg_ref[0, :, g * HEAD_DIM:(g + 1) * HEAD_DIM]

    @pl.when(pin == group - 1)
    def _():
        per_page = PAGE // CMP_BLOCK
        nrow = group * per_page
        hid = w1_ref.shape[2]
        for g in range(N_KV):
            acc = jnp.zeros((nrow, hid), F32)
            for t in range(CMP_BLOCK):
                lhs = slabs[g][pl.ds(t, nrow, stride=CMP_BLOCK), :] + pos_ref[0, t:t + 1, :]
                acc = acc + _dot(lhs, w1_ref[0, t * HEAD_DIM:(t + 1) * HEAD_DIM, :])
            o_ref[0, 0, :, g * HEAD_DIM:(g + 1) * HEAD_DIM] = _dot(_gelu_tanh(acc), w2_ref[0])


def compress(pages, table, slot0, cmp_pos, cmp_w1, cmp_w2):
    bsz, n_pg = table.shape
    group = min(32, n_pg)
    gw = N_KV * HEAD_DIM
    per_page = PAGE // CMP_BLOCK
    hid = cmp_w1.shape[2]
    grid_spec = pltpu.PrefetchScalarGridSpec(
        num_scalar_prefetch=1,
        grid=(bsz, 2, n_pg),
        in_specs=[
            pl.BlockSpec((1, PAGE, gw), lambda b, s, p, tbl: (tbl[b * n_pg + p], 0, slot0 + s)),
            pl.BlockSpec((1, CMP_BLOCK, HEAD_DIM), lambda b, s, p, tbl: (s, 0, 0)),
            pl.BlockSpec((1, CMP_BLOCK * HEAD_DIM, hid), lambda b, s, p, tbl: (s, 0, 0)),
            pl.BlockSpec((1, hid, HEAD_DIM), lambda b, s, p, tbl: (s, 0, 0)),
        ],
        out_specs=pl.BlockSpec((1, 1, group * per_page, gw), lambda b, s, p, tbl: (b, s, p // group, 0)),
        scratch_shapes=[pltpu.VMEM((group * PAGE, HEAD_DIM), F32) for _ in range(N_KV)],
    )
    return pl.pallas_call(
        functools.partial(_compress_kernel, group=group),
        grid_spec=grid_spec,
        out_shape=jax.ShapeDtypeStruct((bsz, 2, n_pg * per_page, gw), F32),
        compiler_params=_cparams(3),
    )(table.reshape(-1), pages, cmp_pos, cmp_w1, cmp_w2)


def _nsa_cmp_kernel(q_ref, kc_ref, vc_ref, gate_ref, o_ref, sel_ref, score_s, rank_s, *, q_offset, n_sel_blocks,
                    tq, half):
    qt = pl.program_id(1)
    ncp = kc_ref.shape[1]
    nsp = sel_ref.shape[2]
    scale = HEAD_DIM ** -0.5
    rpg = N_KV
    n_heads = q_ref.shape[2] // HEAD_DIM
    rpg = n_heads // N_KV
    qpos = q_offset + qt * tq + lax.broadcasted_iota(jnp.int32, (1, tq), 1)
    cidx = lax.broadcasted_iota(jnp.int32, (ncp, 1), 0)
    corig = jnp.where(cidx < half, 2 * cidx, 2 * (cidx - half) + 1)
    cmask = ((corig + 1) * CMP_BLOCK - 1) <= qpos
    blk = lax.broadcasted_iota(jnp.int32, (nsp, 1), 0)
    cur = qpos // SEL_BLOCK
    forced = (blk == 0) | (blk == cur) | (blk == cur - 1)
    valid = (blk * SEL_BLOCK) <= qpos
    gate = gate_ref[0]
    for g in range(N_KV):
        kc = kc_ref[0, :, g * HEAD_DIM:(g + 1) * HEAD_DIM]
        vc = vc_ref[0, :, g * HEAD_DIM:(g + 1) * HEAD_DIM]
        imp = jnp.zeros((ncp, tq), F32)
        for h in range(rpg):
            hd = g * rpg + h
            qh = q_ref[0, :, hd * HEAD_DIM:(hd + 1) * HEAD_DIM]
            s = _dot(kc, qh, 1, _NT) * scale
            s = jnp.where(cmask, s, NEG)
            mx = jnp.max(s, axis=0, keepdims=True)
            e = jnp.where(cmask, jnp.exp(s - mx), 0.0)
            p = e / jnp.maximum(jnp.sum(e, axis=0, keepdims=True), 1e-30)
            imp = imp + p
            oc = _dot(p, vc, 1, _TN)
            o_ref[0, :, hd * HEAD_DIM:(hd + 1) * HEAD_DIM] = gate[:, hd * 3:hd * 3 + 1] * oc
        imp_blk = imp[:half] + imp[half:]
        if nsp > half:
            imp_blk = jnp.concatenate([imp_blk, jnp.zeros((nsp - half, tq), F32)], axis=0)
        else:
            imp_blk = imp_blk[:nsp]
        score = jnp.where(valid, imp_blk + FORCED_BONUS * forced.astype(F32), -1.0)
        score = jnp.where(blk < n_sel_blocks, score, -2.0)
        score_s[...] = score
        rank_s[...] = jnp.zeros((nsp, tq), jnp.int32)

        def body(j, carry):
            rowv = score_s[pl.ds(j, 1), :]
            rank_s[...] = rank_s[...] + jnp.where(blk > j, (rowv >= score).astype(jnp.int32),
                                                  (rowv > score).astype(jnp.int32))
            return carry

        lax.fori_loop(0, n_sel_blocks, body, 0)
        sel_ref[0, g] = (rank_s[...] < N_SEL).astype(F32)


def nsa_cmp(q, kc, vc, gates, *, q_offset, n_sel_blocks, tq):
    bsz, t, dq = q.shape
    ncp = kc.shape[1]
    half = ncp // 2
    nsp = _round_up(n_sel_blocks, SUBLANES)
    gw = N_KV * HEAD_DIM
    qspec = pl.BlockSpec((1, tq, dq), lambda b, i: (b, i, 0))
    cspec = pl.BlockSpec((1, ncp, gw), lambda b, i: (b, 0, 0))
    return pl.pallas_call(
        functools.partial(_nsa_cmp_kernel, q_offset=q_offset, n_sel_blocks=n_sel_blocks, tq=tq, half=half),
        grid=(bsz, t // tq),
        in_specs=[qspec, cspec, cspec, pl.BlockSpec((1, tq, gates.shape[2]), lambda b, i: (b, i, 0))],
        out_specs=[qspec, pl.BlockSpec((1, N_KV, nsp, tq), lambda b, i: (b, 0, 0, i))],
        out_shape=[jax.ShapeDtypeStruct((bsz, t, dq), F32), jax.ShapeDtypeStruct((bsz, N_KV, nsp, t), F32)],
        scratch_shapes=[pltpu.VMEM((nsp, tq), F32), pltpu.VMEM((nsp, tq), jnp.int32)],
        compiler_params=_cparams(2),
    )(q, kc, vc, gates)


def _nsa_attn_kernel(*refs, mode, q_offset, tq, n_kt, has_tail, tail_pos0, branch):
    tbl_ref, kp0_ref = refs[0], refs[1]
    it = iter(refs[2:])
    q_ref = next(it)
    k_ref = next(it)
    v_ref = next(it)
    tk_ref = next(it) if has_tail else None
    tv_ref = next(it) if has_tail else None
    sel_ref = next(it) if mode == "sel" else None
    gate_ref = next(it)
    prev_ref = next(it)
    o_ref = next(it)
    m_s, l_s, acc_s = next(it), next(it), next(it)

    qt = pl.program_id(1)
    j = pl.program_id(2)
    n_heads = q_ref.shape[2] // HEAD_DIM
    rpg = n_heads // N_KV
    rows = rpg * tq
    scale = HEAD_DIM ** -0.5
    width = WINDOW if mode == "win" else (1 << 30)

    @pl.when(j == 0)
    def _():
        m_s[...] = jnp.full(m_s.shape, NEG, F32)
        l_s[...] = jnp.zeros(l_s.shape, F32)
        acc_s[...] = jnp.zeros(acc_s.shape, F32)

    q_first = q_offset + qt * tq
    qpos = q_first + lax.broadcasted_iota(jnp.int32, (tq, 1), 0)

    def tile(kt_ref, vt_ref, kp0):
        kpos = kp0 + lax.broadcasted_iota(jnp.int32, (1, PAGE), 1)
        dist = qpos - kpos
        band = (dist >= 0) & (dist < width)
        for g in range(N_KV):
            mask = band
            if mode == "sel":
                nsp = sel_ref.shape[2]
                blk = lax.broadcasted_iota(jnp.int32, (nsp, 1), 0)
                expand = (blk == kpos // SEL_BLOCK).astype(BF16)
                hit = _dg(sel_ref[0, g].astype(BF16), expand, _TN)
                mask = band & (hit > 0.5)
            mask4 = jnp.concatenate([mask] * rpg, axis=0)
            qg = jnp.concatenate(
                [q_ref[0, :, (g * rpg + h) * HEAD_DIM:(g * rpg + h + 1) * HEAD_DIM] for h in range(rpg)], axis=0)
            kg = kt_ref[0, :, g * HEAD_DIM:(g + 1) * HEAD_DIM]
            vg = vt_ref[0, :, g * HEAD_DIM:(g + 1) * HEAD_DIM]
            s = _dot(qg, kg, 1, _NT) * scale
            s = jnp.where(mask4, s, NEG)
            m_old = m_s[g]
            m_new = jnp.maximum(m_old, jnp.max(s, -1, keepdims=True))
            alpha = jnp.exp(m_old - m_new)
            p = jnp.where(mask4, jnp.exp(s - m_new), 0.0)
            l_s[g] = alpha * l_s[g] + jnp.sum(p, -1, keepdims=True)
            acc_s[g] = alpha * acc_s[g] + _dot(p, vg, 1)
            m_s[g] = m_new

    idx = (pl.program_id(0) * pl.num_programs(1) + qt) * n_kt + jnp.minimum(j, n_kt - 1)
    kp0 = kp0_ref[idx]
    live = (kp0 >= 0) & (kp0 <= q_first + tq - 1) & (kp0 + PAGE - 1 > q_first - width)
    if has_tail:
        live = live & (j < n_kt)

    @pl.when(live)
    def _():
        tile(k_ref, v_ref, kp0)

    if has_tail:
        @pl.when(j == n_kt)
        def _():
            tile(tk_ref, tv_ref, tail_pos0)

    @pl.when(j == pl.num_programs(2) - 1)
    def _():
        gate = gate_ref[0]
        for g in range(N_KV):
            o = acc_s[g] / jnp.maximum(l_s[g], 1e-30)
            for h in range(rpg):
                hd = g * rpg + h
                cs = slice(hd * HEAD_DIM, (hd + 1) * HEAD_DIM)
                o_ref[0, :, cs] = prev_ref[0, :, cs] + gate[:, hd * 3 + branch:hd * 3 + branch + 1] * o[h * tq:(h + 1) * tq]


def nsa_attn(q, pages, tbl, kp0, slot_k, slot_v, tail, sel, gates, prev, *, mode, q_offset, tq, tail_pos0, branch):
    bsz, t, dq = q.shape
    n_qt = t // tq
    n_kt = tbl.shape[2]
    gw = N_KV * HEAD_DIM
    n_heads = dq // HEAD_DIM
    rows = (n_heads // N_KV) * tq
    has_tail = tail is not None
    n_steps = n_kt + (1 if has_tail else 0)

    def page_idx(b, i, j, tb, kp):
        return tb[(b * n_qt + i) * n_kt + jnp.minimum(j, n_kt - 1)]

    qspec = pl.BlockSpec((1, tq, dq), lambda b, i, j, tb, kp: (b, i, 0))
    in_specs = [
        qspec,
        pl.BlockSpec((1, PAGE, gw), lambda b, i, j, tb, kp: (page_idx(b, i, j, tb, kp), 0, slot_k)),
        pl.BlockSpec((1, PAGE, gw), lambda b, i, j, tb, kp: (page_idx(b, i, j, tb, kp), 0, slot_v)),
    ]
    args = [q, pages, pages]
    if has_tail:
        in_specs += [pl.BlockSpec((1, PAGE, gw), lambda b, i, j, tb, kp: (b, 0, 0)),
                     pl.BlockSpec((1, PAGE, gw), lambda b, i, j, tb, kp: (b, 0, 1))]
        args += [tail, tail]
    if mode == "sel":
        nsp = sel.shape[2]
        in_specs.append(pl.BlockSpec((1, N_KV, nsp, tq), lambda b, i, j, tb, kp: (b, 0, 0, i)))
        args.append(sel)
    in_specs += [pl.BlockSpec((1, tq, gates.shape[2]), lambda b, i, j, tb, kp: (b, i, 0)), qspec]
    args += [gates, prev]
    grid_spec = pltpu.PrefetchScalarGridSpec(
        num_scalar_prefetch=2,
        grid=(bsz, n_qt, n_steps),
        in_specs=in_specs,
        out_specs=qspec,
        scratch_shapes=[pltpu.VMEM((N_KV, rows, 1), F32), pltpu.VMEM((N_KV, rows, 1), F32),
                        pltpu.VMEM((N_KV, rows, HEAD_DIM), F32)],
    )
    return pl.pallas_call(
        functools.partial(_nsa_attn_kernel, mode=mode, q_offset=q_offset, tq=tq, n_kt=n_kt, has_tail=has_tail,
                          tail_pos0=tail_pos0, branch=branch),
        grid_spec=grid_spec,
        out_shape=jax.ShapeDtypeStruct((bsz, t, dq), F32),
        compiler_params=_cparams(3),
    )(tbl.reshape(-1).astype(jnp.int32), kp0.reshape(-1).astype(jnp.int32), *args)


def _softplus(z):
    return jnp.maximum(z, 0.0) + jnp.log(1.0 + jnp.exp(-jnp.abs(z)))


def _post_logdecay(acc, w0):
    return -jnp.exp(-_softplus(-(w0 + acc)) - 0.5)


def _post_sig_bias(acc, b):
    return jax.nn.sigmoid(acc + b)


def _post_vres(acc, v0, v, vf):
    return v + (vf - v) * jax.nn.sigmoid(v0 + acc)


def _post_ple(acc, b, x, t):
    return x + jax.nn.sigmoid(acc + b) * t


def _pad_rows(x, m):
    return jnp.pad(x, ((0, m - x.shape[0]),) + ((0, 0),) * (x.ndim - 1))


def kernel(x_prompt, x_sample, cache_kv_pages, cache_win_kv, state_wkv, state_shift, page_table, p_prompt, p_sample, ln_g, ln_b, rw_mix, rw_w_rkv, rw_w0, rw_w1, rw_w2, rw_a0, rw_a1, rw_a2, rw_v0, rw_v1, rw_v2, rw_g1, rw_g2, rw_k_k, rw_k_a, rw_r_k, rw_lnx_g, rw_lnx_b, rw_w_o, kv_ln_g, kv_ln_b, w_kv, cmp_pos, cmp_w1, cmp_w2, nsa_w_qg, nsa_b_g, nsa_w_o, moe_router, moe_bias, moe_w_in, moe_w_out, moe_ws_in, moe_ws_out, ple_proj, ple_gate, ple_gate_b):
    bp, tp, d = x_prompt.shape
    bs, ts, _ = x_sample.shape
    depth = ln_g.shape[0]
    n_a = state_wkv.shape[0]
    alpha = (2 * depth) ** 0.25
    n_p, n_s = bp * tp, bs * ts
    n_tok = n_p + n_s
    m = _round_up(n_tok, 768) if n_tok > 768 else _round_up(n_tok, SUBLANES)
    ts_pad = _round_up(ts, SUBLANES)
    hh = d // RW_HEAD
    gw = N_KV * HEAD_DIM
    n_pg_s = page_table.shape[1]
    past_len = n_pg_s * cache_kv_pages.shape[1]
    win_buf = cache_win_kv.shape[1]
    assert cache_kv_pages.shape[1] == PAGE and tp % PAGE == 0 and win_buf % PAGE == 0 and ts < CMP_BLOCK

    def join(a_p, a_s):
        return _pad_rows(jnp.concatenate([a_p.reshape(n_p, -1), a_s.reshape(n_s, -1)], axis=0), m)

    def split_seq(z):
        zp = z[:n_p].reshape(bp, tp, -1)
        zs = jnp.pad(z[n_p:n_tok].reshape(bs, ts, -1), ((0, 0), (0, ts_pad - ts), (0, 0)))
        return zp, zs

    def merge_seq(zp, zs):
        return join(zp, zs[:, :ts])

    x = join(x_prompt, x_sample)
    pe = jnp.concatenate([p_prompt.reshape(depth, n_p, -1), p_sample.reshape(depth, n_s, -1)], axis=1)
    pe = jnp.pad(pe, ((0, 0), (0, m - n_tok), (0, 0)))

    shifts_p, shifts_s, wkvs_p, wkvs_s = [], [], [], []
    v_first = None
    ctx = None
    kv_rows_p = kv_rows_s = win_p = win_s = None
    for i in range(depth):
        if i < n_a:
            scan_passes = 3 if i == 0 else 1
            xp_, xs_ = x[:n_p].reshape(bp, tp, d), x[n_p:n_tok].reshape(bs, ts, d)
            shifts_p.append(xp_[:, -1])
            shifts_s.append(xs_[:, -1])
            prev = join(jnp.concatenate([jnp.zeros((bp, 1, d), F32), xp_[:, :-1]], axis=1),
                        jnp.concatenate([state_shift[i][:, None, :], xs_[:, :-1]], axis=1))
            mix = rw_mix[i]
            lin = linear
            r = lin(x, rw_w_rkv[i, 0], mix=(prev, mix[0]))
            k = lin(x, rw_w_rkv[i, 1], mix=(prev, mix[2]))
            v = lin(x, rw_w_rkv[i, 2], mix=(prev, mix[3]))
            lw = lin(lin(x, rw_w1[i], mix=(prev, mix[1]), post=jnp.tanh), rw_w2[i], rows=(rw_w0[i],),
                     post=_post_logdecay)
            if i == 0:
                v_first = v
            else:
                v = lin(lin(x, rw_v1[i - 1], mix=(prev, mix[3])), rw_v2[i - 1], rows=(rw_v0[i - 1],),
                        fulls=(v, v_first), post=_post_vres)
            a = lin(lin(x, rw_a1[i], mix=(prev, mix[4])), rw_a2[i], rows=(rw_a0[i],), post=_post_sig_bias)
            g = lin(lin(x, rw_g1[i], mix=(prev, mix[5]), post=jax.nn.sigmoid), rw_g2[i])
            seqs = [split_seq(z) for z in (r, lw, k, v, a, g)]
            vecs = (rw_k_k[i], rw_k_a[i], rw_r_k[i].reshape(-1), rw_lnx_g[i], rw_lnx_b[i])
            y_p, s_p = wkv7(*[z[0] for z in seqs], *vecs, jnp.zeros((bp, hh, RW_HEAD, RW_HEAD), F32),
                            passes=scan_passes)
            y_s, s_s = wkv7_seq(*[z[1] for z in seqs], *vecs, state_wkv[i], steps=ts)
            wkvs_p.append(s_p)
            wkvs_s.append(s_s)
            h = lin(merge_seq(y_p, y_s), rw_w_o[i])
        else:
            jn = i - n_a
            if ctx is None:
                rows = linear(layer_norm(x, kv_ln_g, kv_ln_b), w_kv)
                rows_p = rows[:n_p].reshape(bp, tp, 6 * gw)
                rows_s = rows[n_p:n_tok].reshape(bs, ts, 6 * gw)
                kv_rows_p = rows_p[:, :, :4 * gw].reshape(bp, tp, 4, N_KV, HEAD_DIM)
                kv_rows_s = rows_s[:, :, :4 * gw].reshape(bs, ts, 4, N_KV, HEAD_DIM)
                win_p = rows_p[:, max(tp - WINDOW, 0):, 4 * gw:].reshape(bp, -1, 2, N_KV, HEAD_DIM)
                new_win = rows_s[:, :, 4 * gw:].reshape(bs, ts, 2, N_KV, HEAD_DIM)
                win_s = jnp.concatenate([cache_win_kv, new_win], axis=1)[:, ts:]
                n_pg_p = tp // PAGE
                pages_p = rows[:n_p].reshape(bp * n_pg_p, PAGE, 6 * gw)
                table_p = jnp.arange(bp * n_pg_p, dtype=jnp.int32).reshape(bp, n_pg_p)
                pages_s = cache_kv_pages.reshape(cache_kv_pages.shape[0], PAGE, 4 * gw)
                pages_w = cache_win_kv.reshape(bs * (win_buf // PAGE), PAGE, 2 * gw)

                def perm(c):
                    return jnp.concatenate([c[:, 0::2], c[:, 1::2]], axis=1)

                cmp_p = compress(pages_p, table_p, 0, cmp_pos, cmp_w1, cmp_w2)
                cmp_s = compress(pages_s, page_table.astype(jnp.int32), 0, cmp_pos, cmp_w1, cmp_w2)
                tail_sel = jnp.pad(rows_s[:, :, 2 * gw:4 * gw], ((0, 0), (0, PAGE - ts), (0, 0)))
                tail_win = jnp.pad(rows_s[:, :, 4 * gw:6 * gw], ((0, 0), (0, PAGE - ts), (0, 0)))
                tq_p = PAGE
                n_qt = tp // tq_p
                qi = jnp.arange(n_qt, dtype=jnp.int32)[:, None]
                kj = jnp.arange(n_pg_p, dtype=jnp.int32)[None, :]
                base = (jnp.arange(bp, dtype=jnp.int32) * n_pg_p)[:, None, None]
                tbl_sel_p = base + jnp.minimum(kj, qi)[None]
                kp0_sel_p = jnp.broadcast_to(jnp.where(kj <= qi, kj * PAGE, -1)[None], tbl_sel_p.shape)
                n_wt = WINDOW // PAGE + 1
                kw = qi - (n_wt - 1) + jnp.arange(n_wt, dtype=jnp.int32)[None, :]
                tbl_win_p = base + jnp.maximum(kw, 0)[None]
                kp0_win_p = jnp.broadcast_to(jnp.where(kw >= 0, kw * PAGE, -1)[None], tbl_win_p.shape)
                tbl_sel_s = page_table.astype(jnp.int32)[:, None, :]
                kp0_sel_s = jnp.broadcast_to((jnp.arange(n_pg_s, dtype=jnp.int32) * PAGE)[None, None], tbl_sel_s.shape)
                n_ww = win_buf // PAGE
                tbl_win_s = jnp.arange(bs * n_ww, dtype=jnp.int32).reshape(bs, 1, n_ww)
                kp0_win_s = jnp.broadcast_to(
                    (past_len - win_buf + jnp.arange(n_ww, dtype=jnp.int32) * PAGE)[None, None], tbl_win_s.shape)
                ctx = dict(kc_p=perm(cmp_p[:, 0]), vc_p=perm(cmp_p[:, 1]), kc_s=perm(cmp_s[:, 0]), vc_s=perm(cmp_s[:, 1]))
            dq = nsa_w_qg.shape[2] - nsa_b_g.shape[1]
            q = linear(x, nsa_w_qg[jn][:, :dq])
            gates = linear(x, nsa_w_qg[jn][:, dq:], rows=(nsa_b_g[jn],), post=_post_sig_bias)
            (q_p, q_s), (g_p, g_s) = split_seq(q), split_seq(gates)
            ns_p = -(-tp // SEL_BLOCK)
            ns_s = -(-(past_len + ts) // SEL_BLOCK)
            o_p, sel_p = nsa_cmp(q_p, ctx['kc_p'], ctx['vc_p'], g_p, q_offset=0, n_sel_blocks=ns_p, tq=tq_p)
            o_p = nsa_attn(q_p, pages_p, tbl_sel_p, kp0_sel_p, 2, 3, None, sel_p, g_p, o_p, mode="sel", q_offset=0,
                           tq=tq_p, tail_pos0=0, branch=1)
            o_p = nsa_attn(q_p, pages_p, tbl_win_p, kp0_win_p, 4, 5, None, None, g_p, o_p, mode="win", q_offset=0,
                           tq=tq_p, tail_pos0=0, branch=2)
            o_s, sel_s = nsa_cmp(q_s, ctx['kc_s'], ctx['vc_s'], g_s, q_offset=past_len, n_sel_blocks=ns_s, tq=ts_pad)
            o_s = nsa_attn(q_s, pages_s, tbl_sel_s, kp0_sel_s, 2, 3, tail_sel, sel_s, g_s, o_s, mode="sel",
                           q_offset=past_len, tq=ts_pad, tail_pos0=past_len, branch=1)
            o_s = nsa_attn(q_s, pages_w, tbl_win_s, kp0_win_s, 0, 1, tail_win, None, g_s, o_s, mode="win",
                           q_offset=past_len, tq=ts_pad, tail_pos0=past_len, branch=2)
            h = linear(merge_seq(o_p, o_s), nsa_w_o[jn])
        x = layer_norm(x, ln_g[i, 0], ln_b[i, 0], h=h, alpha=alpha)
        mo = moe_ffn(x, moe_router[i], moe_bias[i], moe_w_in[i], moe_w_out[i], moe_ws_in[i], moe_ws_out[i])
        x = layer_norm(x, ln_g[i, 1], ln_b[i, 1], h=mo, alpha=alpha)
        t = linear(pe[i], ple_proj[i])
        x = linear(x, ple_gate[i], rows=(ple_gate_b[i],), fulls=(x, t), post=_post_ple)

    y_p = x[:n_p].reshape(bp, tp, d)
    y_s = x[n_p:n_tok].reshape(bs, ts, d)
    return (y_p, y_s, kv_rows_p, kv_rows_s, win_p, win_s, jnp.stack(wkvs_p), jnp.stack(wkvs_s),
            jnp.stack(shifts_p), jnp.stack(shifts_s))
```

```python
import functools

import jax
import jax.numpy as jnp
from jax import lax
from jax.experimental import pallas as pl
from jax.experimental.pallas import tpu as pltpu

F32 = jnp.float32
BF16 = jnp.bfloat16

RW_HEAD = 64
GN_EPS = 64e-5
N_KV = 4
HEAD_DIM = 128
CMP_BLOCK = 32
SEL_BLOCK = 64
N_SEL = 16
WINDOW = 512
FORCED_BONUS = 100.0
N_GROUPS = 8
TOPK_GROUPS = 4
TOP_K = 8
ROUTED_SCALE = 2.5
LN_EPS = 1e-5
NEG = -1e30

LANES = 128
SUBLANES = 8
PAGE = 128
VMEM_LIMIT = 56 * 1024 * 1024


def _cparams(n_axes):
    return pltpu.CompilerParams(dimension_semantics=("arbitrary",) * n_axes, vmem_limit_bytes=VMEM_LIMIT)


def _split(x):
    hi = x.astype(BF16)
    lo = (x - hi.astype(F32)).astype(BF16)
    return hi, lo


_NN = (((1,), (0,)), ((), ()))
_NT = (((1,), (1,)), ((), ()))
_TN = (((0,), (0,)), ((), ()))


def _dg(a, b, dims):
    return lax.dot_general(a, b, dims, preferred_element_type=F32)


def _dot(a, b, passes=1, dims=_NN):
    if passes == 1:
        return _dg(a.astype(BF16), b.astype(BF16), dims)
    ah, al = _split(a)
    bh, bl = _split(b)
    return _dg(ah, bh, dims) + (_dg(ah, bl, dims) + _dg(al, bh, dims))


def _round_up(x, m):
    return (x + m - 1) // m * m


def _pick_tile(n, prefs):
    for t in prefs:
        if n % t == 0:
            return t
    return n


def _linear_kernel(*refs, passes, has_mix, glu, n_row, n_full, post):
    it = iter(refs)
    x_ref = next(it)
    xp_ref = next(it) if has_mix else None
    mix_ref = next(it) if has_mix else None
    w_refs = [next(it) for _ in range(2 if glu else 1)]
    row_refs = [next(it) for _ in range(n_row)]
    full_refs = [next(it) for _ in range(n_full)]
    o_ref = next(it)
    scr = list(it)
    per = 2 if passes == 3 else 1

    @pl.when(pl.program_id(1) == 0)
    def _():
        for wi, w_ref in enumerate(w_refs):
            w = w_ref[...]
            if passes == 3:
                hi, lo = _split(w)
                scr[wi * per][...] = hi
                scr[wi * per + 1][...] = lo
            else:
                scr[wi * per][...] = w.astype(BF16)

    x = x_ref[...]
    if has_mix:
        x = x + (xp_ref[...] - x) * mix_ref[...]
    if passes == 3:
        xh, xl = _split(x)
    else:
        xh = x.astype(BF16)
    accs = []
    for wi in range(len(w_refs)):
        wh = scr[wi * per][...]
        acc = jnp.dot(xh, wh, preferred_element_type=F32)
        if passes == 3:
            wl = scr[wi * per + 1][...]
            acc = acc + (jnp.dot(xh, wl, preferred_element_type=F32) + jnp.dot(xl, wh, preferred_element_type=F32))
        accs.append(acc)
    extras = [r[...] for r in row_refs] + [r[...] for r in full_refs]
    o_ref[...] = post(*accs, *extras)


def _identity(acc):
    return acc


def linear(x, w, *, xidx=(), widx=(), ncols=None, passes=1, mix=None, glu=False, rows=(), fulls=(), post=_identity,
           tm=None, tn=None):
    m, k = x.shape[-2:]
    n = (ncols or w.shape[-1]) // (2 if glu else 1)
    lead = (None,) * len(widx)
    tm = tm or _pick_tile(m, (768, 512, 384, 256, 128, 64, 32, 16, 8))
    if mix is not None:
        tm = _pick_tile(m, (384, 256, 128, 64, 32, 16, 8))
    tn = tn or _pick_tile(n, (512, 256, 128))
    nb = n // tn
    grid = (nb, m // tm)
    x_spec = pl.BlockSpec((None,) * len(xidx) + (tm, k), lambda j, i: xidx + (i, 0))
    in_specs = [x_spec]
    args = [x]
    if mix is not None:
        in_specs += [x_spec, pl.BlockSpec((1, k), lambda j, i: (0, 0))]
        args += [mix[0], mix[1].reshape(1, k)]
    in_specs.append(pl.BlockSpec(lead + (k, tn), lambda j, i: widx + (0, j)))
    args.append(w)
    if glu:
        in_specs.append(pl.BlockSpec(lead + (k, tn), lambda j, i: widx + (0, j + nb)))
        args.append(w)
    for r in rows:
        in_specs.append(pl.BlockSpec((1, tn), lambda j, i: (0, j)))
        args.append(r.reshape(1, n))
    for f in fulls:
        in_specs.append(pl.BlockSpec((tm, tn), lambda j, i: (i, j)))
        args.append(f)
    n_w = 2 if glu else 1
    scratch = [pltpu.VMEM((k, tn), BF16) for _ in range(n_w * (2 if passes == 3 else 1))]
    kern = functools.partial(_linear_kernel, passes=passes, has_mix=mix is not None, glu=glu,
                             n_row=len(rows), n_full=len(fulls), post=post)
    return pl.pallas_call(
        kern,
        grid=grid,
        in_specs=in_specs,
        out_specs=pl.BlockSpec((tm, tn), lambda j, i: (i, j)),
        out_shape=jax.ShapeDtypeStruct((m, n), F32),
        scratch_shapes=scratch,
        compiler_params=_cparams(2),
    )(*args)


def _ln_kernel(*refs, alpha, has_h):
    if has_h:
        x_ref, h_ref, g_ref, b_ref, o_ref = refs
        x = alpha * x_ref[...] + h_ref[...]
    else:
        x_ref, g_ref, b_ref, o_ref = refs
        x = x_ref[...]
    mu = jnp.mean(x, -1, keepdims=True)
    xc = x - mu
    var = jnp.mean(xc * xc, -1, keepdims=True)
    o_ref[...] = xc * lax.rsqrt(var + LN_EPS) * g_ref[...] + b_ref[...]


def layer_norm(x, g, b, h=None, alpha=1.0):
    m, d = x.shape
    tm = _pick_tile(m, (384, 256, 128, 64, 32, 16, 8))
    spec = pl.BlockSpec((tm, d), lambda i: (i, 0))
    vspec = pl.BlockSpec((1, d), lambda i: (0, 0))
    args = [x] + ([h] if h is not None else []) + [g.reshape(1, d), b.reshape(1, d)]
    in_specs = [spec] + ([spec] if h is not None else []) + [vspec, vspec]
    return pl.pallas_call(
        functools.partial(_ln_kernel, alpha=alpha, has_h=h is not None),
        grid=(m // tm,),
        in_specs=in_specs,
        out_specs=spec,
        out_shape=jax.ShapeDtypeStruct((m, d), F32),
        compiler_params=_cparams(1),
    )(*args)


def _bf16r(x):
    return x.astype(BF16).astype(F32)


def _matvec_bf16(s, vec_row):
    return jnp.sum(_bf16r(s) * _bf16r(vec_row), axis=1, keepdims=True)


def _row_from_col(col, eye):
    return jnp.sum(eye * col, axis=0, keepdims=True)


def _col_from_row(row, eye):
    return jnp.sum(eye * row, axis=1, keepdims=True)


def _wkv_out(y, v_h, rk_h, lg, lb):
    mu = jnp.mean(y, -1, keepdims=True)
    yc = y - mu
    var = jnp.mean(yc * yc, -1, keepdims=True)
    return yc * lax.rsqrt(var + GN_EPS) * lg + lb + jnp.sum(rk_h, -1, keepdims=True) * v_h


def _wkv_seq_kernel(r_ref, lw_ref, k_ref, v_ref, a_ref, g_ref, kk_ref, ka_ref, rk_ref, lg_ref, lb_ref, s0_ref,
                    y_ref, s_ref, *, steps, heads):
    N = RW_HEAD
    r = r_ref[0]
    k = k_ref[0]
    v = v_ref[0]
    a = a_ref[0]
    w = jnp.exp(lw_ref[0])
    kkw = k * kk_ref[...]
    k2 = k * (1.0 + (a - 1.0) * ka_ref[...])
    rk = r * k2 * rk_ref[...]
    eye = (lax.broadcasted_iota(jnp.int32, (N, N), 0) == lax.broadcasted_iota(jnp.int32, (N, N), 1)).astype(F32)
    tpad = r.shape[0]
    outs = []
    for h in range(heads):
        sl = slice(h * N, (h + 1) * N)
        kk = kkw[:, sl]
        kk = kk * lax.rsqrt(jnp.maximum(jnp.sum(kk * kk, -1, keepdims=True), 1e-24))
        b_h = kk * a[:, sl]
        s = s0_ref[0, h]
        ys = []
        for t in range(steps):
            sa = _matvec_bf16(s, -kk[t:t + 1])
            s = s * w[t:t + 1, sl] + sa * b_h[t:t + 1] + _col_from_row(v[t:t + 1, sl], eye) * k2[t:t + 1, sl]
            ys.append(_row_from_col(_matvec_bf16(s, r[t:t + 1, sl]), eye))
        s_ref[0, h] = s
        y = jnp.concatenate(ys + [jnp.zeros((tpad - steps, N), F32)], axis=0)
        outs.append(_wkv_out(y, v[:, sl], rk[:, sl], lg_ref[:, sl], lb_ref[:, sl]))
    y_ref[0] = jnp.concatenate(outs, axis=1) * g_ref[0]


def wkv7_seq(r, lw, k, v, a, g, k_k, k_a, r_k, lnx_g, lnx_b, s0, *, steps):
    bsz, t, d = r.shape
    n = RW_HEAD
    hh = d // n
    heads = 4 if hh % 4 == 0 else 2
    wide = heads * n
    seq = pl.BlockSpec((1, t, wide), lambda b, hb: (b, 0, hb))
    vec = pl.BlockSpec((1, wide), lambda b, hb: (0, hb))
    st = pl.BlockSpec((1, heads, n, n), lambda b, hb: (b, hb, 0, 0))
    vecs = [z.reshape(1, d) for z in (k_k, k_a, r_k, lnx_g, lnx_b)]
    return pl.pallas_call(
        functools.partial(_wkv_seq_kernel, steps=steps, heads=heads),
        grid=(bsz, hh // heads),
        in_specs=[seq] * 6 + [vec] * 5 + [st],
        out_specs=[seq, st],
        out_shape=[jax.ShapeDtypeStruct((bsz, t, d), F32), jax.ShapeDtypeStruct((bsz, hh, n, n), F32)],
        compiler_params=_cparams(2),
    )(r, lw, k, v, a, g, *vecs, s0)


def _wkv_kernel(r_ref, lw_ref, k_ref, v_ref, a_ref, g_ref, kk_ref, ka_ref, rk_ref, lg_ref, lb_ref, s0_ref,
                y_ref, s_ref, *, chunk, heads, passes):
    L, N = chunk, RW_HEAD

    @pl.when(pl.program_id(2) == 0)
    def _():
        s_ref[...] = s0_ref[...]

    r = r_ref[0]
    lw = lw_ref[0]
    k = k_ref[0]
    v = v_ref[0]
    a = a_ref[0]
    kkw = k * kk_ref[...]
    k2 = k * (1.0 + (a - 1.0) * ka_ref[...])
    rk = r * k2 * rk_ref[...]

    row = lax.broadcasted_iota(jnp.int32, (L, L), 0)
    col = lax.broadcasted_iota(jnp.int32, (L, L), 1)
    tri_incl = (col <= row)
    tri_strict = (col < row)
    tri = tri_incl.astype(BF16)
    l1 = lw.astype(BF16)
    rem = lw - l1.astype(F32)
    l2 = rem.astype(BF16)
    l3 = (rem - l2.astype(F32)).astype(BF16)
    cs = (jnp.dot(tri, l1, preferred_element_type=F32) + jnp.dot(tri, l2, preferred_element_type=F32)
          + jnp.dot(tri, l3, preferred_element_type=F32))
    cs_last = cs[L - 1:L, :]
    e_prev = jnp.exp(cs - lw)
    e_inv = jnp.exp(-cs)
    e_fwd = jnp.exp(cs)
    e_tail = jnp.exp(cs_last - cs)
    gam = jnp.exp(cs_last)

    n_dbl = max(1, (L - 1).bit_length())
    eye = (lax.broadcasted_iota(jnp.int32, (N, N), 0) == lax.broadcasted_iota(jnp.int32, (N, N), 1)).astype(F32)
    H = range(heads)
    sls = [slice(h * N, (h + 1) * N) for h in H]
    kk = [kkw[:, sl] for sl in sls]
    kk = [z * lax.rsqrt(jnp.maximum(jnp.sum(z * z, -1, keepdims=True), 1e-24)) for z in kk]
    b_h = [kk[h] * a[:, sls[h]] for h in H]
    k_h = [k2[:, sl] for sl in sls]
    v_h = [v[:, sl] for sl in sls]
    lhs2 = [jnp.concatenate([-kk[h] * e_prev[:, sls[h]], r[:, sls[h]] * e_fwd[:, sls[h]]], axis=0) for h in H]
    rhs2 = [jnp.concatenate([b_h[h] * e_inv[:, sls[h]], k_h[h] * e_inv[:, sls[h]]], axis=0) for h in H]
    bk = [jnp.concatenate([b_h[h] * e_tail[:, sls[h]], k_h[h] * e_tail[:, sls[h]]], axis=0) for h in H]
    s0 = [s_ref[0, h] for h in H]
    p = [_dot(lhs2[h], rhs2[h], passes, _NT) for h in H]
    q = [_dot(lhs2[h], s0[h], passes, _NT) for h in H]
    pk = [jnp.where(tri_strict, p[h][:L, :L], 0.0) for h in H]
    a_k = [jnp.where(tri_strict, p[h][:L, L:], 0.0) for h in H]
    rbk = [jnp.concatenate([jnp.where(tri_incl, p[h][L:, :L], 0.0), jnp.where(tri_incl, p[h][L:, L:], 0.0)], axis=1)
           for h in H]
    x = [q[h][:L] + _dot(a_k[h], v_h[h], passes) for h in H]
    for d in range(n_dbl):
        x = [x[h] + _dot(pk[h], x[h], passes) for h in H]
        if d + 1 < n_dbl:
            pk = [_dot(pk[h], pk[h], passes) for h in H]
    uv = [jnp.concatenate([x[h], v_h[h]], axis=0) for h in H]
    y = [q[h][L:] + _dot(rbk[h], uv[h], passes) for h in H]
    s_new = [s0[h] * gam[:, sls[h]] + _dot(uv[h], bk[h], passes, _TN) for h in H]
    for h in H:
        s_ref[0, h] = s_new[h]
    last = lax.broadcasted_iota(jnp.int32, (L, 1), 0) == L - 1
    y = [jnp.where(last, _row_from_col(_matvec_bf16(s_new[h], r[L - 1:L, sls[h]]), eye), y[h]) for h in H]
    outs = [_wkv_out(y[h], v_h[h], rk[:, sls[h]], lg_ref[:, sls[h]], lb_ref[:, sls[h]]) for h in H]
    y_ref[0] = jnp.concatenate(outs, axis=1) * g_ref[0]


def wkv7(r, lw, k, v, a, g, k_k, k_a, r_k, lnx_g, lnx_b, s0, *, passes):
    bsz, t, d = r.shape
    n = RW_HEAD
    hh = d // n
    heads = _pick_tile(hh, (16, 4, 2))
    chunk = _pick_tile(t, (128, 64))
    wide = heads * n
    seq = pl.BlockSpec((1, chunk, wide), lambda b, hb, c: (b, c, hb))
    vec = pl.BlockSpec((1, wide), lambda b, hb, c: (0, hb))
    st = pl.BlockSpec((1, heads, n, n), lambda b, hb, c: (b, hb, 0, 0))
    vecs = [z.reshape(1, d) for z in (k_k, k_a, r_k, lnx_g, lnx_b)]
    return pl.pallas_call(
        functools.partial(_wkv_kernel, chunk=chunk, heads=heads, passes=passes),
        grid=(bsz, hh // heads, t // chunk),
        in_specs=[seq] * 6 + [vec] * 5 + [st],
        out_specs=[seq, st],
        out_shape=[jax.ShapeDtypeStruct((bsz, t, d), F32), jax.ShapeDtypeStruct((bsz, hh, n, n), F32)],
        compiler_params=_cparams(3),
    )(r, lw, k, v, a, g, *vecs, s0)


def _route_kernel(x_ref, rt_ref, bias_ref, eidx_ref, w_ref, pos_ref, cnt_ref, *, n_valid):
    ne = rt_ref.shape[0]
    per = ne // N_GROUPS
    logits = _dot(rt_ref[...], x_ref[...], 1, _NT)
    s = jax.nn.sigmoid(logits)
    sb = s + bias_ref[...]
    tm = sb.shape[1]
    sub = lax.broadcasted_iota(jnp.int32, (per, tm), 0)
    gs_rows = []
    for gi in range(N_GROUPS):
        tile = sb[gi * per:(gi + 1) * per, :]
        m1 = jnp.max(tile, axis=0, keepdims=True)
        first = jnp.min(jnp.where(tile == m1, sub, per), axis=0, keepdims=True)
        m2 = jnp.max(jnp.where(sub == first, -jnp.inf, tile), axis=0, keepdims=True)
        gs_rows.append(m1 + m2)
    keep_rows = []
    for gi in range(N_GROUPS):
        cnt = jnp.zeros((1, tm), jnp.int32)
        for gj in range(N_GROUPS):
            if gj == gi:
                continue
            beats = (gs_rows[gj] > gs_rows[gi]) if gj > gi else (gs_rows[gj] >= gs_rows[gi])
            cnt = cnt + beats.astype(jnp.int32)
        keep_rows.append(jnp.broadcast_to(cnt, (per, tm)))
    masked = jnp.where(jnp.concatenate(keep_rows, axis=0) < TOPK_GROUPS, sb, -1e9)
    eidx = lax.broadcasted_iota(jnp.int32, (ne, tm), 0)
    rank = jnp.zeros((ne, tm), jnp.int32)
    for ej in range(ne):
        rowv = masked[ej:ej + 1, :]
        rank = rank + jnp.where(eidx > ej, (rowv >= masked).astype(jnp.int32), (rowv > masked).astype(jnp.int32))
    tok = pl.program_id(0) * tm + lax.broadcasted_iota(jnp.int32, (ne, tm), 1)
    rank = jnp.where(tok < n_valid, rank, ne)
    sel = rank < TOP_K
    self32 = sel.astype(F32)
    den = jnp.sum(jnp.where(sel, s, 0.0), axis=0, keepdims=True)
    wgt = jnp.where(sel, s / den * ROUTED_SCALE, 0.0)

    @pl.when(pl.program_id(0) == 0)
    def _():
        cnt_ref[...] = jnp.zeros(cnt_ref.shape, F32)

    upper = (lax.broadcasted_iota(jnp.int32, (tm, tm), 0) < lax.broadcasted_iota(jnp.int32, (tm, tm), 1)).astype(BF16)
    carry = cnt_ref[...]
    pos = jnp.dot(self32.astype(BF16), upper, preferred_element_type=F32) + carry[:, :1]
    cnt_ref[...] = carry + jnp.sum(self32, axis=1, keepdims=True)
    eidx_f = eidx.astype(F32)
    rows_e, rows_w, rows_p = [], [], []
    for kk in range(TOP_K):
        hit = rank == kk
        rows_e.append(jnp.sum(jnp.where(hit, eidx_f, 0.0), axis=0, keepdims=True))
        rows_w.append(jnp.sum(jnp.where(hit, wgt, 0.0), axis=0, keepdims=True))
        rows_p.append(jnp.sum(jnp.where(hit, pos, 0.0), axis=0, keepdims=True))
    eidx_ref[...] = jnp.concatenate(rows_e, axis=0).astype(jnp.int32)
    w_ref[...] = jnp.concatenate(rows_w, axis=0)
    pos_ref[...] = jnp.concatenate(rows_p, axis=0).astype(jnp.int32)


def route(x, router_t, bias, n_valid):
    m, d = x.shape
    ne = router_t.shape[0]
    tm = _pick_tile(m, (256, 128))
    out = pl.BlockSpec((TOP_K, tm), lambda i: (0, i))
    return pl.pallas_call(
        functools.partial(_route_kernel, n_valid=n_valid),
        grid=(m // tm,),
        in_specs=[pl.BlockSpec((tm, d), lambda i: (i, 0)), pl.BlockSpec((ne, d), lambda i: (0, 0)),
                  pl.BlockSpec((ne, 1), lambda i: (0, 0))],
        out_specs=[out, out, out, pl.BlockSpec((ne, LANES), lambda i: (0, 0))],
        out_shape=[jax.ShapeDtypeStruct((TOP_K, m), jnp.int32), jax.ShapeDtypeStruct((TOP_K, m), F32),
                   jax.ShapeDtypeStruct((TOP_K, m), jnp.int32), jax.ShapeDtypeStruct((ne, LANES), F32)],
        compiler_params=_cparams(1),
    )(x, router_t, bias.reshape(ne, 1))


def _experts_kernel(be_ref, nu_ref, xs_ref, win_ref, wout_ref, o_ref, win_s, wout_s):
    i = pl.program_id(0)
    f = wout_ref.shape[0]

    @pl.when(i < nu_ref[0])
    def _():
        changed = jnp.logical_or(i == 0, be_ref[i] != be_ref[jnp.maximum(i - 1, 0)])

        @pl.when(changed)
        def _():
            win_s[...] = win_ref[...].astype(BF16)
            wout_s[...] = wout_ref[...].astype(BF16)

        hu = jnp.dot(xs_ref[...].astype(BF16), win_s[...], preferred_element_type=F32)
        hid = jax.nn.silu(hu[:, :f]) * hu[:, f:]
        y = jnp.dot(hid.astype(BF16), wout_s[...], preferred_element_type=F32)
        o_ref[...] = y.astype(BF16).astype(F32)

    @pl.when(i >= nu_ref[0])
    def _():
        o_ref[...] = jnp.zeros(o_ref.shape, F32)


def experts(xs, blk_expert, n_used, w_in, w_out, layer, tm):
    rws, d = xs.shape
    f2 = w_in.shape[3]
    f = f2 // 2
    grid_spec = pltpu.PrefetchScalarGridSpec(
        num_scalar_prefetch=2,
        grid=(rws // tm,),
        in_specs=[
            pl.BlockSpec((tm, d), lambda i, be, nu: (jnp.minimum(i, nu[0] - 1), 0)),
            pl.BlockSpec((None, None, d, f2), lambda i, be, nu: (layer, be[i], 0, 0)),
            pl.BlockSpec((None, None, f, d), lambda i, be, nu: (layer, be[i], 0, 0)),
        ],
        out_specs=pl.BlockSpec((tm, d), lambda i, be, nu: (i, 0)),
        scratch_shapes=[pltpu.VMEM((d, f2), BF16), pltpu.VMEM((f, d), BF16)],
    )
    return pl.pallas_call(
        _experts_kernel,
        grid_spec=grid_spec,
        out_shape=jax.ShapeDtypeStruct((rws, d), F32),
        compiler_params=_cparams(1),
    )(blk_expert, n_used, xs, w_in, w_out)


def moe_ffn(x, n_valid, layer, router, bias, w_in, w_out, ws_in, ws_out):
    m, d = x.shape
    ne = router.shape[2]
    eidx, ew, posk, cnt = route(x, router[layer].T, bias[layer], n_valid)
    tm = 256 if m * TOP_K // ne >= 256 else _round_up(max(m * TOP_K // ne, SUBLANES), SUBLANES)
    nk = m * TOP_K
    counts = cnt[:, 0].astype(jnp.int32)
    padded = (counts + tm - 1) // tm * tm
    pad_end = jnp.cumsum(padded)
    pad_start = pad_end - padded
    n_blocks = (nk + tm - 1) // tm + ne
    n_rows = n_blocks * tm
    tok = jnp.arange(m, dtype=jnp.int32)
    dest = jnp.where(tok[None, :] < n_valid, jnp.take(pad_start, eidx) + posk, n_rows - 1).astype(jnp.int32)
    token_of_row = jnp.zeros((n_rows,), jnp.int32).at[dest.reshape(-1)].set(jnp.tile(tok, TOP_K))
    blk_start = jnp.arange(n_blocks, dtype=jnp.int32) * tm
    blk_expert = jnp.minimum(jnp.sum((pad_end[None, :] <= blk_start[:, None]).astype(jnp.int32), axis=1), ne - 1)
    n_used = (pad_end[-1:] // tm).astype(jnp.int32)
    xs = jnp.take(x, token_of_row, axis=0)
    yb = experts(xs, blk_expert, n_used, w_in, w_out, layer, tm)
    wr = ew.astype(BF16).astype(F32)
    routed = jnp.sum(jnp.take(yb, dest.T, axis=0) * wr.T[:, :, None], axis=1)
    hid = linear(x, ws_in, widx=(layer,), glu=True, post=lambda a, b: jax.nn.silu(a) * b)
    return linear(hid, ws_out, widx=(layer,), fulls=(routed,), post=lambda acc, rt: acc + rt)


def _gelu_tanh(x):
    return 0.5 * x * (1.0 + jnp.tanh(0.7978845608028654 * (x + 0.044715 * x * x * x)))


def _group_tile(ref, g):
    if len(ref.shape) == 3:
        return ref[:, g, :]
    return ref[:, g * HEAD_DIM:(g + 1) * HEAD_DIM]


def _page_spec(pages, index_fn, slot_fn):
    if pages.ndim == 5:
        return pl.BlockSpec((None, PAGE, None, N_KV, HEAD_DIM), lambda *a: (index_fn(*a), 0, slot_fn(*a), 0, 0))
    return pl.BlockSpec((None, PAGE, N_KV * HEAD_DIM), lambda *a: (index_fn(*a), 0, slot_fn(*a)))


def _compress_kernel(tbl_ref, pg_ref, pos_ref, w1_ref, w2_ref, o_ref, *slabs, group):
    p = pl.program_id(2)
    pin = p % group
    for g in range(N_KV):
        slabs[g][pl.ds(pl.multiple_of(pin * PAGE, PAGE), PAGE), :] = _group_tile(pg_ref, g)

    @pl.when(pin == group - 1)
    def _():
        per_page = PAGE // CMP_BLOCK
        nrow = group * per_page
        hid = w1_ref.shape[2]
        for g in range(N_KV):
            acc = jnp.zeros((nrow, hid), F32)
            for t in range(CMP_BLOCK):
                lhs = slabs[g][pl.ds(t, nrow, stride=CMP_BLOCK), :] + pos_ref[0, t:t + 1, :]
                acc = acc + _dot(lhs, w1_ref[0, t * HEAD_DIM:(t + 1) * HEAD_DIM, :])
            o_ref[0, 0, :, g * HEAD_DIM:(g + 1) * HEAD_DIM] = _dot(_gelu_tanh(acc), w2_ref[0])


def compress(pages, table, slot0, cmp_pos, cmp_w1, cmp_w2):
    bsz, n_pg = table.shape
    group = min(32, n_pg)
    gw = N_KV * HEAD_DIM
    per_page = PAGE // CMP_BLOCK
    hid = cmp_w1.shape[2]
    grid_spec = pltpu.PrefetchScalarGridSpec(
        num_scalar_prefetch=1,
        grid=(bsz, 2, n_pg),
        in_specs=[
            _page_spec(pages, lambda b, s, p, tbl: tbl[b * n_pg + p], lambda b, s, p, tbl: slot0 + s),
            pl.BlockSpec((1, CMP_BLOCK, HEAD_DIM), lambda b, s, p, tbl: (s, 0, 0)),
            pl.BlockSpec((1, CMP_BLOCK * HEAD_DIM, hid), lambda b, s, p, tbl: (s, 0, 0)),
            pl.BlockSpec((1, hid, HEAD_DIM), lambda b, s, p, tbl: (s, 0, 0)),
        ],
        out_specs=pl.BlockSpec((1, 1, group * per_page, gw), lambda b, s, p, tbl: (b, s, p // group, 0)),
        scratch_shapes=[pltpu.VMEM((group * PAGE, HEAD_DIM), F32) for _ in range(N_KV)],
    )
    return pl.pallas_call(
        functools.partial(_compress_kernel, group=group),
        grid_spec=grid_spec,
        out_shape=jax.ShapeDtypeStruct((bsz, 2, n_pg * per_page, gw), F32),
        compiler_params=_cparams(3),
    )(table.reshape(-1), pages, cmp_pos, cmp_w1, cmp_w2)


def _nsa_cmp_kernel(q_ref, kc_ref, vc_ref, gate_ref, o_ref, sel_ref, score_s, rank_s, *, q_offset, n_sel_blocks,
                    tq, half):
    qt = pl.program_id(1)
    ncp = kc_ref.shape[1]
    nsp = sel_ref.shape[2]
    scale = HEAD_DIM ** -0.5
    rpg = N_KV
    n_heads = q_ref.shape[2] // HEAD_DIM
    rpg = n_heads // N_KV
    qpos = q_offset + qt * tq + lax.broadcasted_iota(jnp.int32, (1, tq), 1)
    cidx = lax.broadcasted_iota(jnp.int32, (ncp, 1), 0)
    corig = jnp.where(cidx < half, 2 * cidx, 2 * (cidx - half) + 1)
    cmask = ((corig + 1) * CMP_BLOCK - 1) <= qpos
    blk = lax.broadcasted_iota(jnp.int32, (nsp, 1), 0)
    cur = qpos // SEL_BLOCK
    forced = (blk == 0) | (blk == cur) | (blk == cur - 1)
    valid = (blk * SEL_BLOCK) <= qpos
    gate = gate_ref[0]
    for g in range(N_KV):
        kc = kc_ref[0, :, g * HEAD_DIM:(g + 1) * HEAD_DIM]
        vc = vc_ref[0, :, g * HEAD_DIM:(g + 1) * HEAD_DIM]
        imp = jnp.zeros((ncp, tq), F32)
        for h in range(rpg):
            hd = g * rpg + h
            qh = q_ref[0, :, hd * HEAD_DIM:(hd + 1) * HEAD_DIM]
            s = _dot(kc, qh, 1, _NT) * scale
            s = jnp.where(cmask, s, NEG)
            mx = jnp.max(s, axis=0, keepdims=True)
            e = jnp.where(cmask, jnp.exp(s - mx), 0.0)
            p = e / jnp.maximum(jnp.sum(e, axis=0, keepdims=True), 1e-30)
            imp = imp + p
            oc = _dot(p, vc, 1, _TN)
            o_ref[0, :, hd * HEAD_DIM:(hd + 1) * HEAD_DIM] = gate[:, hd * 3:hd * 3 + 1] * oc
        imp_blk = imp[:half] + imp[half:]
        if nsp > half:
            imp_blk = jnp.concatenate([imp_blk, jnp.zeros((nsp - half, tq), F32)], axis=0)
        else:
            imp_blk = imp_blk[:nsp]
        score = jnp.where(valid, imp_blk + FORCED_BONUS * forced.astype(F32), -1.0)
        score = jnp.where(blk < n_sel_blocks, score, -2.0)
        score_s[...] = score
        rank_s[...] = jnp.zeros((nsp, tq), jnp.int32)

        def body(j, carry):
            rowv = score_s[pl.ds(j, 1), :]
            rank_s[...] = rank_s[...] + jnp.where(blk > j, (rowv >= score).astype(jnp.int32),
                                                  (rowv > score).astype(jnp.int32))
            return carry

        lax.fori_loop(0, n_sel_blocks, body, 0)
        sel_ref[0, g] = (rank_s[...] < N_SEL).astype(F32)


def nsa_cmp(q, kc, vc, gates, *, q_offset, n_sel_blocks, tq):
    bsz, t, dq = q.shape
    ncp = kc.shape[1]
    half = ncp // 2
    nsp = _round_up(n_sel_blocks, SUBLANES)
    gw = N_KV * HEAD_DIM
    qspec = pl.BlockSpec((1, tq, dq), lambda b, i: (b, i, 0))
    cspec = pl.BlockSpec((1, ncp, gw), lambda b, i: (b, 0, 0))
    return pl.pallas_call(
        functools.partial(_nsa_cmp_kernel, q_offset=q_offset, n_sel_blocks=n_sel_blocks, tq=tq, half=half),
        grid=(bsz, t // tq),
        in_specs=[qspec, cspec, cspec, pl.BlockSpec((1, tq, gates.shape[2]), lambda b, i: (b, i, 0))],
        out_specs=[qspec, pl.BlockSpec((1, N_KV, nsp, tq), lambda b, i: (b, 0, 0, i))],
        out_shape=[jax.ShapeDtypeStruct((bsz, t, dq), F32), jax.ShapeDtypeStruct((bsz, N_KV, nsp, t), F32)],
        scratch_shapes=[pltpu.VMEM((nsp, tq), F32), pltpu.VMEM((nsp, tq), jnp.int32)],
        compiler_params=_cparams(2),
    )(q, kc, vc, gates)


def _nsa_attn_kernel(*refs, mode, q_offset, tq, n_kt, has_tail, tail_pos0, branch):
    tbl_ref, kp0_ref = refs[0], refs[1]
    it = iter(refs[2:])
    q_ref = next(it)
    k_ref = next(it)
    v_ref = next(it)
    tk_ref = next(it) if has_tail else None
    tv_ref = next(it) if has_tail else None
    sel_ref = next(it) if mode == "sel" else None
    gate_ref = next(it)
    prev_ref = next(it)
    o_ref = next(it)
    m_s, l_s, acc_s = next(it), next(it), next(it)

    qt = pl.program_id(1)
    j = pl.program_id(2)
    n_heads = q_ref.shape[2] // HEAD_DIM
    rpg = n_heads // N_KV
    rows = rpg * tq
    scale = HEAD_DIM ** -0.5
    width = WINDOW if mode == "win" else (1 << 30)

    @pl.when(j == 0)
    def _():
        m_s[...] = jnp.full(m_s.shape, NEG, F32)
        l_s[...] = jnp.zeros(l_s.shape, F32)
        acc_s[...] = jnp.zeros(acc_s.shape, F32)

    q_first = q_offset + qt * tq
    qpos = q_first + lax.broadcasted_iota(jnp.int32, (tq, 1), 0)

    def tile(kt_ref, vt_ref, kp0):
        kpos = kp0 + lax.broadcasted_iota(jnp.int32, (1, PAGE), 1)
        dist = qpos - kpos
        band = (dist >= 0) & (dist < width)
        for g in range(N_KV):
            mask = band
            if mode == "sel":
                nsp = sel_ref.shape[2]
                blk = lax.broadcasted_iota(jnp.int32, (nsp, 1), 0)
                expand = (blk == kpos // SEL_BLOCK).astype(BF16)
                hit = _dg(sel_ref[0, g].astype(BF16), expand, _TN)
                mask = band & (hit > 0.5)
            mask4 = jnp.concatenate([mask] * rpg, axis=0)
            qg = jnp.concatenate(
                [q_ref[0, :, (g * rpg + h) * HEAD_DIM:(g * rpg + h + 1) * HEAD_DIM] for h in range(rpg)], axis=0)
            kg = _group_tile(kt_ref, g)
            vg = _group_tile(vt_ref, g)
            s = _dot(qg, kg, 1, _NT) * scale
            s = jnp.where(mask4, s, NEG)
            m_old = m_s[g]
            m_new = jnp.maximum(m_old, jnp.max(s, -1, keepdims=True))
            alpha = jnp.exp(m_old - m_new)
            p = jnp.where(mask4, jnp.exp(s - m_new), 0.0)
            l_s[g] = alpha * l_s[g] + jnp.sum(p, -1, keepdims=True)
            acc_s[g] = alpha * acc_s[g] + _dot(p, vg, 1)
            m_s[g] = m_new

    idx = (pl.program_id(0) * pl.num_programs(1) + qt) * n_kt + jnp.minimum(j, n_kt - 1)
    kp0 = kp0_ref[idx]
    live = (kp0 >= 0) & (kp0 <= q_first + tq - 1) & (kp0 + PAGE - 1 > q_first - width)
    if has_tail:
        live = live & (j < n_kt)

    @pl.when(live)
    def _():
        tile(k_ref, v_ref, kp0)

    if has_tail:
        @pl.when(j == n_kt)
        def _():
            tile(tk_ref, tv_ref, tail_pos0)

    @pl.when(j == pl.num_programs(2) - 1)
    def _():
        gate = gate_ref[0]
        for g in range(N_KV):
            o = acc_s[g] / jnp.maximum(l_s[g], 1e-30)
            for h in range(rpg):
                hd = g * rpg + h
                cs = slice(hd * HEAD_DIM, (hd + 1) * HEAD_DIM)
                o_ref[0, :, cs] = prev_ref[0, :, cs] + gate[:, hd * 3 + branch:hd * 3 + branch + 1] * o[h * tq:(h + 1) * tq]


def nsa_attn(q, pages, tbl, kp0, slot_k, slot_v, tail, sel, gates, prev, *, mode, q_offset, tq, tail_pos0, branch):
    bsz, t, dq = q.shape
    n_qt = t // tq
    n_kt = tbl.shape[2]
    gw = N_KV * HEAD_DIM
    n_heads = dq // HEAD_DIM
    rows = (n_heads // N_KV) * tq
    has_tail = tail is not None
    n_steps = n_kt + (1 if has_tail else 0)

    def page_idx(b, i, j, tb, kp):
        return tb[(b * n_qt + i) * n_kt + jnp.minimum(j, n_kt - 1)]

    qspec = pl.BlockSpec((1, tq, dq), lambda b, i, j, tb, kp: (b, i, 0))
    in_specs = [
        qspec,
        _page_spec(pages, page_idx, lambda *a: slot_k),
        _page_spec(pages, page_idx, lambda *a: slot_v),
    ]
    args = [q, pages, pages]
    if has_tail:
        in_specs += [pl.BlockSpec((None, PAGE, gw), lambda b, i, j, tb, kp: (b, 0, 0)),
                     pl.BlockSpec((None, PAGE, gw), lambda b, i, j, tb, kp: (b, 0, 1))]
        args += [tail, tail]
    if mode == "sel":
        nsp = sel.shape[2]
        in_specs.append(pl.BlockSpec((1, N_KV, nsp, tq), lambda b, i, j, tb, kp: (b, 0, 0, i)))
        args.append(sel)
    in_specs += [pl.BlockSpec((1, tq, gates.shape[2]), lambda b, i, j, tb, kp: (b, i, 0)), qspec]
    args += [gates, prev]
    grid_spec = pltpu.PrefetchScalarGridSpec(
        num_scalar_prefetch=2,
        grid=(bsz, n_qt, n_steps),
        in_specs=in_specs,
        out_specs=qspec,
        scratch_shapes=[pltpu.VMEM((N_KV, rows, 1), F32), pltpu.VMEM((N_KV, rows, 1), F32),
                        pltpu.VMEM((N_KV, rows, HEAD_DIM), F32)],
    )
    return pl.pallas_call(
        functools.partial(_nsa_attn_kernel, mode=mode, q_offset=q_offset, tq=tq, n_kt=n_kt, has_tail=has_tail,
                          tail_pos0=tail_pos0, branch=branch),
        grid_spec=grid_spec,
        out_shape=jax.ShapeDtypeStruct((bsz, t, dq), F32),
        compiler_params=_cparams(3),
    )(tbl.reshape(-1).astype(jnp.int32), kp0.reshape(-1).astype(jnp.int32), *args)


def _softplus(z):
    return jnp.maximum(z, 0.0) + jnp.log(1.0 + jnp.exp(-jnp.abs(z)))


def _post_logdecay(acc, w0):
    return -jnp.exp(-_softplus(-(w0 + acc)) - 0.5)


def _post_sig_bias(acc, b):
    return jax.nn.sigmoid(acc + b)


def _post_vres(acc, v0, v, vf):
    return v + (vf - v) * jax.nn.sigmoid(v0 + acc)


def _post_ple(acc, b, x, t):
    return x + jax.nn.sigmoid(acc + b) * t


def _pad_rows(x, m):
    return jnp.pad(x, ((0, m - x.shape[0]),) + ((0, 0),) * (x.ndim - 1))


def kernel(x_prompt, x_sample, cache_kv_pages, cache_win_kv, state_wkv, state_shift, page_table, p_prompt, p_sample, ln_g, ln_b, rw_mix, rw_w_rkv, rw_w0, rw_w1, rw_w2, rw_a0, rw_a1, rw_a2, rw_v0, rw_v1, rw_v2, rw_g1, rw_g2, rw_k_k, rw_k_a, rw_r_k, rw_lnx_g, rw_lnx_b, rw_w_o, kv_ln_g, kv_ln_b, w_kv, cmp_pos, cmp_w1, cmp_w2, nsa_w_qg, nsa_b_g, nsa_w_o, moe_router, moe_bias, moe_w_in, moe_w_out, moe_ws_in, moe_ws_out, ple_proj, ple_gate, ple_gate_b):
    bp, tp, d = x_prompt.shape
    bs, ts, _ = x_sample.shape
    depth = ln_g.shape[0]
    n_a = state_wkv.shape[0]
    alpha = (2 * depth) ** 0.25
    n_p, n_s = bp * tp, bs * ts
    n_tok = n_p + n_s
    m = _round_up(n_tok, 768) if n_tok > 768 else _round_up(n_tok, SUBLANES)
    ts_pad = _round_up(ts, SUBLANES)
    hh = d // RW_HEAD
    gw = N_KV * HEAD_DIM
    n_pg_s = page_table.shape[1]
    past_len = n_pg_s * cache_kv_pages.shape[1]
    win_buf = cache_win_kv.shape[1]
    assert cache_kv_pages.shape[1] == PAGE and tp % PAGE == 0 and win_buf % PAGE == 0 and ts < CMP_BLOCK

    def join(a_p, a_s):
        return _pad_rows(jnp.concatenate([a_p.reshape(n_p, -1), a_s.reshape(n_s, -1)], axis=0), m)

    def split_seq(z):
        zp = z[:n_p].reshape(bp, tp, -1)
        zs = jnp.pad(z[n_p:n_tok].reshape(bs, ts, -1), ((0, 0), (0, ts_pad - ts), (0, 0)))
        return zp, zs

    def merge_seq(zp, zs):
        return join(zp, zs[:, :ts])

    x = join(x_prompt, x_sample)
    pe = jnp.concatenate([p_prompt.reshape(depth, n_p, -1), p_sample.reshape(depth, n_s, -1)], axis=1)
    pe = jnp.pad(pe, ((0, 0), (0, m - n_tok), (0, 0)))

    shifts_p, shifts_s, wkvs_p, wkvs_s = [], [], [], []
    v_first = None
    ctx = None
    kv_rows_p = kv_rows_s = win_p = win_s = None
    for i in range(depth):
        if i < n_a:
            scan_passes = 3 if i == 0 else 1
            xp_, xs_ = x[:n_p].reshape(bp, tp, d), x[n_p:n_tok].reshape(bs, ts, d)
            shifts_p.append(xp_[:, -1])
            shifts_s.append(xs_[:, -1])
            prev = join(jnp.concatenate([jnp.zeros((bp, 1, d), F32), xp_[:, :-1]], axis=1),
                        jnp.concatenate([state_shift[i][:, None, :], xs_[:, :-1]], axis=1))
            mix = rw_mix[i]
            lin = linear
            r = lin(x, rw_w_rkv, widx=(i, 0), mix=(prev, mix[0]))
            k = lin(x, rw_w_rkv, widx=(i, 1), mix=(prev, mix[2]))
            v = lin(x, rw_w_rkv, widx=(i, 2), mix=(prev, mix[3]))
            lw = lin(lin(x, rw_w1, widx=(i,), mix=(prev, mix[1]), post=jnp.tanh), rw_w2, widx=(i,), rows=(rw_w0[i],),
                     post=_post_logdecay)
            if i == 0:
                v_first = v
            else:
                v = lin(lin(x, rw_v1, widx=(i - 1,), mix=(prev, mix[3])), rw_v2, widx=(i - 1,), rows=(rw_v0[i - 1],),
                        fulls=(v, v_first), post=_post_vres)
            a = lin(lin(x, rw_a1, widx=(i,), mix=(prev, mix[4])), rw_a2, widx=(i,), rows=(rw_a0[i],),
                    post=_post_sig_bias)
            g = lin(lin(x, rw_g1, widx=(i,), mix=(prev, mix[5]), post=jax.nn.sigmoid), rw_g2, widx=(i,))
            seqs = [split_seq(z) for z in (r, lw, k, v, a, g)]
            vecs = (rw_k_k[i], rw_k_a[i], rw_r_k[i].reshape(-1), rw_lnx_g[i], rw_lnx_b[i])
            y_p, s_p = wkv7(*[z[0] for z in seqs], *vecs, jnp.zeros((bp, hh, RW_HEAD, RW_HEAD), F32),
                            passes=scan_passes)
            y_s, s_s = wkv7_seq(*[z[1] for z in seqs], *vecs, state_wkv[i], steps=ts)
            wkvs_p.append(s_p)
            wkvs_s.append(s_s)
            h = lin(merge_seq(y_p, y_s), rw_w_o, widx=(i,))
        else:
            jn = i - n_a
            if ctx is None:
                rows = linear(layer_norm(x, kv_ln_g, kv_ln_b), w_kv)
                rows_p = rows[:n_p].reshape(bp, tp, 6 * gw)
                rows_s = rows[n_p:n_tok].reshape(bs, ts, 6 * gw)
                kv_rows_p = rows_p[:, :, :4 * gw].reshape(bp, tp, 4, N_KV, HEAD_DIM)
                kv_rows_s = rows_s[:, :, :4 * gw].reshape(bs, ts, 4, N_KV, HEAD_DIM)
                win_p = rows_p[:, max(tp - WINDOW, 0):, 4 * gw:].reshape(bp, -1, 2, N_KV, HEAD_DIM)
                new_win = rows_s[:, :, 4 * gw:].reshape(bs, ts, 2, N_KV, HEAD_DIM)
                win_s = jnp.concatenate([cache_win_kv, new_win], axis=1)[:, ts:]
                n_pg_p = tp // PAGE
                pages_p = rows[:n_p].reshape(bp * n_pg_p, PAGE, 6 * gw)
                table_p = jnp.arange(bp * n_pg_p, dtype=jnp.int32).reshape(bp, n_pg_p)
                pages_s = cache_kv_pages
                pages_w = cache_win_kv.reshape(bs * (win_buf // PAGE), PAGE, 2, N_KV, HEAD_DIM)

                def perm(c):
                    return jnp.concatenate([c[:, 0::2], c[:, 1::2]], axis=1)

                cmp_p = compress(pages_p, table_p, 0, cmp_pos, cmp_w1, cmp_w2)
                cmp_s = compress(pages_s, page_table.astype(jnp.int32), 0, cmp_pos, cmp_w1, cmp_w2)
                tail_sel = jnp.pad(rows_s[:, :, 2 * gw:4 * gw], ((0, 0), (0, PAGE - ts), (0, 0)))
                tail_win = jnp.pad(rows_s[:, :, 4 * gw:6 * gw], ((0, 0), (0, PAGE - ts), (0, 0)))
                tq_p = PAGE
                n_qt = tp // tq_p
                qi = jnp.arange(n_qt, dtype=jnp.int32)[:, None]
                kj = jnp.arange(n_pg_p, dtype=jnp.int32)[None, :]
                base = (jnp.arange(bp, dtype=jnp.int32) * n_pg_p)[:, None, None]
                tbl_sel_p = base + jnp.minimum(kj, qi)[None]
                kp0_sel_p = jnp.broadcast_to(jnp.where(kj <= qi, kj * PAGE, -1)[None], tbl_sel_p.shape)
                n_wt = WINDOW // PAGE + 1
                kw = qi - (n_wt - 1) + jnp.arange(n_wt, dtype=jnp.int32)[None, :]
                tbl_win_p = base + jnp.maximum(kw, 0)[None]
                kp0_win_p = jnp.broadcast_to(jnp.where(kw >= 0, kw * PAGE, -1)[None], tbl_win_p.shape)
                tbl_sel_s = page_table.astype(jnp.int32)[:, None, :]
                kp0_sel_s = jnp.broadcast_to((jnp.arange(n_pg_s, dtype=jnp.int32) * PAGE)[None, None], tbl_sel_s.shape)
                n_ww = win_buf // PAGE
                tbl_win_s = jnp.arange(bs * n_ww, dtype=jnp.int32).reshape(bs, 1, n_ww)
                kp0_win_s = jnp.broadcast_to(
                    (past_len - win_buf + jnp.arange(n_ww, dtype=jnp.int32) * PAGE)[None, None], tbl_win_s.shape)
                ctx = dict(kc_p=perm(cmp_p[:, 0]), vc_p=perm(cmp_p[:, 1]), kc_s=perm(cmp_s[:, 0]), vc_s=perm(cmp_s[:, 1]))
            dq = nsa_w_qg.shape[2] - nsa_b_g.shape[1]
            q = linear(x, nsa_w_qg, widx=(jn,), ncols=dq)
            gates = linear(x, nsa_w_qg[jn][:, dq:], rows=(nsa_b_g[jn],), post=_post_sig_bias)
            (q_p, q_s), (g_p, g_s) = split_seq(q), split_seq(gates)
            ns_p = -(-tp // SEL_BLOCK)
            ns_s = -(-(past_len + ts) // SEL_BLOCK)
            o_p, sel_p = nsa_cmp(q_p, ctx['kc_p'], ctx['vc_p'], g_p, q_offset=0, n_sel_blocks=ns_p, tq=tq_p)
            o_p = nsa_attn(q_p, pages_p, tbl_sel_p, kp0_sel_p, 2, 3, None, sel_p, g_p, o_p, mode="sel", q_offset=0,
                           tq=tq_p, tail_pos0=0, branch=1)
            o_p = nsa_attn(q_p, pages_p, tbl_win_p, kp0_win_p, 4, 5, None, None, g_p, o_p, mode="win", q_offset=0,
                           tq=tq_p, tail_pos0=0, branch=2)
            o_s, sel_s = nsa_cmp(q_s, ctx['kc_s'], ctx['vc_s'], g_s, q_offset=past_len, n_sel_blocks=ns_s, tq=ts_pad)
            o_s = nsa_attn(q_s, pages_s, tbl_sel_s, kp0_sel_s, 2, 3, tail_sel, sel_s, g_s, o_s, mode="sel",
                           q_offset=past_len, tq=ts_pad, tail_pos0=past_len, branch=1)
            o_s = nsa_attn(q_s, pages_w, tbl_win_s, kp0_win_s, 0, 1, tail_win, None, g_s, o_s, mode="win",
                           q_offset=past_len, tq=ts_pad, tail_pos0=past_len, branch=2)
            h = linear(merge_seq(o_p, o_s), nsa_w_o, widx=(jn,))
        x = layer_norm(x, ln_g[i, 0], ln_b[i, 0], h=h, alpha=alpha)
        mo = moe_ffn(x, n_tok, i, moe_router, moe_bias, moe_w_in, moe_w_out, moe_ws_in, moe_ws_out)
        x = layer_norm(x, ln_g[i, 1], ln_b[i, 1], h=mo, alpha=alpha)
        t = linear(pe, ple_proj, xidx=(i,), widx=(i,))
        x = linear(x, ple_gate, widx=(i,), rows=(ple_gate_b[i],), fulls=(x, t), post=_post_ple)

    y_p = x[:n_p].reshape(bp, tp, d)
    y_s = x[n_p:n_tok].reshape(bs, ts, d)
    return (y_p, y_s, kv_rows_p, kv_rows_s, win_p, win_s, jnp.stack(wkvs_p), jnp.stack(wkvs_s),
            jnp.stack(shifts_p), jnp.stack(shifts_s))
```

```python
import functools

import jax
import jax.numpy as jnp
from jax import lax
from jax.experimental import pallas as pl
from jax.experimental.pallas import tpu as pltpu

F32 = jnp.float32
BF16 = jnp.bfloat16

RW_HEAD = 64
GN_EPS = 64e-5
N_KV = 4
HEAD_DIM = 128
CMP_BLOCK = 32
SEL_BLOCK = 64
N_SEL = 16
WINDOW = 512
FORCED_BONUS = 100.0
N_GROUPS = 8
TOPK_GROUPS = 4
TOP_K = 8
ROUTED_SCALE = 2.5
LN_EPS = 1e-5
NEG = -1e30

LANES = 128
SUBLANES = 8
PAGE = 128
VMEM_LIMIT = 56 * 1024 * 1024
LINEAR_W_BYTES = 16 * 1024 * 1024
LINEAR_VMEM_BYTES = 46 * 1024 * 1024


def _cparams(n_axes):
    return pltpu.CompilerParams(dimension_semantics=("arbitrary",) * n_axes, vmem_limit_bytes=VMEM_LIMIT)


def _split(x):
    hi = x.astype(BF16)
    lo = (x - hi.astype(F32)).astype(BF16)
    return hi, lo


_NN = (((1,), (0,)), ((), ()))
_NT = (((1,), (1,)), ((), ()))
_TN = (((0,), (0,)), ((), ()))


def _dg(a, b, dims):
    return lax.dot_general(a, b, dims, preferred_element_type=F32)


def _dot(a, b, passes=1, dims=_NN):
    if passes == 1:
        return _dg(a.astype(BF16), b.astype(BF16), dims)
    ah, al = _split(a)
    bh, bl = _split(b)
    return _dg(ah, bh, dims) + (_dg(ah, bl, dims) + _dg(al, bh, dims))


def _round_up(x, m):
    return (x + m - 1) // m * m


def _pick_tile(n, prefs):
    for t in prefs:
        if n % t == 0:
            return t
    return n


def _linear_kernel(*refs, passes, has_mix, glu, n_row, n_full, post):
    it = iter(refs)
    x_ref = next(it)
    xp_ref = next(it) if has_mix else None
    mix_ref = next(it) if has_mix else None
    w_refs = [next(it) for _ in range(2 if glu else 1)]
    row_refs = [next(it) for _ in range(n_row)]
    full_refs = [next(it) for _ in range(n_full)]
    o_ref = next(it)
    scr = list(it)
    per = 2 if passes == 3 else 1

    @pl.when(pl.program_id(1) == 0)
    def _():
        for wi, w_ref in enumerate(w_refs):
            w = w_ref[...]
            if passes == 3:
                hi, lo = _split(w)
                scr[wi * per][...] = hi
                scr[wi * per + 1][...] = lo
            else:
                scr[wi * per][...] = w.astype(BF16)

    x = x_ref[...]
    if has_mix:
        x = x + (xp_ref[...] - x) * mix_ref[...]
    if passes == 3:
        xh, xl = _split(x)
    else:
        xh = x.astype(BF16)
    accs = []
    for wi in range(len(w_refs)):
        wh = scr[wi * per][...]
        acc = jnp.dot(xh, wh, preferred_element_type=F32)
        if passes == 3:
            wl = scr[wi * per + 1][...]
            acc = acc + (jnp.dot(xh, wl, preferred_element_type=F32) + jnp.dot(xl, wh, preferred_element_type=F32))
        accs.append(acc)
    extras = [r[...] for r in row_refs] + [r[...] for r in full_refs]
    o_ref[...] = post(*accs, *extras)


def _identity(acc):
    return acc


def linear(x, w, *, xidx=(), widx=(), ncols=None, passes=1, mix=None, glu=False, rows=(), fulls=(), post=_identity,
           tm=None, tn=None):
    m, k = x.shape[-2:]
    n = (ncols or w.shape[-1]) // (2 if glu else 1)
    lead = (None,) * len(widx)
    n_w = 2 if glu else 1
    tn = tn or next((c for c in (n, n // 2, 512, 256, 128)
                     if n % c == 0 and (c % LANES == 0 or c == n) and n_w * k * c * 4 <= LINEAR_W_BYTES), n)
    w_bytes = n_w * k * tn * (4 + 2 * (2 if passes == 3 else 1))
    per_row = (2 if mix is not None else 1) * k * 4 * 2 + (1 + len(fulls)) * tn * 4 * 2
    tm = tm or next((c for c in (768, 512, 384, 256, 128, 64, 32, 16, 8)
                     if m % c == 0 and w_bytes + c * per_row <= LINEAR_VMEM_BYTES), m)
    nb = n // tn
    grid = (nb, m // tm)
    x_spec = pl.BlockSpec((None,) * len(xidx) + (tm, k), lambda j, i: xidx + (i, 0))
    in_specs = [x_spec]
    args = [x]
    if mix is not None:
        in_specs += [x_spec, pl.BlockSpec((1, k), lambda j, i: (0, 0))]
        args += [mix[0], mix[1].reshape(1, k)]
    in_specs.append(pl.BlockSpec(lead + (k, tn), lambda j, i: widx + (0, j), pipeline_mode=pl.Buffered(1)))
    args.append(w)
    if glu:
        in_specs.append(pl.BlockSpec(lead + (k, tn), lambda j, i: widx + (0, j + nb), pipeline_mode=pl.Buffered(1)))
        args.append(w)
    for r in rows:
        in_specs.append(pl.BlockSpec((1, tn), lambda j, i: (0, j)))
        args.append(r.reshape(1, n))
    for f in fulls:
        in_specs.append(pl.BlockSpec((tm, tn), lambda j, i: (i, j)))
        args.append(f)
    scratch = [pltpu.VMEM((k, tn), BF16) for _ in range(n_w * (2 if passes == 3 else 1))]
    kern = functools.partial(_linear_kernel, passes=passes, has_mix=mix is not None, glu=glu,
                             n_row=len(rows), n_full=len(fulls), post=post)
    return pl.pallas_call(
        kern,
        grid=grid,
        in_specs=in_specs,
        out_specs=pl.BlockSpec((tm, tn), lambda j, i: (i, j)),
        out_shape=jax.ShapeDtypeStruct((m, n), F32),
        scratch_shapes=scratch,
        compiler_params=_cparams(2),
    )(*args)


def _ln_kernel(*refs, alpha, has_h):
    if has_h:
        x_ref, h_ref, g_ref, b_ref, o_ref = refs
        x = alpha * x_ref[...] + h_ref[...]
    else:
        x_ref, g_ref, b_ref, o_ref = refs
        x = x_ref[...]
    mu = jnp.mean(x, -1, keepdims=True)
    xc = x - mu
    var = jnp.mean(xc * xc, -1, keepdims=True)
    o_ref[...] = xc * lax.rsqrt(var + LN_EPS) * g_ref[...] + b_ref[...]


def layer_norm(x, g, b, h=None, alpha=1.0):
    m, d = x.shape
    tm = _pick_tile(m, (384, 256, 128, 64, 32, 16, 8))
    spec = pl.BlockSpec((tm, d), lambda i: (i, 0))
    vspec = pl.BlockSpec((1, d), lambda i: (0, 0))
    args = [x] + ([h] if h is not None else []) + [g.reshape(1, d), b.reshape(1, d)]
    in_specs = [spec] + ([spec] if h is not None else []) + [vspec, vspec]
    return pl.pallas_call(
        functools.partial(_ln_kernel, alpha=alpha, has_h=h is not None),
        grid=(m // tm,),
        in_specs=in_specs,
        out_specs=spec,
        out_shape=jax.ShapeDtypeStruct((m, d), F32),
        compiler_params=_cparams(1),
    )(*args)


def _bf16r(x):
    return x.astype(BF16).astype(F32)


def _matvec_bf16(s, vec_row):
    return jnp.sum(_bf16r(s) * _bf16r(vec_row), axis=1, keepdims=True)


def _row_from_col(col, eye):
    return jnp.sum(eye * col, axis=0, keepdims=True)


def _col_from_row(row, eye):
    return jnp.sum(eye * row, axis=1, keepdims=True)


def _wkv_out(y, v_h, rk_h, lg, lb):
    mu = jnp.mean(y, -1, keepdims=True)
    yc = y - mu
    var = jnp.mean(yc * yc, -1, keepdims=True)
    return yc * lax.rsqrt(var + GN_EPS) * lg + lb + jnp.sum(rk_h, -1, keepdims=True) * v_h


def _wkv_seq_kernel(r_ref, lw_ref, k_ref, v_ref, a_ref, g_ref, kk_ref, ka_ref, rk_ref, lg_ref, lb_ref, s0_ref,
                    y_ref, s_ref, *, steps, heads):
    N = RW_HEAD
    r = r_ref[0]
    k = k_ref[0]
    v = v_ref[0]
    a = a_ref[0]
    w = jnp.exp(lw_ref[0])
    kkw = k * kk_ref[...]
    k2 = k * (1.0 + (a - 1.0) * ka_ref[...])
    rk = r * k2 * rk_ref[...]
    eye = (lax.broadcasted_iota(jnp.int32, (N, N), 0) == lax.broadcasted_iota(jnp.int32, (N, N), 1)).astype(F32)
    tpad = r.shape[0]
    outs = []
    for h in range(heads):
        sl = slice(h * N, (h + 1) * N)
        kk = kkw[:, sl]
        kk = kk * lax.rsqrt(jnp.maximum(jnp.sum(kk * kk, -1, keepdims=True), 1e-24))
        b_h = kk * a[:, sl]
        s = s0_ref[0, h]
        ys = []
        for t in range(steps):
            sa = _matvec_bf16(s, -kk[t:t + 1])
            s = s * w[t:t + 1, sl] + sa * b_h[t:t + 1] + _col_from_row(v[t:t + 1, sl], eye) * k2[t:t + 1, sl]
            ys.append(_row_from_col(_matvec_bf16(s, r[t:t + 1, sl]), eye))
        s_ref[0, h] = s
        y = jnp.concatenate(ys + [jnp.zeros((tpad - steps, N), F32)], axis=0)
        outs.append(_wkv_out(y, v[:, sl], rk[:, sl], lg_ref[:, sl], lb_ref[:, sl]))
    y_ref[0] = jnp.concatenate(outs, axis=1) * g_ref[0]


def wkv7_seq(r, lw, k, v, a, g, k_k, k_a, r_k, lnx_g, lnx_b, s0, *, steps):
    bsz, t, d = r.shape
    n = RW_HEAD
    hh = d // n
    heads = 4 if hh % 4 == 0 else 2
    wide = heads * n
    seq = pl.BlockSpec((1, t, wide), lambda b, hb: (b, 0, hb))
    vec = pl.BlockSpec((1, wide), lambda b, hb: (0, hb))
    st = pl.BlockSpec((1, heads, n, n), lambda b, hb: (b, hb, 0, 0))
    vecs = [z.reshape(1, d) for z in (k_k, k_a, r_k, lnx_g, lnx_b)]
    return pl.pallas_call(
        functools.partial(_wkv_seq_kernel, steps=steps, heads=heads),
        grid=(bsz, hh // heads),
        in_specs=[seq] * 6 + [vec] * 5 + [st],
        out_specs=[seq, st],
        out_shape=[jax.ShapeDtypeStruct((bsz, t, d), F32), jax.ShapeDtypeStruct((bsz, hh, n, n), F32)],
        compiler_params=_cparams(2),
    )(r, lw, k, v, a, g, *vecs, s0)


def _wkv_kernel(r_ref, lw_ref, k_ref, v_ref, a_ref, g_ref, kk_ref, ka_ref, rk_ref, lg_ref, lb_ref, s0_ref,
                y_ref, s_ref, *, chunk, heads, passes):
    L, N = chunk, RW_HEAD

    @pl.when(pl.program_id(2) == 0)
    def _():
        s_ref[...] = s0_ref[...]

    r = r_ref[0]
    lw = lw_ref[0]
    k = k_ref[0]
    v = v_ref[0]
    a = a_ref[0]
    kkw = k * kk_ref[...]
    k2 = k * (1.0 + (a - 1.0) * ka_ref[...])
    rk = r * k2 * rk_ref[...]

    row = lax.broadcasted_iota(jnp.int32, (L, L), 0)
    col = lax.broadcasted_iota(jnp.int32, (L, L), 1)
    tri_incl = (col <= row)
    tri_strict = (col < row)
    tri = tri_incl.astype(BF16)
    l1 = lw.astype(BF16)
    rem = lw - l1.astype(F32)
    l2 = rem.astype(BF16)
    l3 = (rem - l2.astype(F32)).astype(BF16)
    cs = (jnp.dot(tri, l1, preferred_element_type=F32) + jnp.dot(tri, l2, preferred_element_type=F32)
          + jnp.dot(tri, l3, preferred_element_type=F32))
    cs_last = cs[L - 1:L, :]
    e_prev = jnp.exp(cs - lw)
    e_inv = jnp.exp(-cs)
    e_fwd = jnp.exp(cs)
    e_tail = jnp.exp(cs_last - cs)
    gam = jnp.exp(cs_last)

    n_dbl = max(1, (L - 1).bit_length())
    eye = (lax.broadcasted_iota(jnp.int32, (N, N), 0) == lax.broadcasted_iota(jnp.int32, (N, N), 1)).astype(F32)
    H = range(heads)
    sls = [slice(h * N, (h + 1) * N) for h in H]
    kk = [kkw[:, sl] for sl in sls]
    kk = [z * lax.rsqrt(jnp.maximum(jnp.sum(z * z, -1, keepdims=True), 1e-24)) for z in kk]
    b_h = [kk[h] * a[:, sls[h]] for h in H]
    k_h = [k2[:, sl] for sl in sls]
    v_h = [v[:, sl] for sl in sls]
    lhs2 = [jnp.concatenate([-kk[h] * e_prev[:, sls[h]], r[:, sls[h]] * e_fwd[:, sls[h]]], axis=0) for h in H]
    rhs2 = [jnp.concatenate([b_h[h] * e_inv[:, sls[h]], k_h[h] * e_inv[:, sls[h]]], axis=0) for h in H]
    bk = [jnp.concatenate([b_h[h] * e_tail[:, sls[h]], k_h[h] * e_tail[:, sls[h]]], axis=0) for h in H]
    s0 = [s_ref[0, h] for h in H]
    p = [_dot(lhs2[h], rhs2[h], passes, _NT) for h in H]
    q = [_dot(lhs2[h], s0[h], passes, _NT) for h in H]
    pk = [jnp.where(tri_strict, p[h][:L, :L], 0.0) for h in H]
    a_k = [jnp.where(tri_strict, p[h][:L, L:], 0.0) for h in H]
    rbk = [jnp.concatenate([jnp.where(tri_incl, p[h][L:, :L], 0.0), jnp.where(tri_incl, p[h][L:, L:], 0.0)], axis=1)
           for h in H]
    x = [q[h][:L] + _dot(a_k[h], v_h[h], passes) for h in H]
    for d in range(n_dbl):
        x = [x[h] + _dot(pk[h], x[h], passes) for h in H]
        if d + 1 < n_dbl:
            pk = [_dot(pk[h], pk[h], passes) for h in H]
    uv = [jnp.concatenate([x[h], v_h[h]], axis=0) for h in H]
    y = [q[h][L:] + _dot(rbk[h], uv[h], passes) for h in H]
    s_new = [s0[h] * gam[:, sls[h]] + _dot(uv[h], bk[h], passes, _TN) for h in H]
    for h in H:
        s_ref[0, h] = s_new[h]
    last = lax.broadcasted_iota(jnp.int32, (L, 1), 0) == L - 1
    y = [jnp.where(last, _row_from_col(_matvec_bf16(s_new[h], r[L - 1:L, sls[h]]), eye), y[h]) for h in H]
    outs = [_wkv_out(y[h], v_h[h], rk[:, sls[h]], lg_ref[:, sls[h]], lb_ref[:, sls[h]]) for h in H]
    y_ref[0] = jnp.concatenate(outs, axis=1) * g_ref[0]


def wkv7(r, lw, k, v, a, g, k_k, k_a, r_k, lnx_g, lnx_b, s0, *, passes):
    bsz, t, d = r.shape
    n = RW_HEAD
    hh = d // n
    heads = _pick_tile(hh, (16, 4, 2))
    chunk = _pick_tile(t, (128, 64))
    wide = heads * n
    seq = pl.BlockSpec((1, chunk, wide), lambda b, hb, c: (b, c, hb))
    vec = pl.BlockSpec((1, wide), lambda b, hb, c: (0, hb))
    st = pl.BlockSpec((1, heads, n, n), lambda b, hb, c: (b, hb, 0, 0))
    vecs = [z.reshape(1, d) for z in (k_k, k_a, r_k, lnx_g, lnx_b)]
    return pl.pallas_call(
        functools.partial(_wkv_kernel, chunk=chunk, heads=heads, passes=passes),
        grid=(bsz, hh // heads, t // chunk),
        in_specs=[seq] * 6 + [vec] * 5 + [st],
        out_specs=[seq, st],
        out_shape=[jax.ShapeDtypeStruct((bsz, t, d), F32), jax.ShapeDtypeStruct((bsz, hh, n, n), F32)],
        compiler_params=_cparams(3),
    )(r, lw, k, v, a, g, *vecs, s0)


def _route_kernel(x_ref, rt_ref, bias_ref, eidx_ref, w_ref, pos_ref, cnt_ref, *, n_valid):
    ne = rt_ref.shape[0]
    per = ne // N_GROUPS
    logits = _dot(rt_ref[...], x_ref[...], 1, _NT)
    s = jax.nn.sigmoid(logits)
    sb = s + bias_ref[...]
    tm = sb.shape[1]
    sub = lax.broadcasted_iota(jnp.int32, (per, tm), 0)
    gs_rows = []
    for gi in range(N_GROUPS):
        tile = sb[gi * per:(gi + 1) * per, :]
        m1 = jnp.max(tile, axis=0, keepdims=True)
        first = jnp.min(jnp.where(tile == m1, sub, per), axis=0, keepdims=True)
        m2 = jnp.max(jnp.where(sub == first, -jnp.inf, tile), axis=0, keepdims=True)
        gs_rows.append(m1 + m2)
    keep_rows = []
    for gi in range(N_GROUPS):
        cnt = jnp.zeros((1, tm), jnp.int32)
        for gj in range(N_GROUPS):
            if gj == gi:
                continue
            beats = (gs_rows[gj] > gs_rows[gi]) if gj > gi else (gs_rows[gj] >= gs_rows[gi])
            cnt = cnt + beats.astype(jnp.int32)
        keep_rows.append(jnp.broadcast_to(cnt, (per, tm)))
    masked = jnp.where(jnp.concatenate(keep_rows, axis=0) < TOPK_GROUPS, sb, -1e9)
    eidx = lax.broadcasted_iota(jnp.int32, (ne, tm), 0)
    rank = jnp.zeros((ne, tm), jnp.int32)
    for ej in range(ne):
        rowv = masked[ej:ej + 1, :]
        rank = rank + jnp.where(eidx > ej, (rowv >= masked).astype(jnp.int32), (rowv > masked).astype(jnp.int32))
    tok = pl.program_id(0) * tm + lax.broadcasted_iota(jnp.int32, (ne, tm), 1)
    rank = jnp.where(tok < n_valid, rank, ne)
    sel = rank < TOP_K
    self32 = sel.astype(F32)
    den = jnp.sum(jnp.where(sel, s, 0.0), axis=0, keepdims=True)
    wgt = jnp.where(sel, s / den * ROUTED_SCALE, 0.0)

    @pl.when(pl.program_id(0) == 0)
    def _():
        cnt_ref[...] = jnp.zeros(cnt_ref.shape, F32)

    upper = (lax.broadcasted_iota(jnp.int32, (tm, tm), 0) < lax.broadcasted_iota(jnp.int32, (tm, tm), 1)).astype(BF16)
    carry = cnt_ref[...]
    pos = jnp.dot(self32.astype(BF16), upper, preferred_element_type=F32) + carry[:, :1]
    cnt_ref[...] = carry + jnp.sum(self32, axis=1, keepdims=True)
    eidx_f = eidx.astype(F32)
    rows_e, rows_w, rows_p = [], [], []
    for kk in range(TOP_K):
        hit = rank == kk
        rows_e.append(jnp.sum(jnp.where(hit, eidx_f, 0.0), axis=0, keepdims=True))
        rows_w.append(jnp.sum(jnp.where(hit, wgt, 0.0), axis=0, keepdims=True))
        rows_p.append(jnp.sum(jnp.where(hit, pos, 0.0), axis=0, keepdims=True))
    eidx_ref[...] = jnp.concatenate(rows_e, axis=0).astype(jnp.int32)
    w_ref[...] = jnp.concatenate(rows_w, axis=0)
    pos_ref[...] = jnp.concatenate(rows_p, axis=0).astype(jnp.int32)


def route(x, router_t, bias, n_valid):
    m, d = x.shape
    ne = router_t.shape[0]
    tm = _pick_tile(m, (256, 128))
    out = pl.BlockSpec((TOP_K, tm), lambda i: (0, i))
    return pl.pallas_call(
        functools.partial(_route_kernel, n_valid=n_valid),
        grid=(m // tm,),
        in_specs=[pl.BlockSpec((tm, d), lambda i: (i, 0)), pl.BlockSpec((ne, d), lambda i: (0, 0)),
                  pl.BlockSpec((ne, 1), lambda i: (0, 0))],
        out_specs=[out, out, out, pl.BlockSpec((ne, LANES), lambda i: (0, 0))],
        out_shape=[jax.ShapeDtypeStruct((TOP_K, m), jnp.int32), jax.ShapeDtypeStruct((TOP_K, m), F32),
                   jax.ShapeDtypeStruct((TOP_K, m), jnp.int32), jax.ShapeDtypeStruct((ne, LANES), F32)],
        compiler_params=_cparams(1),
    )(x, router_t, bias.reshape(ne, 1))


def _experts_kernel(be_ref, nu_ref, xs_ref, win_ref, wout_ref, o_ref, win_s, wout_s):
    i = pl.program_id(0)
    f = wout_ref.shape[0]

    @pl.when(i < nu_ref[0])
    def _():
        changed = jnp.logical_or(i == 0, be_ref[i] != be_ref[jnp.maximum(i - 1, 0)])

        @pl.when(changed)
        def _():
            win_s[...] = win_ref[...].astype(BF16)
            wout_s[...] = wout_ref[...].astype(BF16)

        hu = jnp.dot(xs_ref[...].astype(BF16), win_s[...], preferred_element_type=F32)
        hid = jax.nn.silu(hu[:, :f]) * hu[:, f:]
        y = jnp.dot(hid.astype(BF16), wout_s[...], preferred_element_type=F32)
        o_ref[...] = y.astype(BF16).astype(F32)

    @pl.when(i >= nu_ref[0])
    def _():
        o_ref[...] = jnp.zeros(o_ref.shape, F32)


def experts(xs, blk_expert, n_used, w_in, w_out, layer, tm):
    rws, d = xs.shape
    f2 = w_in.shape[3]
    f = f2 // 2
    grid_spec = pltpu.PrefetchScalarGridSpec(
        num_scalar_prefetch=2,
        grid=(rws // tm,),
        in_specs=[
            pl.BlockSpec((tm, d), lambda i, be, nu: (jnp.minimum(i, nu[0] - 1), 0)),
            pl.BlockSpec((None, None, d, f2), lambda i, be, nu: (layer, be[i], 0, 0)),
            pl.BlockSpec((None, None, f, d), lambda i, be, nu: (layer, be[i], 0, 0)),
        ],
        out_specs=pl.BlockSpec((tm, d), lambda i, be, nu: (i, 0)),
        scratch_shapes=[pltpu.VMEM((d, f2), BF16), pltpu.VMEM((f, d), BF16)],
    )
    return pl.pallas_call(
        _experts_kernel,
        grid_spec=grid_spec,
        out_shape=jax.ShapeDtypeStruct((rws, d), F32),
        compiler_params=_cparams(1),
    )(blk_expert, n_used, xs, w_in, w_out)


def moe_ffn(x, n_valid, layer, router, bias, w_in, w_out, ws_in, ws_out):
    m, d = x.shape
    ne = router.shape[2]
    eidx, ew, posk, cnt = route(x, router[layer].T, bias[layer], n_valid)
    tm = 256 if m * TOP_K // ne >= 256 else _round_up(max(m * TOP_K // ne, SUBLANES), SUBLANES)
    nk = m * TOP_K
    counts = cnt[:, 0].astype(jnp.int32)
    padded = (counts + tm - 1) // tm * tm
    pad_end = jnp.cumsum(padded)
    pad_start = pad_end - padded
    n_blocks = (nk + tm - 1) // tm + ne
    n_rows = n_blocks * tm
    tok = jnp.arange(m, dtype=jnp.int32)
    dest = jnp.where(tok[None, :] < n_valid, jnp.take(pad_start, eidx) + posk, n_rows - 1).astype(jnp.int32)
    token_of_row = jnp.zeros((n_rows,), jnp.int32).at[dest.reshape(-1)].set(jnp.tile(tok, TOP_K))
    blk_start = jnp.arange(n_blocks, dtype=jnp.int32) * tm
    blk_expert = jnp.minimum(jnp.sum((pad_end[None, :] <= blk_start[:, None]).astype(jnp.int32), axis=1), ne - 1)
    n_used = (pad_end[-1:] // tm).astype(jnp.int32)
    xs = jnp.take(x, token_of_row, axis=0)
    yb = experts(xs, blk_expert, n_used, w_in, w_out, layer, tm)
    wr = ew.astype(BF16).astype(F32)
    routed = jnp.sum(jnp.take(yb, dest.T, axis=0) * wr.T[:, :, None], axis=1)
    hid = linear(x, ws_in, widx=(layer,), glu=True, post=lambda a, b: jax.nn.silu(a) * b)
    return linear(hid, ws_out, widx=(layer,), fulls=(routed,), post=lambda acc, rt: acc + rt)


def _gelu_tanh(x):
    return 0.5 * x * (1.0 + jnp.tanh(0.7978845608028654 * (x + 0.044715 * x * x * x)))


def _group_tile(ref, g):
    if len(ref.shape) == 3:
        return ref[:, g, :]
    return ref[:, g * HEAD_DIM:(g + 1) * HEAD_DIM]


def _page_spec(pages, index_fn, slot_fn):
    if pages.ndim == 5:
        return pl.BlockSpec((None, PAGE, None, N_KV, HEAD_DIM), lambda *a: (index_fn(*a), 0, slot_fn(*a), 0, 0))
    return pl.BlockSpec((None, PAGE, N_KV * HEAD_DIM), lambda *a: (index_fn(*a), 0, slot_fn(*a)))


def _compress_kernel(tbl_ref, *refs, group, ppt):
    pg_refs = refs[:ppt]
    pos_ref, w1_ref, w2_ref, o_ref, w1_s = refs[ppt:ppt + 5]
    slabs = refs[ppt + 5:]
    p = pl.program_id(2)
    pin = p % (group // ppt)

    @pl.when(p == 0)
    def _():
        w1_s[...] = w1_ref[0].astype(BF16)

    for pi, pg_ref in enumerate(pg_refs):
        for g in range(N_KV):
            slabs[g][pl.ds(pl.multiple_of((pin * ppt + pi) * PAGE, PAGE), PAGE), :] = _group_tile(pg_ref, g)

    @pl.when(pin == group // ppt - 1)
    def _():
        per_page = PAGE // CMP_BLOCK
        nrow = group * per_page
        hid = w1_ref.shape[2]
        acc = jnp.zeros((N_KV * nrow, hid), F32)
        for t in range(CMP_BLOCK):
            lhs = jnp.concatenate([slabs[g][pl.ds(t, nrow, stride=CMP_BLOCK), :] for g in range(N_KV)], axis=0)
            lhs = (lhs + pos_ref[0, t:t + 1, :]).astype(BF16)
            acc = acc + jnp.dot(lhs, w1_s[t * HEAD_DIM:(t + 1) * HEAD_DIM, :], preferred_element_type=F32)
        out = _dot(_gelu_tanh(acc), w2_ref[0])
        for g in range(N_KV):
            o_ref[0, 0, :, g * HEAD_DIM:(g + 1) * HEAD_DIM] = out[g * nrow:(g + 1) * nrow]


def compress(pages, table, slot0, cmp_pos, cmp_w1, cmp_w2):
    bsz, n_pg = table.shape
    group = min(32, n_pg)
    ppt = _pick_tile(group, (4, 2, 1))
    gw = N_KV * HEAD_DIM
    per_page = PAGE // CMP_BLOCK
    hid = cmp_w1.shape[2]

    def page_of(pi):
        return lambda b, s, p, tbl: tbl[b * n_pg + p * ppt + pi]

    grid_spec = pltpu.PrefetchScalarGridSpec(
        num_scalar_prefetch=1,
        grid=(bsz, 2, n_pg // ppt),
        in_specs=[_page_spec(pages, page_of(pi), lambda b, s, p, tbl: slot0 + s) for pi in range(ppt)] + [
            pl.BlockSpec((1, CMP_BLOCK, HEAD_DIM), lambda b, s, p, tbl: (s, 0, 0)),
            pl.BlockSpec((1, CMP_BLOCK * HEAD_DIM, hid), lambda b, s, p, tbl: (s, 0, 0)),
            pl.BlockSpec((1, hid, HEAD_DIM), lambda b, s, p, tbl: (s, 0, 0)),
        ],
        out_specs=pl.BlockSpec((1, 1, group * per_page, gw), lambda b, s, p, tbl: (b, s, p // (group // ppt), 0)),
        scratch_shapes=[pltpu.VMEM((CMP_BLOCK * HEAD_DIM, hid), BF16)]
        + [pltpu.VMEM((group * PAGE, HEAD_DIM), F32) for _ in range(N_KV)],
    )
    return pl.pallas_call(
        functools.partial(_compress_kernel, group=group, ppt=ppt),
        grid_spec=grid_spec,
        out_shape=jax.ShapeDtypeStruct((bsz, 2, n_pg * per_page, gw), F32),
        compiler_params=_cparams(3),
    )(table.reshape(-1), *([pages] * ppt), cmp_pos, cmp_w1, cmp_w2)


def _nsa_cmp_kernel(q_ref, kc_ref, vc_ref, gate_ref, o_ref, sel_ref, score_s, rank_s, *, q_offset, n_sel_blocks,
                    tq, half):
    qt = pl.program_id(1)
    ncp = kc_ref.shape[1]
    nsp = sel_ref.shape[2]
    scale = HEAD_DIM ** -0.5
    rpg = N_KV
    n_heads = q_ref.shape[2] // HEAD_DIM
    rpg = n_heads // N_KV
    qpos = q_offset + qt * tq + lax.broadcasted_iota(jnp.int32, (1, tq), 1)
    cidx = lax.broadcasted_iota(jnp.int32, (ncp, 1), 0)
    corig = jnp.where(cidx < half, 2 * cidx, 2 * (cidx - half) + 1)
    cmask = ((corig + 1) * CMP_BLOCK - 1) <= qpos
    blk = lax.broadcasted_iota(jnp.int32, (nsp, 1), 0)
    cur = qpos // SEL_BLOCK
    forced = (blk == 0) | (blk == cur) | (blk == cur - 1)
    valid = (blk * SEL_BLOCK) <= qpos
    gate = gate_ref[0]
    for g in range(N_KV):
        kc = kc_ref[0, :, g * HEAD_DIM:(g + 1) * HEAD_DIM]
        vc = vc_ref[0, :, g * HEAD_DIM:(g + 1) * HEAD_DIM]
        imp = jnp.zeros((ncp, tq), F32)
        for h in range(rpg):
            hd = g * rpg + h
            qh = q_ref[0, :, hd * HEAD_DIM:(hd + 1) * HEAD_DIM]
            s = _dot(kc, qh, 1, _NT) * scale
            s = jnp.where(cmask, s, NEG)
            mx = jnp.max(s, axis=0, keepdims=True)
            e = jnp.where(cmask, jnp.exp(s - mx), 0.0)
            p = e / jnp.maximum(jnp.sum(e, axis=0, keepdims=True), 1e-30)
            imp = imp + p
            oc = _dot(p, vc, 1, _TN)
            o_ref[0, :, hd * HEAD_DIM:(hd + 1) * HEAD_DIM] = gate[:, hd * 3:hd * 3 + 1] * oc
        imp_blk = imp[:half] + imp[half:]
        if nsp > half:
            imp_blk = jnp.concatenate([imp_blk, jnp.zeros((nsp - half, tq), F32)], axis=0)
        else:
            imp_blk = imp_blk[:nsp]
        score = jnp.where(valid, imp_blk + FORCED_BONUS * forced.astype(F32), -1.0)
        score = jnp.where(blk < n_sel_blocks, score, -2.0)
        score_s[...] = score
        rank_s[...] = jnp.zeros((nsp, tq), jnp.int32)

        def body(j, carry):
            rowv = score_s[pl.ds(j, 1), :]
            rank_s[...] = rank_s[...] + jnp.where(blk > j, (rowv >= score).astype(jnp.int32),
                                                  (rowv > score).astype(jnp.int32))
            return carry

        lax.fori_loop(0, n_sel_blocks, body, 0)
        sel_ref[0, g] = (rank_s[...] < N_SEL).astype(F32)


def nsa_cmp(q, kc, vc, gates, *, q_offset, n_sel_blocks, tq):
    bsz, t, dq = q.shape
    ncp = kc.shape[1]
    half = ncp // 2
    nsp = _round_up(n_sel_blocks, SUBLANES)
    gw = N_KV * HEAD_DIM
    qspec = pl.BlockSpec((1, tq, dq), lambda b, i: (b, i, 0))
    cspec = pl.BlockSpec((1, ncp, gw), lambda b, i: (b, 0, 0))
    return pl.pallas_call(
        functools.partial(_nsa_cmp_kernel, q_offset=q_offset, n_sel_blocks=n_sel_blocks, tq=tq, half=half),
        grid=(bsz, t // tq),
        in_specs=[qspec, cspec, cspec, pl.BlockSpec((1, tq, gates.shape[2]), lambda b, i: (b, i, 0))],
        out_specs=[qspec, pl.BlockSpec((1, N_KV, nsp, tq), lambda b, i: (b, 0, 0, i))],
        out_shape=[jax.ShapeDtypeStruct((bsz, t, dq), F32), jax.ShapeDtypeStruct((bsz, N_KV, nsp, t), F32)],
        scratch_shapes=[pltpu.VMEM((nsp, tq), F32), pltpu.VMEM((nsp, tq), jnp.int32)],
        compiler_params=_cparams(2),
    )(q, kc, vc, gates)


def _nsa_attn_kernel(*refs, mode, q_offset, tq, n_kt, ppt, has_tail, tail_pos0, branch):
    tbl_ref, kp0_ref = refs[0], refs[1]
    it = iter(refs[2:])
    q_ref = next(it)
    k_refs = [next(it) for _ in range(ppt)]
    v_refs = [next(it) for _ in range(ppt)]
    tk_ref = next(it) if has_tail else None
    tv_ref = next(it) if has_tail else None
    sel_ref = next(it) if mode == "sel" else None
    gate_ref = next(it)
    prev_ref = next(it)
    o_ref = next(it)
    m_s, l_s, acc_s = next(it), next(it), next(it)

    qt = pl.program_id(1)
    j = pl.program_id(2)
    n_heads = q_ref.shape[2] // HEAD_DIM
    rpg = n_heads // N_KV
    scale = HEAD_DIM ** -0.5
    width = WINDOW if mode == "win" else (1 << 30)
    n_steps = n_kt // ppt
    G = range(N_KV)

    @pl.when(j == 0)
    def _():
        m_s[...] = jnp.full(m_s.shape, NEG, F32)
        l_s[...] = jnp.zeros(l_s.shape, F32)
        acc_s[...] = jnp.zeros(acc_s.shape, F32)

    q_first = q_offset + qt * tq
    qpos = q_first + lax.broadcasted_iota(jnp.int32, (tq, 1), 0)

    def tile(kt_refs, vt_refs, kp0s):
        lane = lax.broadcasted_iota(jnp.int32, (1, PAGE), 1)
        kpos = jnp.concatenate([jnp.where(kp >= 0, kp, 1 << 30) + lane for kp in kp0s], axis=1)
        dist = qpos - kpos
        band = (dist >= 0) & (dist < width)
        if mode == "sel":
            nsp = sel_ref.shape[2]
            blk = lax.broadcasted_iota(jnp.int32, (nsp, 1), 0)
            expand = (blk == kpos // SEL_BLOCK).astype(BF16)
            hits = [_dg(sel_ref[0, g].astype(BF16), expand, _TN) for g in G]
            masks = [band & (hits[g] > 0.5) for g in G]
        else:
            masks = [band for g in G]
        masks = [jnp.concatenate([mk] * rpg, axis=0) for mk in masks]
        qg = [jnp.concatenate([q_ref[0, :, (g * rpg + h) * HEAD_DIM:(g * rpg + h + 1) * HEAD_DIM] for h in range(rpg)],
                              axis=0) for g in G]
        kg = [jnp.concatenate([_group_tile(r, g) for r in kt_refs], axis=0) for g in G]
        vg = [jnp.concatenate([_group_tile(r, g) for r in vt_refs], axis=0) for g in G]
        s = [_dot(qg[g], kg[g], 1, _NT) * scale for g in G]
        s = [jnp.where(masks[g], s[g], NEG) for g in G]
        m_old = [m_s[g] for g in G]
        m_new = [jnp.maximum(m_old[g], jnp.max(s[g], -1, keepdims=True)) for g in G]
        alpha = [jnp.exp(m_old[g] - m_new[g]) for g in G]
        p = [jnp.where(masks[g], jnp.exp(s[g] - m_new[g]), 0.0) for g in G]
        pv = [_dot(p[g], vg[g], 1) for g in G]
        for g in G:
            l_s[g] = alpha[g] * l_s[g] + jnp.sum(p[g], -1, keepdims=True)
            acc_s[g] = alpha[g] * acc_s[g] + pv[g]
            m_s[g] = m_new[g]

    base = (pl.program_id(0) * pl.num_programs(1) + qt) * n_kt + jnp.minimum(j, n_steps - 1) * ppt
    kp0s = [kp0_ref[base + pi] for pi in range(ppt)]
    live = None
    for kp in kp0s:
        ok = (kp >= 0) & (kp <= q_first + tq - 1) & (kp + PAGE - 1 > q_first - width)
        live = ok if live is None else (live | ok)
    if has_tail:
        live = live & (j < n_steps)

    @pl.when(live)
    def _():
        tile(k_refs, v_refs, kp0s)

    if has_tail:
        @pl.when(j == n_steps)
        def _():
            tile([tk_ref], [tv_ref], [tail_pos0])

    @pl.when(j == pl.num_programs(2) - 1)
    def _():
        gate = gate_ref[0]
        for g in range(N_KV):
            o = acc_s[g] / jnp.maximum(l_s[g], 1e-30)
            for h in range(rpg):
                hd = g * rpg + h
                cs = slice(hd * HEAD_DIM, (hd + 1) * HEAD_DIM)
                o_ref[0, :, cs] = prev_ref[0, :, cs] + gate[:, hd * 3 + branch:hd * 3 + branch + 1] * o[h * tq:(h + 1) * tq]


def nsa_attn(q, pages, tbl, kp0, slot_k, slot_v, tail, sel, gates, prev, *, mode, q_offset, tq, tail_pos0, branch,
             ppt=1):
    bsz, t, dq = q.shape
    n_qt = t // tq
    n_kt = tbl.shape[2]
    assert n_kt % ppt == 0
    gw = N_KV * HEAD_DIM
    n_heads = dq // HEAD_DIM
    rows = (n_heads // N_KV) * tq
    has_tail = tail is not None
    n_steps = n_kt // ppt + (1 if has_tail else 0)

    def page_idx(pi):
        return lambda b, i, j, tb, kp: tb[(b * n_qt + i) * n_kt + jnp.minimum(j, n_kt // ppt - 1) * ppt + pi]

    qspec = pl.BlockSpec((1, tq, dq), lambda b, i, j, tb, kp: (b, i, 0))
    in_specs = [qspec]
    in_specs += [_page_spec(pages, page_idx(pi), lambda *a: slot_k) for pi in range(ppt)]
    in_specs += [_page_spec(pages, page_idx(pi), lambda *a: slot_v) for pi in range(ppt)]
    args = [q] + [pages] * (2 * ppt)
    if has_tail:
        in_specs += [pl.BlockSpec((None, PAGE, gw), lambda b, i, j, tb, kp: (b, 0, 0)),
                     pl.BlockSpec((None, PAGE, gw), lambda b, i, j, tb, kp: (b, 0, 1))]
        args += [tail, tail]
    if mode == "sel":
        nsp = sel.shape[2]
        in_specs.append(pl.BlockSpec((1, N_KV, nsp, tq), lambda b, i, j, tb, kp: (b, 0, 0, i)))
        args.append(sel)
    in_specs += [pl.BlockSpec((1, tq, gates.shape[2]), lambda b, i, j, tb, kp: (b, i, 0)), qspec]
    args += [gates, prev]
    grid_spec = pltpu.PrefetchScalarGridSpec(
        num_scalar_prefetch=2,
        grid=(bsz, n_qt, n_steps),
        in_specs=in_specs,
        out_specs=qspec,
        scratch_shapes=[pltpu.VMEM((N_KV, rows, 1), F32), pltpu.VMEM((N_KV, rows, 1), F32),
                        pltpu.VMEM((N_KV, rows, HEAD_DIM), F32)],
    )
    return pl.pallas_call(
        functools.partial(_nsa_attn_kernel, mode=mode, q_offset=q_offset, tq=tq, n_kt=n_kt, ppt=ppt,
                          has_tail=has_tail, tail_pos0=tail_pos0, branch=branch),
        grid_spec=grid_spec,
        out_shape=jax.ShapeDtypeStruct((bsz, t, dq), F32),
        compiler_params=_cparams(3),
    )(tbl.reshape(-1).astype(jnp.int32), kp0.reshape(-1).astype(jnp.int32), *args)


def _softplus(z):
    return jnp.maximum(z, 0.0) + jnp.log(1.0 + jnp.exp(-jnp.abs(z)))


def _post_logdecay(acc, w0):
    return -jnp.exp(-_softplus(-(w0 + acc)) - 0.5)


def _post_sig_bias(acc, b):
    return jax.nn.sigmoid(acc + b)


def _post_vres(acc, v0, v, vf):
    return v + (vf - v) * jax.nn.sigmoid(v0 + acc)


def _post_ple(acc, b, x, t):
    return x + jax.nn.sigmoid(acc + b) * t


def _pad_rows(x, m):
    return jnp.pad(x, ((0, m - x.shape[0]),) + ((0, 0),) * (x.ndim - 1))


def kernel(x_prompt, x_sample, cache_kv_pages, cache_win_kv, state_wkv, state_shift, page_table, p_prompt, p_sample, ln_g, ln_b, rw_mix, rw_w_rkv, rw_w0, rw_w1, rw_w2, rw_a0, rw_a1, rw_a2, rw_v0, rw_v1, rw_v2, rw_g1, rw_g2, rw_k_k, rw_k_a, rw_r_k, rw_lnx_g, rw_lnx_b, rw_w_o, kv_ln_g, kv_ln_b, w_kv, cmp_pos, cmp_w1, cmp_w2, nsa_w_qg, nsa_b_g, nsa_w_o, moe_router, moe_bias, moe_w_in, moe_w_out, moe_ws_in, moe_ws_out, ple_proj, ple_gate, ple_gate_b):
    bp, tp, d = x_prompt.shape
    bs, ts, _ = x_sample.shape
    depth = ln_g.shape[0]
    n_a = state_wkv.shape[0]
    alpha = (2 * depth) ** 0.25
    n_p, n_s = bp * tp, bs * ts
    n_tok = n_p + n_s
    m = _round_up(n_tok, 768) if n_tok > 768 else _round_up(n_tok, SUBLANES)
    ts_pad = _round_up(ts, SUBLANES)
    hh = d // RW_HEAD
    gw = N_KV * HEAD_DIM
    n_pg_s = page_table.shape[1]
    past_len = n_pg_s * cache_kv_pages.shape[1]
    win_buf = cache_win_kv.shape[1]
    assert cache_kv_pages.shape[1] == PAGE and tp % PAGE == 0 and win_buf % PAGE == 0 and ts < CMP_BLOCK

    def join(a_p, a_s):
        return _pad_rows(jnp.concatenate([a_p.reshape(n_p, -1), a_s.reshape(n_s, -1)], axis=0), m)

    def split_seq(z):
        zp = z[:n_p].reshape(bp, tp, -1)
        zs = jnp.pad(z[n_p:n_tok].reshape(bs, ts, -1), ((0, 0), (0, ts_pad - ts), (0, 0)))
        return zp, zs

    def merge_seq(zp, zs):
        return join(zp, zs[:, :ts])

    x = join(x_prompt, x_sample)
    pe = jnp.concatenate([p_prompt.reshape(depth, n_p, -1), p_sample.reshape(depth, n_s, -1)], axis=1)
    pe = jnp.pad(pe, ((0, 0), (0, m - n_tok), (0, 0)))

    shifts_p, shifts_s, wkvs_p, wkvs_s = [], [], [], []
    v_first = None
    ctx = None
    kv_rows_p = kv_rows_s = win_p = win_s = None
    for i in range(depth):
        if i < n_a:
            scan_passes = 3 if i == 0 else 1
            xp_, xs_ = x[:n_p].reshape(bp, tp, d), x[n_p:n_tok].reshape(bs, ts, d)
            shifts_p.append(xp_[:, -1])
            shifts_s.append(xs_[:, -1])
            prev = join(jnp.concatenate([jnp.zeros((bp, 1, d), F32), xp_[:, :-1]], axis=1),
                        jnp.concatenate([state_shift[i][:, None, :], xs_[:, :-1]], axis=1))
            mix = rw_mix[i]
            lin = linear
            r = lin(x, rw_w_rkv, widx=(i, 0), mix=(prev, mix[0]))
            k = lin(x, rw_w_rkv, widx=(i, 1), mix=(prev, mix[2]))
            v = lin(x, rw_w_rkv, widx=(i, 2), mix=(prev, mix[3]))
            lw = lin(lin(x, rw_w1, widx=(i,), mix=(prev, mix[1]), post=jnp.tanh), rw_w2, widx=(i,), rows=(rw_w0[i],),
                     post=_post_logdecay)
            if i == 0:
                v_first = v
            else:
                v = lin(lin(x, rw_v1, widx=(i - 1,), mix=(prev, mix[3])), rw_v2, widx=(i - 1,), rows=(rw_v0[i - 1],),
                        fulls=(v, v_first), post=_post_vres)
            a = lin(lin(x, rw_a1, widx=(i,), mix=(prev, mix[4])), rw_a2, widx=(i,), rows=(rw_a0[i],),
                    post=_post_sig_bias)
            g = lin(lin(x, rw_g1, widx=(i,), mix=(prev, mix[5]), post=jax.nn.sigmoid), rw_g2, widx=(i,))
            seqs = [split_seq(z) for z in (r, lw, k, v, a, g)]
            vecs = (rw_k_k[i], rw_k_a[i], rw_r_k[i].reshape(-1), rw_lnx_g[i], rw_lnx_b[i])
            y_p, s_p = wkv7(*[z[0] for z in seqs], *vecs, jnp.zeros((bp, hh, RW_HEAD, RW_HEAD), F32),
                            passes=scan_passes)
            y_s, s_s = wkv7_seq(*[z[1] for z in seqs], *vecs, state_wkv[i], steps=ts)
            wkvs_p.append(s_p)
            wkvs_s.append(s_s)
            h = lin(merge_seq(y_p, y_s), rw_w_o, widx=(i,))
        else:
            jn = i - n_a
            if ctx is None:
                rows = linear(layer_norm(x, kv_ln_g, kv_ln_b), w_kv)
                rows_p = rows[:n_p].reshape(bp, tp, 6 * gw)
                rows_s = rows[n_p:n_tok].reshape(bs, ts, 6 * gw)
                kv_rows_p = rows_p[:, :, :4 * gw].reshape(bp, tp, 4, N_KV, HEAD_DIM)
                kv_rows_s = rows_s[:, :, :4 * gw].reshape(bs, ts, 4, N_KV, HEAD_DIM)
                win_p = rows_p[:, max(tp - WINDOW, 0):, 4 * gw:].reshape(bp, -1, 2, N_KV, HEAD_DIM)
                new_win = rows_s[:, :, 4 * gw:].reshape(bs, ts, 2, N_KV, HEAD_DIM)
                win_s = jnp.concatenate([cache_win_kv, new_win], axis=1)[:, ts:]
                n_pg_p = tp // PAGE
                pages_p = rows[:n_p].reshape(bp * n_pg_p, PAGE, 6 * gw)
                table_p = jnp.arange(bp * n_pg_p, dtype=jnp.int32).reshape(bp, n_pg_p)
                pages_s = cache_kv_pages
                pages_w = cache_win_kv.reshape(bs * (win_buf // PAGE), PAGE, 2, N_KV, HEAD_DIM)

                def perm(c):
                    return jnp.concatenate([c[:, 0::2], c[:, 1::2]], axis=1)

                cmp_p = compress(pages_p, table_p, 0, cmp_pos, cmp_w1, cmp_w2)
                cmp_s = compress(pages_s, page_table.astype(jnp.int32), 0, cmp_pos, cmp_w1, cmp_w2)
                tail_sel = jnp.pad(rows_s[:, :, 2 * gw:4 * gw], ((0, 0), (0, PAGE - ts), (0, 0)))
                tail_win = jnp.pad(rows_s[:, :, 4 * gw:6 * gw], ((0, 0), (0, PAGE - ts), (0, 0)))
                tq_p = PAGE
                n_qt = tp // tq_p
                qi = jnp.arange(n_qt, dtype=jnp.int32)[:, None]
                kj = jnp.arange(n_pg_p, dtype=jnp.int32)[None, :]
                base = (jnp.arange(bp, dtype=jnp.int32) * n_pg_p)[:, None, None]
                tbl_sel_p = base + jnp.minimum(kj, qi)[None]
                kp0_sel_p = jnp.broadcast_to(jnp.where(kj <= qi, kj * PAGE, -1)[None], tbl_sel_p.shape)
                n_wt = WINDOW // PAGE + 1
                kw = qi - (n_wt - 1) + jnp.arange(n_wt, dtype=jnp.int32)[None, :]
                tbl_win_p = base + jnp.maximum(kw, 0)[None]
                kp0_win_p = jnp.broadcast_to(jnp.where(kw >= 0, kw * PAGE, -1)[None], tbl_win_p.shape)
                tbl_sel_s = page_table.astype(jnp.int32)[:, None, :]
                kp0_sel_s = jnp.broadcast_to((jnp.arange(n_pg_s, dtype=jnp.int32) * PAGE)[None, None], tbl_sel_s.shape)
                n_ww = win_buf // PAGE
                tbl_win_s = jnp.arange(bs * n_ww, dtype=jnp.int32).reshape(bs, 1, n_ww)
                kp0_win_s = jnp.broadcast_to(
                    (past_len - win_buf + jnp.arange(n_ww, dtype=jnp.int32) * PAGE)[None, None], tbl_win_s.shape)
                ctx = dict(kc_p=perm(cmp_p[:, 0]), vc_p=perm(cmp_p[:, 1]), kc_s=perm(cmp_s[:, 0]), vc_s=perm(cmp_s[:, 1]))
            dq = nsa_w_qg.shape[2] - nsa_b_g.shape[1]
            q = linear(x, nsa_w_qg, widx=(jn,), ncols=dq)
            gates = linear(x, nsa_w_qg[jn][:, dq:], rows=(nsa_b_g[jn],), post=_post_sig_bias)
            (q_p, q_s), (g_p, g_s) = split_seq(q), split_seq(gates)
            ns_p = -(-tp // SEL_BLOCK)
            ns_s = -(-(past_len + ts) // SEL_BLOCK)
            o_p, sel_p = nsa_cmp(q_p, ctx['kc_p'], ctx['vc_p'], g_p, q_offset=0, n_sel_blocks=ns_p, tq=tq_p)
            o_p = nsa_attn(q_p, pages_p, tbl_sel_p, kp0_sel_p, 2, 3, None, sel_p, g_p, o_p, mode="sel", q_offset=0,
                           tq=tq_p, tail_pos0=0, branch=1)
            o_p = nsa_attn(q_p, pages_p, tbl_win_p, kp0_win_p, 4, 5, None, None, g_p, o_p, mode="win", q_offset=0,
                           tq=tq_p, tail_pos0=0, branch=2)
            o_s, sel_s = nsa_cmp(q_s, ctx['kc_s'], ctx['vc_s'], g_s, q_offset=past_len, n_sel_blocks=ns_s, tq=ts_pad)
            o_s = nsa_attn(q_s, pages_s, tbl_sel_s, kp0_sel_s, 2, 3, tail_sel, sel_s, g_s, o_s, mode="sel",
                           q_offset=past_len, tq=ts_pad, tail_pos0=past_len, branch=1,
                           ppt=_pick_tile(n_pg_s, (4, 2, 1)))
            o_s = nsa_attn(q_s, pages_w, tbl_win_s, kp0_win_s, 0, 1, tail_win, None, g_s, o_s, mode="win",
                           q_offset=past_len, tq=ts_pad, tail_pos0=past_len, branch=2,
                           ppt=_pick_tile(win_buf // PAGE, (4, 2, 1)))
            h = linear(merge_seq(o_p, o_s), nsa_w_o, widx=(jn,))
        x = layer_norm(x, ln_g[i, 0], ln_b[i, 0], h=h, alpha=alpha)
        mo = moe_ffn(x, n_tok, i, moe_router, moe_bias, moe_w_in, moe_w_out, moe_ws_in, moe_ws_out)
        x = layer_norm(x, ln_g[i, 1], ln_b[i, 1], h=mo, alpha=alpha)
        t = linear(pe, ple_proj, xidx=(i,), widx=(i,))
        x = linear(x, ple_gate, widx=(i,), rows=(ple_gate_b[i],), fulls=(x, t), post=_post_ple)

    y_p = x[:n_p].reshape(bp, tp, d)
    y_s = x[n_p:n_tok].reshape(bs, ts, d)
    return (y_p, y_s, kv_rows_p, kv_rows_s, win_p, win_s, jnp.stack(wkvs_p), jnp.stack(wkvs_s),
            jnp.stack(shifts_p), jnp.stack(shifts_s))
```

```python
import functools

import jax
import jax.numpy as jnp
from jax import lax
from jax.experimental import pallas as pl
from jax.experimental.pallas import tpu as pltpu

F32 = jnp.float32
BF16 = jnp.bfloat16

RW_HEAD = 64
GN_EPS = 64e-5
N_KV = 4
HEAD_DIM = 128
CMP_BLOCK = 32
SEL_BLOCK = 64
N_SEL = 16
WINDOW = 512
FORCED_BONUS = 100.0
N_GROUPS = 8
TOPK_GROUPS = 4
TOP_K = 8
ROUTED_SCALE = 2.5
LN_EPS = 1e-5
NEG = -1e30

LANES = 128
SUBLANES = 8
PAGE = 128
VMEM_LIMIT = 56 * 1024 * 1024
LINEAR_W_BYTES = 16 * 1024 * 1024
LINEAR_VMEM_BYTES = 46 * 1024 * 1024


def _cparams(n_axes):
    return pltpu.CompilerParams(dimension_semantics=("arbitrary",) * n_axes, vmem_limit_bytes=VMEM_LIMIT)


def _split(x):
    hi = x.astype(BF16)
    lo = (x - hi.astype(F32)).astype(BF16)
    return hi, lo


_NN = (((1,), (0,)), ((), ()))
_NT = (((1,), (1,)), ((), ()))
_TN = (((0,), (0,)), ((), ()))


def _dg(a, b, dims):
    return lax.dot_general(a, b, dims, preferred_element_type=F32)


def _dot(a, b, passes=1, dims=_NN):
    if passes == 1:
        return _dg(a.astype(BF16), b.astype(BF16), dims)
    ah, al = _split(a)
    bh, bl = _split(b)
    return _dg(ah, bh, dims) + (_dg(ah, bl, dims) + _dg(al, bh, dims))


def _round_up(x, m):
    return (x + m - 1) // m * m


def _pick_tile(n, prefs):
    for t in prefs:
        if n % t == 0:
            return t
    return n


def _linear_kernel(*refs, passes, has_mix, glu, n_row, n_full, post):
    it = iter(refs)
    x_ref = next(it)
    xp_ref = next(it) if has_mix else None
    mix_ref = next(it) if has_mix else None
    w_refs = [next(it) for _ in range(2 if glu else 1)]
    row_refs = [next(it) for _ in range(n_row)]
    full_refs = [next(it) for _ in range(n_full)]
    o_ref = next(it)
    scr = list(it)
    per = 2 if passes == 3 else 1

    @pl.when(pl.program_id(1) == 0)
    def _():
        for wi, w_ref in enumerate(w_refs):
            w = w_ref[...]
            if passes == 3:
                hi, lo = _split(w)
                scr[wi * per][...] = hi
                scr[wi * per + 1][...] = lo
            else:
                scr[wi * per][...] = w.astype(BF16)

    x = x_ref[...]
    if has_mix:
        x = x + (xp_ref[...] - x) * mix_ref[...]
    if passes == 3:
        xh, xl = _split(x)
    else:
        xh = x.astype(BF16)
    accs = []
    for wi in range(len(w_refs)):
        wh = scr[wi * per][...]
        acc = jnp.dot(xh, wh, preferred_element_type=F32)
        if passes == 3:
            wl = scr[wi * per + 1][...]
            acc = acc + (jnp.dot(xh, wl, preferred_element_type=F32) + jnp.dot(xl, wh, preferred_element_type=F32))
        accs.append(acc)
    extras = [r[...] for r in row_refs] + [r[...] for r in full_refs]
    o_ref[...] = post(*accs, *extras)


def _identity(acc):
    return acc


def linear(x, w, *, xidx=(), widx=(), ncols=None, passes=1, mix=None, glu=False, rows=(), fulls=(), post=_identity,
           tm=None, tn=None):
    m, k = x.shape[-2:]
    n = (ncols or w.shape[-1]) // (2 if glu else 1)
    lead = (None,) * len(widx)
    n_w = 2 if glu else 1
    tn = tn or next((c for c in (n, n // 2, 512, 256, 128)
                     if n % c == 0 and (c % LANES == 0 or c == n) and n_w * k * c * 4 <= LINEAR_W_BYTES), n)
    w_bytes = n_w * k * tn * (4 + 2 * (2 if passes == 3 else 1))
    per_row = (2 if mix is not None else 1) * k * 4 * 2 + (1 + len(fulls)) * tn * 4 * 2
    tm = tm or next((c for c in (768, 512, 384, 256, 128, 64, 32, 16, 8)
                     if m % c == 0 and w_bytes + c * per_row <= LINEAR_VMEM_BYTES), m)
    nb = n // tn
    grid = (nb, m // tm)
    x_spec = pl.BlockSpec((None,) * len(xidx) + (tm, k), lambda j, i: xidx + (i, 0))
    in_specs = [x_spec]
    args = [x]
    if mix is not None:
        in_specs += [x_spec, pl.BlockSpec((1, k), lambda j, i: (0, 0))]
        args += [mix[0], mix[1].reshape(1, k)]
    in_specs.append(pl.BlockSpec(lead + (k, tn), lambda j, i: widx + (0, j), pipeline_mode=pl.Buffered(1)))
    args.append(w)
    if glu:
        in_specs.append(pl.BlockSpec(lead + (k, tn), lambda j, i: widx + (0, j + nb), pipeline_mode=pl.Buffered(1)))
        args.append(w)
    for r in rows:
        in_specs.append(pl.BlockSpec((1, tn), lambda j, i: (0, j)))
        args.append(r.reshape(1, n))
    for f in fulls:
        in_specs.append(pl.BlockSpec((tm, tn), lambda j, i: (i, j)))
        args.append(f)
    scratch = [pltpu.VMEM((k, tn), BF16) for _ in range(n_w * (2 if passes == 3 else 1))]
    kern = functools.partial(_linear_kernel, passes=passes, has_mix=mix is not None, glu=glu,
                             n_row=len(rows), n_full=len(fulls), post=post)
    return pl.pallas_call(
        kern,
        grid=grid,
        in_specs=in_specs,
        out_specs=pl.BlockSpec((tm, tn), lambda j, i: (i, j)),
        out_shape=jax.ShapeDtypeStruct((m, n), F32),
        scratch_shapes=scratch,
        compiler_params=_cparams(2),
    )(*args)


def _ln_kernel(*refs, alpha, has_h, with_bf16):
    if has_h:
        x_ref, h_ref, g_ref, b_ref = refs[:4]
        x = alpha * x_ref[...] + h_ref[...]
    else:
        x_ref, g_ref, b_ref = refs[:3]
        x = x_ref[...]
    mu = jnp.mean(x, -1, keepdims=True)
    xc = x - mu
    var = jnp.mean(xc * xc, -1, keepdims=True)
    y = xc * lax.rsqrt(var + LN_EPS) * g_ref[...] + b_ref[...]
    if with_bf16:
        refs[-2][...] = y
        refs[-1][...] = y.astype(BF16)
    else:
        refs[-1][...] = y


def layer_norm(x, g, b, h=None, alpha=1.0, with_bf16=False):
    m, d = x.shape
    tm = _pick_tile(m, (384, 256, 128, 64, 32, 16, 8))
    spec = pl.BlockSpec((tm, d), lambda i: (i, 0))
    vspec = pl.BlockSpec((1, d), lambda i: (0, 0))
    args = [x] + ([h] if h is not None else []) + [g.reshape(1, d), b.reshape(1, d)]
    in_specs = [spec] + ([spec] if h is not None else []) + [vspec, vspec]
    return pl.pallas_call(
        functools.partial(_ln_kernel, alpha=alpha, has_h=h is not None, with_bf16=with_bf16),
        grid=(m // tm,),
        in_specs=in_specs,
        out_specs=[spec, spec] if with_bf16 else spec,
        out_shape=([jax.ShapeDtypeStruct((m, d), F32), jax.ShapeDtypeStruct((m, d), BF16)] if with_bf16
                   else jax.ShapeDtypeStruct((m, d), F32)),
        compiler_params=_cparams(1),
    )(*args)


def _bf16r(x):
    return x.astype(BF16).astype(F32)


def _matvec_bf16(s, vec_row):
    return jnp.sum(_bf16r(s) * _bf16r(vec_row), axis=1, keepdims=True)


def _row_from_col(col, eye):
    return jnp.sum(eye * col, axis=0, keepdims=True)


def _col_from_row(row, eye):
    return jnp.sum(eye * row, axis=1, keepdims=True)


def _wkv_out(y, v_h, rk_h, lg, lb):
    mu = jnp.mean(y, -1, keepdims=True)
    yc = y - mu
    var = jnp.mean(yc * yc, -1, keepdims=True)
    return yc * lax.rsqrt(var + GN_EPS) * lg + lb + jnp.sum(rk_h, -1, keepdims=True) * v_h


def _wkv_seq_kernel(r_ref, lw_ref, k_ref, v_ref, a_ref, g_ref, kk_ref, ka_ref, rk_ref, lg_ref, lb_ref, s0_ref,
                    y_ref, s_ref, *, steps, heads):
    N = RW_HEAD
    r = r_ref[0]
    k = k_ref[0]
    v = v_ref[0]
    a = a_ref[0]
    w = jnp.exp(lw_ref[0])
    kkw = k * kk_ref[...]
    k2 = k * (1.0 + (a - 1.0) * ka_ref[...])
    rk = r * k2 * rk_ref[...]
    eye = (lax.broadcasted_iota(jnp.int32, (N, N), 0) == lax.broadcasted_iota(jnp.int32, (N, N), 1)).astype(F32)
    tpad = r.shape[0]
    outs = []
    for h in range(heads):
        sl = slice(h * N, (h + 1) * N)
        kk = kkw[:, sl]
        kk = kk * lax.rsqrt(jnp.maximum(jnp.sum(kk * kk, -1, keepdims=True), 1e-24))
        b_h = kk * a[:, sl]
        s = s0_ref[0, h]
        ys = []
        for t in range(steps):
            sa = _matvec_bf16(s, -kk[t:t + 1])
            s = s * w[t:t + 1, sl] + sa * b_h[t:t + 1] + _col_from_row(v[t:t + 1, sl], eye) * k2[t:t + 1, sl]
            ys.append(_row_from_col(_matvec_bf16(s, r[t:t + 1, sl]), eye))
        s_ref[0, h] = s
        y = jnp.concatenate(ys + [jnp.zeros((tpad - steps, N), F32)], axis=0)
        outs.append(_wkv_out(y, v[:, sl], rk[:, sl], lg_ref[:, sl], lb_ref[:, sl]))
    y_ref[0] = jnp.concatenate(outs, axis=1) * g_ref[0]


def wkv7_seq(r, lw, k, v, a, g, k_k, k_a, r_k, lnx_g, lnx_b, s0, *, steps):
    bsz, t, d = r.shape
    n = RW_HEAD
    hh = d // n
    heads = 4 if hh % 4 == 0 else 2
    wide = heads * n
    seq = pl.BlockSpec((1, t, wide), lambda b, hb: (b, 0, hb))
    vec = pl.BlockSpec((1, wide), lambda b, hb: (0, hb))
    st = pl.BlockSpec((1, heads, n, n), lambda b, hb: (b, hb, 0, 0))
    vecs = [z.reshape(1, d) for z in (k_k, k_a, r_k, lnx_g, lnx_b)]
    return pl.pallas_call(
        functools.partial(_wkv_seq_kernel, steps=steps, heads=heads),
        grid=(bsz, hh // heads),
        in_specs=[seq] * 6 + [vec] * 5 + [st],
        out_specs=[seq, st],
        out_shape=[jax.ShapeDtypeStruct((bsz, t, d), F32), jax.ShapeDtypeStruct((bsz, hh, n, n), F32)],
        compiler_params=_cparams(2),
    )(r, lw, k, v, a, g, *vecs, s0)


def _wkv_kernel(r_ref, lw_ref, k_ref, v_ref, a_ref, g_ref, kk_ref, ka_ref, rk_ref, lg_ref, lb_ref, s0_ref,
                y_ref, s_ref, *, chunk, heads, passes):
    L, N = chunk, RW_HEAD

    @pl.when(pl.program_id(2) == 0)
    def _():
        s_ref[...] = s0_ref[...]

    r = r_ref[...]
    lw = lw_ref[...]
    k = k_ref[...]
    v = v_ref[...]
    a = a_ref[...]
    kkw = k * kk_ref[...]
    k2 = k * (1.0 + (a - 1.0) * ka_ref[...])
    rk = r * k2 * rk_ref[...]

    row = lax.broadcasted_iota(jnp.int32, (L, L), 0)
    col = lax.broadcasted_iota(jnp.int32, (L, L), 1)
    tri_incl = (col <= row)
    tri_strict = (col < row)
    tri = tri_incl.astype(BF16)
    l1 = lw.astype(BF16)
    rem = lw - l1.astype(F32)
    l2 = rem.astype(BF16)
    l3 = (rem - l2.astype(F32)).astype(BF16)
    cs = (jnp.dot(tri, l1, preferred_element_type=F32) + jnp.dot(tri, l2, preferred_element_type=F32)
          + jnp.dot(tri, l3, preferred_element_type=F32))
    cs_last = cs[L - 1:L, :]
    e_prev = jnp.exp(cs - lw)
    e_inv = jnp.exp(-cs)
    e_fwd = jnp.exp(cs)
    e_tail = jnp.exp(cs_last - cs)
    gam = jnp.exp(cs_last)

    n_dbl = max(1, (L - 1).bit_length())
    eye = (lax.broadcasted_iota(jnp.int32, (N, N), 0) == lax.broadcasted_iota(jnp.int32, (N, N), 1)).astype(F32)
    H = range(heads)
    sls = [slice(h * N, (h + 1) * N) for h in H]
    kk = [kkw[:, sl] for sl in sls]
    kk = [z * lax.rsqrt(jnp.maximum(jnp.sum(z * z, -1, keepdims=True), 1e-24)) for z in kk]
    b_h = [kk[h] * a[:, sls[h]] for h in H]
    k_h = [k2[:, sl] for sl in sls]
    v_h = [v[:, sl] for sl in sls]
    lhs2 = [jnp.concatenate([-kk[h] * e_prev[:, sls[h]], r[:, sls[h]] * e_fwd[:, sls[h]]], axis=0) for h in H]
    rhs2 = [jnp.concatenate([b_h[h] * e_inv[:, sls[h]], k_h[h] * e_inv[:, sls[h]]], axis=0) for h in H]
    bk = [jnp.concatenate([b_h[h] * e_tail[:, sls[h]], k_h[h] * e_tail[:, sls[h]]], axis=0) for h in H]
    s0 = [s_ref[0, h] for h in H]
    p = [_dot(lhs2[h], rhs2[h], passes, _NT) for h in H]
    q = [_dot(lhs2[h], s0[h], passes, _NT) for h in H]
    pk = [jnp.where(tri_strict, p[h][:L, :L], 0.0) for h in H]
    a_k = [jnp.where(tri_strict, p[h][:L, L:], 0.0) for h in H]
    rbk = [jnp.concatenate([jnp.where(tri_incl, p[h][L:, :L], 0.0), jnp.where(tri_incl, p[h][L:, L:], 0.0)], axis=1)
           for h in H]
    x = [q[h][:L] + _dot(a_k[h], v_h[h], passes) for h in H]
    for d in range(n_dbl):
        x = [x[h] + _dot(pk[h], x[h], passes) for h in H]
        if d + 1 < n_dbl:
            pk = [_dot(pk[h], pk[h], passes) for h in H]
    uv = [jnp.concatenate([x[h], v_h[h]], axis=0) for h in H]
    y = [q[h][L:] + _dot(rbk[h], uv[h], passes) for h in H]
    s_new = [s0[h] * gam[:, sls[h]] + _dot(uv[h], bk[h], passes, _TN) for h in H]
    for h in H:
        s_ref[0, h] = s_new[h]
    last = lax.broadcasted_iota(jnp.int32, (L, 1), 0) == L - 1
    y = [jnp.where(last, _row_from_col(_matvec_bf16(s_new[h], r[L - 1:L, sls[h]]), eye), y[h]) for h in H]
    outs = [_wkv_out(y[h], v_h[h], rk[:, sls[h]], lg_ref[:, sls[h]], lb_ref[:, sls[h]]) for h in H]
    y_ref[...] = jnp.concatenate(outs, axis=1) * g_ref[...]


def wkv7(r, lw, k, v, a, g, k_k, k_a, r_k, lnx_g, lnx_b, s0, *, seq_len, passes):
    bsz, t = s0.shape[0], seq_len
    d = r.shape[1]
    n = RW_HEAD
    hh = d // n
    heads = _pick_tile(hh, (16, 4, 2))
    chunk = _pick_tile(t, (128, 64))
    wide = heads * n
    n_chunks = t // chunk
    seq = pl.BlockSpec((chunk, wide), lambda b, hb, c: (b * n_chunks + c, hb))
    vec = pl.BlockSpec((1, wide), lambda b, hb, c: (0, hb))
    st = pl.BlockSpec((1, heads, n, n), lambda b, hb, c: (b, hb, 0, 0))
    vecs = [z.reshape(1, d) for z in (k_k, k_a, r_k, lnx_g, lnx_b)]
    return pl.pallas_call(
        functools.partial(_wkv_kernel, chunk=chunk, heads=heads, passes=passes),
        grid=(bsz, hh // heads, n_chunks),
        in_specs=[seq] * 6 + [vec] * 5 + [st],
        out_specs=[seq, st],
        out_shape=[jax.ShapeDtypeStruct((bsz * t, d), F32), jax.ShapeDtypeStruct((bsz, hh, n, n), F32)],
        compiler_params=_cparams(3),
    )(r, lw, k, v, a, g, *vecs, s0)


def _route_kernel(x_ref, rt_ref, bias_ref, eidx_ref, w_ref, pos_ref, cnt_ref, *, n_valid):
    ne = rt_ref.shape[0]
    per = ne // N_GROUPS
    logits = _dot(rt_ref[...], x_ref[...], 1, _NT)
    s = jax.nn.sigmoid(logits)
    sb = s + bias_ref[...]
    tm = sb.shape[1]
    sub = lax.broadcasted_iota(jnp.int32, (per, tm), 0)
    gs_rows = []
    for gi in range(N_GROUPS):
        tile = sb[gi * per:(gi + 1) * per, :]
        m1 = jnp.max(tile, axis=0, keepdims=True)
        first = jnp.min(jnp.where(tile == m1, sub, per), axis=0, keepdims=True)
        m2 = jnp.max(jnp.where(sub == first, -jnp.inf, tile), axis=0, keepdims=True)
        gs_rows.append(m1 + m2)
    keep_rows = []
    for gi in range(N_GROUPS):
        cnt = jnp.zeros((1, tm), jnp.int32)
        for gj in range(N_GROUPS):
            if gj == gi:
                continue
            beats = (gs_rows[gj] > gs_rows[gi]) if gj > gi else (gs_rows[gj] >= gs_rows[gi])
            cnt = cnt + beats.astype(jnp.int32)
        keep_rows.append(jnp.broadcast_to(cnt, (per, tm)))
    masked = jnp.where(jnp.concatenate(keep_rows, axis=0) < TOPK_GROUPS, sb, -1e9)
    eidx = lax.broadcasted_iota(jnp.int32, (ne, tm), 0)
    rank = jnp.zeros((ne, tm), jnp.int32)
    for ej in range(ne):
        rowv = masked[ej:ej + 1, :]
        rank = rank + jnp.where(eidx > ej, (rowv >= masked).astype(jnp.int32), (rowv > masked).astype(jnp.int32))
    tok = pl.program_id(0) * tm + lax.broadcasted_iota(jnp.int32, (ne, tm), 1)
    rank = jnp.where(tok < n_valid, rank, ne)
    sel = rank < TOP_K
    self32 = sel.astype(F32)
    den = jnp.sum(jnp.where(sel, s, 0.0), axis=0, keepdims=True)
    wgt = jnp.where(sel, s / den * ROUTED_SCALE, 0.0)

    @pl.when(pl.program_id(0) == 0)
    def _():
        cnt_ref[...] = jnp.zeros(cnt_ref.shape, F32)

    upper = (lax.broadcasted_iota(jnp.int32, (tm, tm), 0) < lax.broadcasted_iota(jnp.int32, (tm, tm), 1)).astype(BF16)
    carry = cnt_ref[...]
    pos = jnp.dot(self32.astype(BF16), upper, preferred_element_type=F32) + carry[:, :1]
    cnt_ref[...] = carry + jnp.sum(self32, axis=1, keepdims=True)
    eidx_f = eidx.astype(F32)
    rows_e, rows_w, rows_p = [], [], []
    for kk in range(TOP_K):
        hit = rank == kk
        rows_e.append(jnp.sum(jnp.where(hit, eidx_f, 0.0), axis=0, keepdims=True))
        rows_w.append(jnp.sum(jnp.where(hit, wgt, 0.0), axis=0, keepdims=True))
        rows_p.append(jnp.sum(jnp.where(hit, pos, 0.0), axis=0, keepdims=True))
    eidx_ref[...] = jnp.concatenate(rows_e, axis=0).astype(jnp.int32)
    w_ref[...] = jnp.concatenate(rows_w, axis=0)
    pos_ref[...] = jnp.concatenate(rows_p, axis=0).astype(jnp.int32)


def route(x, router_t, bias, n_valid):
    m, d = x.shape
    ne = router_t.shape[0]
    tm = _pick_tile(m, (256, 128))
    out = pl.BlockSpec((TOP_K, tm), lambda i: (0, i))
    return pl.pallas_call(
        functools.partial(_route_kernel, n_valid=n_valid),
        grid=(m // tm,),
        in_specs=[pl.BlockSpec((tm, d), lambda i: (i, 0)), pl.BlockSpec((ne, d), lambda i: (0, 0)),
                  pl.BlockSpec((ne, 1), lambda i: (0, 0))],
        out_specs=[out, out, out, pl.BlockSpec((ne, LANES), lambda i: (0, 0))],
        out_shape=[jax.ShapeDtypeStruct((TOP_K, m), jnp.int32), jax.ShapeDtypeStruct((TOP_K, m), F32),
                   jax.ShapeDtypeStruct((TOP_K, m), jnp.int32), jax.ShapeDtypeStruct((ne, LANES), F32)],
        compiler_params=_cparams(1),
    )(x, router_t, bias.reshape(ne, 1))


def _experts_kernel(be_ref, nu_ref, xs_ref, win_ref, wout_ref, o_ref, win_s, wout_s):
    i = pl.program_id(0)
    f = wout_ref.shape[0]

    @pl.when(i < nu_ref[0])
    def _():
        changed = jnp.logical_or(i == 0, be_ref[i] != be_ref[jnp.maximum(i - 1, 0)])

        @pl.when(changed)
        def _():
            win_s[...] = win_ref[...].astype(BF16)
            wout_s[...] = wout_ref[...].astype(BF16)

        hu = jnp.dot(xs_ref[...], win_s[...], preferred_element_type=F32)
        hid = jax.nn.silu(hu[:, :f]) * hu[:, f:]
        y = jnp.dot(hid.astype(BF16), wout_s[...], preferred_element_type=F32)
        o_ref[...] = y.astype(BF16)

    @pl.when(i >= nu_ref[0])
    def _():
        o_ref[...] = jnp.zeros(o_ref.shape, BF16)


def experts(xs, blk_expert, n_used, w_in, w_out, layer, tm):
    rws, d = xs.shape
    f2 = w_in.shape[3]
    f = f2 // 2
    grid_spec = pltpu.PrefetchScalarGridSpec(
        num_scalar_prefetch=2,
        grid=(rws // tm,),
        in_specs=[
            pl.BlockSpec((tm, d), lambda i, be, nu: (jnp.minimum(i, nu[0] - 1), 0)),
            pl.BlockSpec((None, None, d, f2), lambda i, be, nu: (layer, be[i], 0, 0)),
            pl.BlockSpec((None, None, f, d), lambda i, be, nu: (layer, be[i], 0, 0)),
        ],
        out_specs=pl.BlockSpec((tm, d), lambda i, be, nu: (i, 0)),
        scratch_shapes=[pltpu.VMEM((d, f2), BF16), pltpu.VMEM((f, d), BF16)],
    )
    return pl.pallas_call(
        _experts_kernel,
        grid_spec=grid_spec,
        out_shape=jax.ShapeDtypeStruct((rws, d), BF16),
        compiler_params=_cparams(1),
    )(blk_expert, n_used, xs, w_in, w_out)


def moe_ffn(x, x_bf16, n_valid, layer, router, bias, w_in, w_out, ws_in, ws_out):
    m, d = x.shape
    ne = router.shape[2]
    eidx, ew, posk, cnt = route(x, router[layer].T, bias[layer], n_valid)
    tm = 256 if m * TOP_K // ne >= 256 else _round_up(max(m * TOP_K // ne, SUBLANES), SUBLANES)
    nk = m * TOP_K
    counts = cnt[:, 0].astype(jnp.int32)
    padded = (counts + tm - 1) // tm * tm
    pad_end = jnp.cumsum(padded)
    pad_start = pad_end - padded
    n_blocks = (nk + tm - 1) // tm + ne
    n_rows = n_blocks * tm
    tok = jnp.arange(m, dtype=jnp.int32)
    first_row = jnp.sum(jnp.where(eidx[:, :, None] == jnp.arange(ne, dtype=jnp.int32), pad_start, 0), axis=-1)
    dest = jnp.where(tok[None, :] < n_valid, first_row + posk, n_rows - 1).astype(jnp.int32)
    token_of_row = jnp.zeros((n_rows,), jnp.int32).at[dest.reshape(-1)].set(jnp.tile(tok, TOP_K))
    blk_start = jnp.arange(n_blocks, dtype=jnp.int32) * tm
    blk_expert = jnp.minimum(jnp.sum((pad_end[None, :] <= blk_start[:, None]).astype(jnp.int32), axis=1), ne - 1)
    n_used = (pad_end[-1:] // tm).astype(jnp.int32)
    xs = x_bf16.at[token_of_row].get(mode="promise_in_bounds")
    yb = experts(xs, blk_expert, n_used, w_in, w_out, layer, tm)
    wr = ew.astype(BF16).astype(F32)
    picked = yb.at[dest.T].get(mode="promise_in_bounds").astype(F32)
    routed = jnp.sum(picked * wr.T[:, :, None], axis=1)
    hid = linear(x, ws_in, widx=(layer,), glu=True, post=lambda a, b: jax.nn.silu(a) * b)
    return linear(hid, ws_out, widx=(layer,), fulls=(routed,), post=lambda acc, rt: acc + rt)


def _gelu_tanh(x):
    return 0.5 * x * (1.0 + jnp.tanh(0.7978845608028654 * (x + 0.044715 * x * x * x)))


def _group_tile(ref, g):
    if len(ref.shape) == 3:
        return ref[:, g, :]
    return ref[:, g * HEAD_DIM:(g + 1) * HEAD_DIM]


def _page_spec(pages, index_fn, slot_fn):
    if pages.ndim == 5:
        return pl.BlockSpec((None, PAGE, None, N_KV, HEAD_DIM), lambda *a: (index_fn(*a), 0, slot_fn(*a), 0, 0))
    return pl.BlockSpec((None, PAGE, N_KV * HEAD_DIM), lambda *a: (index_fn(*a), 0, slot_fn(*a)))


def _compress_kernel(tbl_ref, *refs, group, ppt):
    pg_refs = refs[:ppt]
    pos_ref, w1_ref, w2_ref, o_ref, w1_s = refs[ppt:ppt + 5]
    slabs = refs[ppt + 5:]
    p = pl.program_id(2)
    pin = p % (group // ppt)

    @pl.when(p == 0)
    def _():
        w1_s[...] = w1_ref[0].astype(BF16)

    for pi, pg_ref in enumerate(pg_refs):
        for g in range(N_KV):
            slabs[g][pl.ds(pl.multiple_of((pin * ppt + pi) * PAGE, PAGE), PAGE), :] = _group_tile(pg_ref, g)

    @pl.when(pin == group // ppt - 1)
    def _():
        per_page = PAGE // CMP_BLOCK
        nrow = group * per_page
        hid = w1_ref.shape[2]
        acc = jnp.zeros((N_KV * nrow, hid), F32)
        for t in range(CMP_BLOCK):
            lhs = jnp.concatenate([slabs[g][pl.ds(t, nrow, stride=CMP_BLOCK), :] for g in range(N_KV)], axis=0)
            lhs = (lhs + pos_ref[0, t:t + 1, :]).astype(BF16)
            acc = acc + jnp.dot(lhs, w1_s[t * HEAD_DIM:(t + 1) * HEAD_DIM, :], preferred_element_type=F32)
        out = _dot(_gelu_tanh(acc), w2_ref[0])
        for g in range(N_KV):
            o_ref[0, 0, :, g * HEAD_DIM:(g + 1) * HEAD_DIM] = out[g * nrow:(g + 1) * nrow]


def compress(pages, table, slot0, cmp_pos, cmp_w1, cmp_w2):
    bsz, n_pg = table.shape
    group = min(32, n_pg)
    ppt = _pick_tile(group, (4, 2, 1))
    gw = N_KV * HEAD_DIM
    per_page = PAGE // CMP_BLOCK
    hid = cmp_w1.shape[2]

    def page_of(pi):
        return lambda b, s, p, tbl: tbl[b * n_pg + p * ppt + pi]

    grid_spec = pltpu.PrefetchScalarGridSpec(
        num_scalar_prefetch=1,
        grid=(bsz, 2, n_pg // ppt),
        in_specs=[_page_spec(pages, page_of(pi), lambda b, s, p, tbl: slot0 + s) for pi in range(ppt)] + [
            pl.BlockSpec((1, CMP_BLOCK, HEAD_DIM), lambda b, s, p, tbl: (s, 0, 0)),
            pl.BlockSpec((1, CMP_BLOCK * HEAD_DIM, hid), lambda b, s, p, tbl: (s, 0, 0)),
            pl.BlockSpec((1, hid, HEAD_DIM), lambda b, s, p, tbl: (s, 0, 0)),
        ],
        out_specs=pl.BlockSpec((1, 1, group * per_page, gw), lambda b, s, p, tbl: (b, s, p // (group // ppt), 0)),
        scratch_shapes=[pltpu.VMEM((CMP_BLOCK * HEAD_DIM, hid), BF16)]
        + [pltpu.VMEM((group * PAGE, HEAD_DIM), F32) for _ in range(N_KV)],
    )
    return pl.pallas_call(
        functools.partial(_compress_kernel, group=group, ppt=ppt),
        grid_spec=grid_spec,
        out_shape=jax.ShapeDtypeStruct((bsz, 2, n_pg * per_page, gw), F32),
        compiler_params=_cparams(3),
    )(table.reshape(-1), *([pages] * ppt), cmp_pos, cmp_w1, cmp_w2)


def _nsa_cmp_kernel(q_ref, kc_ref, vc_ref, gate_ref, o_ref, sel_ref, score_s, rank_s, *, q_offset, n_sel_blocks,
                    tq, half):
    qt = pl.program_id(1)
    ncp = kc_ref.shape[1]
    nsp = sel_ref.shape[2]
    scale = HEAD_DIM ** -0.5
    rpg = N_KV
    n_heads = q_ref.shape[2] // HEAD_DIM
    rpg = n_heads // N_KV
    qpos = q_offset + qt * tq + lax.broadcasted_iota(jnp.int32, (1, tq), 1)
    cidx = lax.broadcasted_iota(jnp.int32, (ncp, 1), 0)
    corig = jnp.where(cidx < half, 2 * cidx, 2 * (cidx - half) + 1)
    cmask = ((corig + 1) * CMP_BLOCK - 1) <= qpos
    blk = lax.broadcasted_iota(jnp.int32, (nsp, 1), 0)
    cur = qpos // SEL_BLOCK
    forced = (blk == 0) | (blk == cur) | (blk == cur - 1)
    valid = (blk * SEL_BLOCK) <= qpos
    gate = gate_ref[0]
    for g in range(N_KV):
        kc = kc_ref[0, :, g * HEAD_DIM:(g + 1) * HEAD_DIM]
        vc = vc_ref[0, :, g * HEAD_DIM:(g + 1) * HEAD_DIM]
        imp = jnp.zeros((ncp, tq), F32)
        for h in range(rpg):
            hd = g * rpg + h
            qh = q_ref[0, :, hd * HEAD_DIM:(hd + 1) * HEAD_DIM]
            s = _dot(kc, qh, 1, _NT) * scale
            s = jnp.where(cmask, s, NEG)
            mx = jnp.max(s, axis=0, keepdims=True)
            e = jnp.where(cmask, jnp.exp(s - mx), 0.0)
            p = e / jnp.maximum(jnp.sum(e, axis=0, keepdims=True), 1e-30)
            imp = imp + p
            oc = _dot(p, vc, 1, _TN)
            o_ref[0, :, hd * HEAD_DIM:(hd + 1) * HEAD_DIM] = gate[:, hd * 3:hd * 3 + 1] * oc
        imp_blk = imp[:half] + imp[half:]
        if nsp > half:
            imp_blk = jnp.concatenate([imp_blk, jnp.zeros((nsp - half, tq), F32)], axis=0)
        else:
            imp_blk = imp_blk[:nsp]
        score = jnp.where(valid, imp_blk + FORCED_BONUS * forced.astype(F32), -1.0)
        score = jnp.where(blk < n_sel_blocks, score, -2.0)
        score_s[...] = score
        rank_s[...] = jnp.zeros((nsp, tq), jnp.int32)

        def body(j, carry):
            rowv = score_s[pl.ds(j, 1), :]
            rank_s[...] = rank_s[...] + jnp.where(blk > j, (rowv >= score).astype(jnp.int32),
                                                  (rowv > score).astype(jnp.int32))
            return carry

        lax.fori_loop(0, n_sel_blocks, body, 0)
        sel_ref[0, g] = (rank_s[...] < N_SEL).astype(F32)


def nsa_cmp(q, kc, vc, gates, *, q_offset, n_sel_blocks, tq):
    bsz, t, dq = q.shape
    ncp = kc.shape[1]
    half = ncp // 2
    nsp = _round_up(n_sel_blocks, SUBLANES)
    gw = N_KV * HEAD_DIM
    qspec = pl.BlockSpec((1, tq, dq), lambda b, i: (b, i, 0))
    cspec = pl.BlockSpec((1, ncp, gw), lambda b, i: (b, 0, 0))
    return pl.pallas_call(
        functools.partial(_nsa_cmp_kernel, q_offset=q_offset, n_sel_blocks=n_sel_blocks, tq=tq, half=half),
        grid=(bsz, t // tq),
        in_specs=[qspec, cspec, cspec, pl.BlockSpec((1, tq, gates.shape[2]), lambda b, i: (b, i, 0))],
        out_specs=[qspec, pl.BlockSpec((1, N_KV, nsp, tq), lambda b, i: (b, 0, 0, i))],
        out_shape=[jax.ShapeDtypeStruct((bsz, t, dq), F32), jax.ShapeDtypeStruct((bsz, N_KV, nsp, t), F32)],
        scratch_shapes=[pltpu.VMEM((nsp, tq), F32), pltpu.VMEM((nsp, tq), jnp.int32)],
        compiler_params=_cparams(2),
    )(q, kc, vc, gates)


def _nsa_attn_kernel(*refs, mode, q_offset, tq, n_kt, ppt, has_tail, tail_pos0, branch):
    tbl_ref, kp0_ref = refs[0], refs[1]
    it = iter(refs[2:])
    q_ref = next(it)
    k_refs = [next(it) for _ in range(ppt)]
    v_refs = [next(it) for _ in range(ppt)]
    tk_ref = next(it) if has_tail else None
    tv_ref = next(it) if has_tail else None
    sel_ref = next(it) if mode == "sel" else None
    gate_ref = next(it)
    prev_ref = next(it)
    o_ref = next(it)
    m_s, l_s, acc_s = next(it), next(it), next(it)

    qt = pl.program_id(1)
    j = pl.program_id(2)
    n_heads = q_ref.shape[2] // HEAD_DIM
    rpg = n_heads // N_KV
    scale = HEAD_DIM ** -0.5
    width = WINDOW if mode == "win" else (1 << 30)
    n_steps = n_kt // ppt
    G = range(N_KV)

    @pl.when(j == 0)
    def _():
        m_s[...] = jnp.full(m_s.shape, NEG, F32)
        l_s[...] = jnp.zeros(l_s.shape, F32)
        acc_s[...] = jnp.zeros(acc_s.shape, F32)

    q_first = q_offset + qt * tq
    qpos = q_first + lax.broadcasted_iota(jnp.int32, (tq, 1), 0)

    def tile(kt_refs, vt_refs, kp0s):
        lane = lax.broadcasted_iota(jnp.int32, (1, PAGE), 1)
        kpos = jnp.concatenate([jnp.where(kp >= 0, kp, 1 << 30) + lane for kp in kp0s], axis=1)
        dist = qpos - kpos
        band = (dist >= 0) & (dist < width)
        if mode == "sel":
            nsp = sel_ref.shape[2]
            blk = lax.broadcasted_iota(jnp.int32, (nsp, 1), 0)
            expand = (blk == kpos // SEL_BLOCK).astype(BF16)
            hits = [_dg(sel_ref[0, g].astype(BF16), expand, _TN) for g in G]
            masks = [band & (hits[g] > 0.5) for g in G]
        else:
            masks = [band for g in G]
        masks = [jnp.concatenate([mk] * rpg, axis=0) for mk in masks]
        qg = [jnp.concatenate([q_ref[0, :, (g * rpg + h) * HEAD_DIM:(g * rpg + h + 1) * HEAD_DIM] for h in range(rpg)],
                              axis=0) for g in G]
        kg = [jnp.concatenate([_group_tile(r, g) for r in kt_refs], axis=0) for g in G]
        vg = [jnp.concatenate([_group_tile(r, g) for r in vt_refs], axis=0) for g in G]
        s = [_dot(qg[g], kg[g], 1, _NT) * scale for g in G]
        s = [jnp.where(masks[g], s[g], NEG) for g in G]
        m_old = [m_s[g] for g in G]
        m_new = [jnp.maximum(m_old[g], jnp.max(s[g], -1, keepdims=True)) for g in G]
        alpha = [jnp.exp(m_old[g] - m_new[g]) for g in G]
        p = [jnp.where(masks[g], jnp.exp(s[g] - m_new[g]), 0.0) for g in G]
        pv = [_dot(p[g], vg[g], 1) for g in G]
        for g in G:
            l_s[g] = alpha[g] * l_s[g] + jnp.sum(p[g], -1, keepdims=True)
            acc_s[g] = alpha[g] * acc_s[g] + pv[g]
            m_s[g] = m_new[g]

    base = (pl.program_id(0) * pl.num_programs(1) + qt) * n_kt + jnp.minimum(j, n_steps - 1) * ppt
    kp0s = [kp0_ref[base + pi] for pi in range(ppt)]
    live = None
    for kp in kp0s:
        ok = (kp >= 0) & (kp <= q_first + tq - 1) & (kp + PAGE - 1 > q_first - width)
        live = ok if live is None else (live | ok)
    if has_tail:
        live = live & (j < n_steps)

    @pl.when(live)
    def _():
        tile(k_refs, v_refs, kp0s)

    if has_tail:
        @pl.when(j == n_steps)
        def _():
            tile([tk_ref], [tv_ref], [tail_pos0])

    @pl.when(j == pl.num_programs(2) - 1)
    def _():
        gate = gate_ref[0]
        for g in range(N_KV):
            o = acc_s[g] / jnp.maximum(l_s[g], 1e-30)
            for h in range(rpg):
                hd = g * rpg + h
                cs = slice(hd * HEAD_DIM, (hd + 1) * HEAD_DIM)
                o_ref[0, :, cs] = prev_ref[0, :, cs] + gate[:, hd * 3 + branch:hd * 3 + branch + 1] * o[h * tq:(h + 1) * tq]


def nsa_attn(q, pages, tbl, kp0, slot_k, slot_v, tail, sel, gates, prev, *, mode, q_offset, tq, tail_pos0, branch,
             ppt=1):
    bsz, t, dq = q.shape
    n_qt = t // tq
    n_kt = tbl.shape[2]
    assert n_kt % ppt == 0
    gw = N_KV * HEAD_DIM
    n_heads = dq // HEAD_DIM
    rows = (n_heads // N_KV) * tq
    has_tail = tail is not None
    n_steps = n_kt // ppt + (1 if has_tail else 0)

    def page_idx(pi):
        return lambda b, i, j, tb, kp: tb[(b * n_qt + i) * n_kt + jnp.minimum(j, n_kt // ppt - 1) * ppt + pi]

    qspec = pl.BlockSpec((1, tq, dq), lambda b, i, j, tb, kp: (b, i, 0))
    in_specs = [qspec]
    in_specs += [_page_spec(pages, page_idx(pi), lambda *a: slot_k) for pi in range(ppt)]
    in_specs += [_page_spec(pages, page_idx(pi), lambda *a: slot_v) for pi in range(ppt)]
    args = [q] + [pages] * (2 * ppt)
    if has_tail:
        in_specs += [pl.BlockSpec((None, PAGE, gw), lambda b, i, j, tb, kp: (b, 0, 0)),
                     pl.BlockSpec((None, PAGE, gw), lambda b, i, j, tb, kp: (b, 0, 1))]
        args += [tail, tail]
    if mode == "sel":
        nsp = sel.shape[2]
        in_specs.append(pl.BlockSpec((1, N_KV, nsp, tq), lambda b, i, j, tb, kp: (b, 0, 0, i)))
        args.append(sel)
    in_specs += [pl.BlockSpec((1, tq, gates.shape[2]), lambda b, i, j, tb, kp: (b, i, 0)), qspec]
    args += [gates, prev]
    grid_spec = pltpu.PrefetchScalarGridSpec(
        num_scalar_prefetch=2,
        grid=(bsz, n_qt, n_steps),
        in_specs=in_specs,
        out_specs=qspec,
        scratch_shapes=[pltpu.VMEM((N_KV, rows, 1), F32), pltpu.VMEM((N_KV, rows, 1), F32),
                        pltpu.VMEM((N_KV, rows, HEAD_DIM), F32)],
    )
    return pl.pallas_call(
        functools.partial(_nsa_attn_kernel, mode=mode, q_offset=q_offset, tq=tq, n_kt=n_kt, ppt=ppt,
                          has_tail=has_tail, tail_pos0=tail_pos0, branch=branch),
        grid_spec=grid_spec,
        out_shape=jax.ShapeDtypeStruct((bsz, t, dq), F32),
        compiler_params=_cparams(3),
    )(tbl.reshape(-1).astype(jnp.int32), kp0.reshape(-1).astype(jnp.int32), *args)


def _softplus(z):
    return jnp.maximum(z, 0.0) + jnp.log(1.0 + jnp.exp(-jnp.abs(z)))


def _post_logdecay(acc, w0):
    return -jnp.exp(-_softplus(-(w0 + acc)) - 0.5)


def _post_sig_bias(acc, b):
    return jax.nn.sigmoid(acc + b)


def _post_vres(acc, v0, v, vf):
    return v + (vf - v) * jax.nn.sigmoid(v0 + acc)


def _post_ple(acc, b, x, t):
    return x + jax.nn.sigmoid(acc + b) * t


def _pad_rows(x, m):
    return jnp.pad(x, ((0, m - x.shape[0]),) + ((0, 0),) * (x.ndim - 1))


def kernel(x_prompt, x_sample, cache_kv_pages, cache_win_kv, state_wkv, state_shift, page_table, p_prompt, p_sample, ln_g, ln_b, rw_mix, rw_w_rkv, rw_w0, rw_w1, rw_w2, rw_a0, rw_a1, rw_a2, rw_v0, rw_v1, rw_v2, rw_g1, rw_g2, rw_k_k, rw_k_a, rw_r_k, rw_lnx_g, rw_lnx_b, rw_w_o, kv_ln_g, kv_ln_b, w_kv, cmp_pos, cmp_w1, cmp_w2, nsa_w_qg, nsa_b_g, nsa_w_o, moe_router, moe_bias, moe_w_in, moe_w_out, moe_ws_in, moe_ws_out, ple_proj, ple_gate, ple_gate_b):
    bp, tp, d = x_prompt.shape
    bs, ts, _ = x_sample.shape
    depth = ln_g.shape[0]
    n_a = state_wkv.shape[0]
    alpha = (2 * depth) ** 0.25
    n_p, n_s = bp * tp, bs * ts
    n_tok = n_p + n_s
    m = _round_up(n_tok, 768) if n_tok > 768 else _round_up(n_tok, SUBLANES)
    ts_pad = _round_up(ts, SUBLANES)
    hh = d // RW_HEAD
    gw = N_KV * HEAD_DIM
    n_pg_s = page_table.shape[1]
    past_len = n_pg_s * cache_kv_pages.shape[1]
    win_buf = cache_win_kv.shape[1]
    assert cache_kv_pages.shape[1] == PAGE and tp % PAGE == 0 and win_buf % PAGE == 0 and ts < CMP_BLOCK

    def join(a_p, a_s):
        return _pad_rows(jnp.concatenate([a_p.reshape(n_p, -1), a_s.reshape(n_s, -1)], axis=0), m)

    def split_seq(z):
        zp = z[:n_p].reshape(bp, tp, -1)
        zs = jnp.pad(z[n_p:n_tok].reshape(bs, ts, -1), ((0, 0), (0, ts_pad - ts), (0, 0)))
        return zp, zs

    def merge_seq(zp, zs):
        return join(zp, zs[:, :ts])

    x = join(x_prompt, x_sample)
    pe = jnp.concatenate([p_prompt.reshape(depth, n_p, -1), p_sample.reshape(depth, n_s, -1)], axis=1)
    pe = jnp.pad(pe, ((0, 0), (0, m - n_tok), (0, 0)))

    shifts_p, shifts_s, wkvs_p, wkvs_s = [], [], [], []
    v_first = None
    ctx = None
    kv_rows_p = kv_rows_s = win_p = win_s = None
    for i in range(depth):
        if i < n_a:
            scan_passes = 3 if i == 0 else 1
            xp_, xs_ = x[:n_p].reshape(bp, tp, d), x[n_p:n_tok].reshape(bs, ts, d)
            shifts_p.append(xp_[:, -1])
            shifts_s.append(xs_[:, -1])
            prev = join(jnp.concatenate([jnp.zeros((bp, 1, d), F32), xp_[:, :-1]], axis=1),
                        jnp.concatenate([state_shift[i][:, None, :], xs_[:, :-1]], axis=1))
            mix = rw_mix[i]
            lin = linear
            r = lin(x, rw_w_rkv, widx=(i, 0), mix=(prev, mix[0]))
            k = lin(x, rw_w_rkv, widx=(i, 1), mix=(prev, mix[2]))
            v = lin(x, rw_w_rkv, widx=(i, 2), mix=(prev, mix[3]))
            lw = lin(lin(x, rw_w1, widx=(i,), mix=(prev, mix[1]), post=jnp.tanh), rw_w2, widx=(i,), rows=(rw_w0[i],),
                     post=_post_logdecay)
            if i == 0:
                v_first = v
            else:
                v = lin(lin(x, rw_v1, widx=(i - 1,), mix=(prev, mix[3])), rw_v2, widx=(i - 1,), rows=(rw_v0[i - 1],),
                        fulls=(v, v_first), post=_post_vres)
            a = lin(lin(x, rw_a1, widx=(i,), mix=(prev, mix[4])), rw_a2, widx=(i,), rows=(rw_a0[i],),
                    post=_post_sig_bias)
            g = lin(lin(x, rw_g1, widx=(i,), mix=(prev, mix[5]), post=jax.nn.sigmoid), rw_g2, widx=(i,))
            proj = (r, lw, k, v, a, g)
            vecs = (rw_k_k[i], rw_k_a[i], rw_r_k[i].reshape(-1), rw_lnx_g[i], rw_lnx_b[i])
            y_p, s_p = wkv7(*proj, *vecs, jnp.zeros((bp, hh, RW_HEAD, RW_HEAD), F32), seq_len=tp,
                            passes=scan_passes)
            y_s, s_s = wkv7_seq(*[split_seq(z)[1] for z in proj], *vecs, state_wkv[i], steps=ts)
            wkvs_p.append(s_p)
            wkvs_s.append(s_s)
            h = lin(merge_seq(y_p, y_s), rw_w_o, widx=(i,))
        else:
            jn = i - n_a
            if ctx is None:
                rows = linear(layer_norm(x, kv_ln_g, kv_ln_b), w_kv)
                rows_p = rows[:n_p].reshape(bp, tp, 6 * gw)
                rows_s = rows[n_p:n_tok].reshape(bs, ts, 6 * gw)
                kv_rows_p = rows_p[:, :, :4 * gw].reshape(bp, tp, 4, N_KV, HEAD_DIM)
                kv_rows_s = rows_s[:, :, :4 * gw].reshape(bs, ts, 4, N_KV, HEAD_DIM)
                win_p = rows_p[:, max(tp - WINDOW, 0):, 4 * gw:].reshape(bp, -1, 2, N_KV, HEAD_DIM)
                new_win = rows_s[:, :, 4 * gw:].reshape(bs, ts, 2, N_KV, HEAD_DIM)
                win_s = jnp.concatenate([cache_win_kv, new_win], axis=1)[:, ts:]
                n_pg_p = tp // PAGE
                pages_p = rows[:n_p].reshape(bp * n_pg_p, PAGE, 6 * gw)
                table_p = jnp.arange(bp * n_pg_p, dtype=jnp.int32).reshape(bp, n_pg_p)
                pages_s = cache_kv_pages
                pages_w = cache_win_kv.reshape(bs * (win_buf // PAGE), PAGE, 2, N_KV, HEAD_DIM)

                def perm(c):
                    return jnp.concatenate([c[:, 0::2], c[:, 1::2]], axis=1)

                cmp_p = compress(pages_p, table_p, 0, cmp_pos, cmp_w1, cmp_w2)
                cmp_s = compress(pages_s, page_table.astype(jnp.int32), 0, cmp_pos, cmp_w1, cmp_w2)
                tail_sel = jnp.pad(rows_s[:, :, 2 * gw:4 * gw], ((0, 0), (0, PAGE - ts), (0, 0)))
                tail_win = jnp.pad(rows_s[:, :, 4 * gw:6 * gw], ((0, 0), (0, PAGE - ts), (0, 0)))
                tq_p = PAGE
                n_qt = tp // tq_p
                qi = jnp.arange(n_qt, dtype=jnp.int32)[:, None]
                kj = jnp.arange(n_pg_p, dtype=jnp.int32)[None, :]
                base = (jnp.arange(bp, dtype=jnp.int32) * n_pg_p)[:, None, None]
                tbl_sel_p = base + jnp.minimum(kj, qi)[None]
                kp0_sel_p = jnp.broadcast_to(jnp.where(kj <= qi, kj * PAGE, -1)[None], tbl_sel_p.shape)
                n_wt = WINDOW // PAGE + 1
                kw = qi - (n_wt - 1) + jnp.arange(n_wt, dtype=jnp.int32)[None, :]
                tbl_win_p = base + jnp.maximum(kw, 0)[None]
                kp0_win_p = jnp.broadcast_to(jnp.where(kw >= 0, kw * PAGE, -1)[None], tbl_win_p.shape)
                tbl_sel_s = page_table.astype(jnp.int32)[:, None, :]
                kp0_sel_s = jnp.broadcast_to((jnp.arange(n_pg_s, dtype=jnp.int32) * PAGE)[None, None], tbl_sel_s.shape)
                n_ww = win_buf // PAGE
                tbl_win_s = jnp.arange(bs * n_ww, dtype=jnp.int32).reshape(bs, 1, n_ww)
                kp0_win_s = jnp.broadcast_to(
                    (past_len - win_buf + jnp.arange(n_ww, dtype=jnp.int32) * PAGE)[None, None], tbl_win_s.shape)
                ctx = dict(kc_p=perm(cmp_p[:, 0]), vc_p=perm(cmp_p[:, 1]), kc_s=perm(cmp_s[:, 0]), vc_s=perm(cmp_s[:, 1]))
            dq = nsa_w_qg.shape[2] - nsa_b_g.shape[1]
            q = linear(x, nsa_w_qg, widx=(jn,), ncols=dq)
            gates = linear(x, nsa_w_qg[jn][:, dq:], rows=(nsa_b_g[jn],), post=_post_sig_bias)
            (q_p, q_s), (g_p, g_s) = split_seq(q), split_seq(gates)
            ns_p = -(-tp // SEL_BLOCK)
            ns_s = -(-(past_len + ts) // SEL_BLOCK)
            o_p, sel_p = nsa_cmp(q_p, ctx['kc_p'], ctx['vc_p'], g_p, q_offset=0, n_sel_blocks=ns_p, tq=tq_p)
            o_p = nsa_attn(q_p, pages_p, tbl_sel_p, kp0_sel_p, 2, 3, None, sel_p, g_p, o_p, mode="sel", q_offset=0,
                           tq=tq_p, tail_pos0=0, branch=1)
            o_p = nsa_attn(q_p, pages_p, tbl_win_p, kp0_win_p, 4, 5, None, None, g_p, o_p, mode="win", q_offset=0,
                           tq=tq_p, tail_pos0=0, branch=2)
            o_s, sel_s = nsa_cmp(q_s, ctx['kc_s'], ctx['vc_s'], g_s, q_offset=past_len, n_sel_blocks=ns_s, tq=ts_pad)
            o_s = nsa_attn(q_s, pages_s, tbl_sel_s, kp0_sel_s, 2, 3, tail_sel, sel_s, g_s, o_s, mode="sel",
                           q_offset=past_len, tq=ts_pad, tail_pos0=past_len, branch=1,
                           ppt=_pick_tile(n_pg_s, (4, 2, 1)))
            o_s = nsa_attn(q_s, pages_w, tbl_win_s, kp0_win_s, 0, 1, tail_win, None, g_s, o_s, mode="win",
                           q_offset=past_len, tq=ts_pad, tail_pos0=past_len, branch=2,
                           ppt=_pick_tile(win_buf // PAGE, (4, 2, 1)))
            h = linear(merge_seq(o_p, o_s), nsa_w_o, widx=(jn,))
        x, x_bf16 = layer_norm(x, ln_g[i, 0], ln_b[i, 0], h=h, alpha=alpha, with_bf16=True)
        mo = moe_ffn(x, x_bf16, n_tok, i, moe_router, moe_bias, moe_w_in, moe_w_out, moe_ws_in, moe_ws_out)
        x = layer_norm(x, ln_g[i, 1], ln_b[i, 1], h=mo, alpha=alpha)
        t = linear(pe, ple_proj, xidx=(i,), widx=(i,))
        x = linear(x, ple_gate, widx=(i,), rows=(ple_gate_b[i],), fulls=(x, t), post=_post_ple)

    y_p = x[:n_p].reshape(bp, tp, d)
    y_s = x[n_p:n_tok].reshape(bs, ts, d)
    return (y_p, y_s, kv_rows_p, kv_rows_s, win_p, win_s, jnp.stack(wkvs_p), jnp.stack(wkvs_s),
            jnp.stack(shifts_p), jnp.stack(shifts_s))
```

```python
import functools

import jax
import jax.numpy as jnp
from jax import lax
from jax.experimental import pallas as pl
from jax.experimental.pallas import tpu as pltpu

F32 = jnp.float32
BF16 = jnp.bfloat16

RW_HEAD = 64
GN_EPS = 64e-5
N_KV = 4
HEAD_DIM = 128
CMP_BLOCK = 32
SEL_BLOCK = 64
N_SEL = 16
WINDOW = 512
FORCED_BONUS = 100.0
N_GROUPS = 8
TOPK_GROUPS = 4
TOP_K = 8
ROUTED_SCALE = 2.5
LN_EPS = 1e-5
NEG = -1e30

LANES = 128
SUBLANES = 8
PAGE = 128
VMEM_LIMIT = 56 * 1024 * 1024
LINEAR_W_BYTES = 16 * 1024 * 1024
LINEAR_VMEM_BYTES = 46 * 1024 * 1024


def _cparams(n_axes):
    return pltpu.CompilerParams(dimension_semantics=("arbitrary",) * n_axes, vmem_limit_bytes=VMEM_LIMIT)


def _split(x):
    hi = x.astype(BF16)
    lo = (x - hi.astype(F32)).astype(BF16)
    return hi, lo


_NN = (((1,), (0,)), ((), ()))
_NT = (((1,), (1,)), ((), ()))
_TN = (((0,), (0,)), ((), ()))


def _dg(a, b, dims):
    return lax.dot_general(a, b, dims, preferred_element_type=F32)


def _dot(a, b, passes=1, dims=_NN):
    if passes == 1:
        return _dg(a.astype(BF16), b.astype(BF16), dims)
    ah, al = _split(a)
    bh, bl = _split(b)
    return _dg(ah, bh, dims) + (_dg(ah, bl, dims) + _dg(al, bh, dims))


def _round_up(x, m):
    return (x + m - 1) // m * m


def _pick_tile(n, prefs):
    for t in prefs:
        if n % t == 0:
            return t
    return n


def _linear_kernel(*refs, passes, has_mix, glu, n_row, n_full, post):
    it = iter(refs)
    x_ref = next(it)
    xp_ref = next(it) if has_mix else None
    mix_ref = next(it) if has_mix else None
    w_refs = [next(it) for _ in range(2 if glu else 1)]
    row_refs = [next(it) for _ in range(n_row)]
    full_refs = [next(it) for _ in range(n_full)]
    o_ref = next(it)
    scr = list(it)
    per = 2 if passes == 3 else 1

    @pl.when(pl.program_id(1) == 0)
    def _():
        for wi, w_ref in enumerate(w_refs):
            w = w_ref[...]
            if passes == 3:
                hi, lo = _split(w)
                scr[wi * per][...] = hi
                scr[wi * per + 1][...] = lo
            else:
                scr[wi * per][...] = w.astype(BF16)

    x = x_ref[...]
    if has_mix:
        x = x + (xp_ref[...] - x) * mix_ref[...]
    if passes == 3:
        xh, xl = _split(x)
    else:
        xh = x.astype(BF16)
    accs = []
    for wi in range(len(w_refs)):
        wh = scr[wi * per][...]
        acc = jnp.dot(xh, wh, preferred_element_type=F32)
        if passes == 3:
            wl = scr[wi * per + 1][...]
            acc = acc + (jnp.dot(xh, wl, preferred_element_type=F32) + jnp.dot(xl, wh, preferred_element_type=F32))
        accs.append(acc)
    extras = [r[...] for r in row_refs] + [r[...] for r in full_refs]
    o_ref[...] = post(*accs, *extras)


def _identity(acc):
    return acc


def linear(x, w, *, xidx=(), widx=(), ncols=None, passes=1, mix=None, glu=False, rows=(), fulls=(), post=_identity,
           tm=None, tn=None):
    m, k = x.shape[-2:]
    n = (ncols or w.shape[-1]) // (2 if glu else 1)
    lead = (None,) * len(widx)
    n_w = 2 if glu else 1
    tn = tn or next((c for c in (n, n // 2, 512, 256, 128)
                     if n % c == 0 and (c % LANES == 0 or c == n) and n_w * k * c * 4 <= LINEAR_W_BYTES), n)
    w_bytes = n_w * k * tn * (4 + 2 * (2 if passes == 3 else 1))
    per_row = (2 if mix is not None else 1) * k * 4 * 2 + (1 + len(fulls)) * tn * 4 * 2
    tm = tm or next((c for c in (768, 512, 384, 256, 128, 64, 32, 16, 8)
                     if m % c == 0 and w_bytes + c * per_row <= LINEAR_VMEM_BYTES), m)
    nb = n // tn
    grid = (nb, m // tm)
    x_spec = pl.BlockSpec((None,) * len(xidx) + (tm, k), lambda j, i: xidx + (i, 0))
    in_specs = [x_spec]
    args = [x]
    if mix is not None:
        in_specs += [x_spec, pl.BlockSpec((1, k), lambda j, i: (0, 0))]
        args += [mix[0], mix[1].reshape(1, k)]
    in_specs.append(pl.BlockSpec(lead + (k, tn), lambda j, i: widx + (0, j), pipeline_mode=pl.Buffered(1)))
    args.append(w)
    if glu:
        in_specs.append(pl.BlockSpec(lead + (k, tn), lambda j, i: widx + (0, j + nb), pipeline_mode=pl.Buffered(1)))
        args.append(w)
    for r in rows:
        in_specs.append(pl.BlockSpec((1, tn), lambda j, i: (0, j)))
        args.append(r.reshape(1, n))
    for f in fulls:
        in_specs.append(pl.BlockSpec((tm, tn), lambda j, i: (i, j)))
        args.append(f)
    scratch = [pltpu.VMEM((k, tn), BF16) for _ in range(n_w * (2 if passes == 3 else 1))]
    kern = functools.partial(_linear_kernel, passes=passes, has_mix=mix is not None, glu=glu,
                             n_row=len(rows), n_full=len(fulls), post=post)
    return pl.pallas_call(
        kern,
        grid=grid,
        in_specs=in_specs,
        out_specs=pl.BlockSpec((tm, tn), lambda j, i: (i, j)),
        out_shape=jax.ShapeDtypeStruct((m, n), F32),
        scratch_shapes=scratch,
        compiler_params=_cparams(2),
    )(*args)


def _ln_kernel(*refs, alpha, has_h):
    if has_h:
        x_ref, h_ref, g_ref, b_ref, o_ref = refs
        x = alpha * x_ref[...] + h_ref[...]
    else:
        x_ref, g_ref, b_ref, o_ref = refs
        x = x_ref[...]
    mu = jnp.mean(x, -1, keepdims=True)
    xc = x - mu
    var = jnp.mean(xc * xc, -1, keepdims=True)
    o_ref[...] = xc * lax.rsqrt(var + LN_EPS) * g_ref[...] + b_ref[...]


def layer_norm(x, g, b, h=None, alpha=1.0):
    m, d = x.shape
    tm = _pick_tile(m, (384, 256, 128, 64, 32, 16, 8))
    spec = pl.BlockSpec((tm, d), lambda i: (i, 0))
    vspec = pl.BlockSpec((1, d), lambda i: (0, 0))
    args = [x] + ([h] if h is not None else []) + [g.reshape(1, d), b.reshape(1, d)]
    in_specs = [spec] + ([spec] if h is not None else []) + [vspec, vspec]
    return pl.pallas_call(
        functools.partial(_ln_kernel, alpha=alpha, has_h=h is not None),
        grid=(m // tm,),
        in_specs=in_specs,
        out_specs=spec,
        out_shape=jax.ShapeDtypeStruct((m, d), F32),
        compiler_params=_cparams(1),
    )(*args)


def _bf16r(x):
    return x.astype(BF16).astype(F32)


def _matvec_bf16(s, vec_row):
    return jnp.sum(_bf16r(s) * _bf16r(vec_row), axis=1, keepdims=True)


def _row_from_col(col, eye):
    return jnp.sum(eye * col, axis=0, keepdims=True)


def _col_from_row(row, eye):
    return jnp.sum(eye * row, axis=1, keepdims=True)


def _wkv_out(y, v_h, rk_h, lg, lb):
    mu = jnp.mean(y, -1, keepdims=True)
    yc = y - mu
    var = jnp.mean(yc * yc, -1, keepdims=True)
    return yc * lax.rsqrt(var + GN_EPS) * lg + lb + jnp.sum(rk_h, -1, keepdims=True) * v_h


def _wkv_seq_kernel(r_ref, lw_ref, k_ref, v_ref, a_ref, g_ref, kk_ref, ka_ref, rk_ref, lg_ref, lb_ref, s0_ref,
                    y_ref, s_ref, *, steps, heads):
    N = RW_HEAD
    r = r_ref[0]
    k = k_ref[0]
    v = v_ref[0]
    a = a_ref[0]
    w = jnp.exp(lw_ref[0])
    kkw = k * kk_ref[...]
    k2 = k * (1.0 + (a - 1.0) * ka_ref[...])
    rk = r * k2 * rk_ref[...]
    eye = (lax.broadcasted_iota(jnp.int32, (N, N), 0) == lax.broadcasted_iota(jnp.int32, (N, N), 1)).astype(F32)
    tpad = r.shape[0]
    outs = []
    for h in range(heads):
        sl = slice(h * N, (h + 1) * N)
        kk = kkw[:, sl]
        kk = kk * lax.rsqrt(jnp.maximum(jnp.sum(kk * kk, -1, keepdims=True), 1e-24))
        b_h = kk * a[:, sl]
        s = s0_ref[0, h]
        ys = []
        for t in range(steps):
            sa = _matvec_bf16(s, -kk[t:t + 1])
            s = s * w[t:t + 1, sl] + sa * b_h[t:t + 1] + _col_from_row(v[t:t + 1, sl], eye) * k2[t:t + 1, sl]
            ys.append(_row_from_col(_matvec_bf16(s, r[t:t + 1, sl]), eye))
        s_ref[0, h] = s
        y = jnp.concatenate(ys + [jnp.zeros((tpad - steps, N), F32)], axis=0)
        outs.append(_wkv_out(y, v[:, sl], rk[:, sl], lg_ref[:, sl], lb_ref[:, sl]))
    y_ref[0] = jnp.concatenate(outs, axis=1) * g_ref[0]


def wkv7_seq(r, lw, k, v, a, g, k_k, k_a, r_k, lnx_g, lnx_b, s0, *, steps):
    bsz, t, d = r.shape
    n = RW_HEAD
    hh = d // n
    heads = 4 if hh % 4 == 0 else 2
    wide = heads * n
    seq = pl.BlockSpec((1, t, wide), lambda b, hb: (b, 0, hb))
    vec = pl.BlockSpec((1, wide), lambda b, hb: (0, hb))
    st = pl.BlockSpec((1, heads, n, n), lambda b, hb: (b, hb, 0, 0))
    vecs = [z.reshape(1, d) for z in (k_k, k_a, r_k, lnx_g, lnx_b)]
    return pl.pallas_call(
        functools.partial(_wkv_seq_kernel, steps=steps, heads=heads),
        grid=(bsz, hh // heads),
        in_specs=[seq] * 6 + [vec] * 5 + [st],
        out_specs=[seq, st],
        out_shape=[jax.ShapeDtypeStruct((bsz, t, d), F32), jax.ShapeDtypeStruct((bsz, hh, n, n), F32)],
        compiler_params=_cparams(2),
    )(r, lw, k, v, a, g, *vecs, s0)


def _wkv_kernel(r_ref, lw_ref, k_ref, v_ref, a_ref, g_ref, kk_ref, ka_ref, rk_ref, lg_ref, lb_ref, s0_ref,
                y_ref, s_ref, *, chunk, heads, passes):
    L, N = chunk, RW_HEAD

    @pl.when(pl.program_id(2) == 0)
    def _():
        s_ref[...] = s0_ref[...]

    r = r_ref[...]
    lw = lw_ref[...]
    k = k_ref[...]
    v = v_ref[...]
    a = a_ref[...]
    kkw = k * kk_ref[...]
    k2 = k * (1.0 + (a - 1.0) * ka_ref[...])
    rk = r * k2 * rk_ref[...]

    row = lax.broadcasted_iota(jnp.int32, (L, L), 0)
    col = lax.broadcasted_iota(jnp.int32, (L, L), 1)
    tri_incl = (col <= row)
    tri_strict = (col < row)
    tri = tri_incl.astype(BF16)
    l1 = lw.astype(BF16)
    rem = lw - l1.astype(F32)
    l2 = rem.astype(BF16)
    l3 = (rem - l2.astype(F32)).astype(BF16)
    cs = (jnp.dot(tri, l1, preferred_element_type=F32) + jnp.dot(tri, l2, preferred_element_type=F32)
          + jnp.dot(tri, l3, preferred_element_type=F32))
    cs_last = cs[L - 1:L, :]
    e_prev = jnp.exp(cs - lw)
    e_inv = jnp.exp(-cs)
    e_fwd = jnp.exp(cs)
    e_tail = jnp.exp(cs_last - cs)
    gam = jnp.exp(cs_last)

    n_dbl = max(1, (L - 1).bit_length())
    eye = (lax.broadcasted_iota(jnp.int32, (N, N), 0) == lax.broadcasted_iota(jnp.int32, (N, N), 1)).astype(F32)
    H = range(heads)
    sls = [slice(h * N, (h + 1) * N) for h in H]
    kk = [kkw[:, sl] for sl in sls]
    kk = [z * lax.rsqrt(jnp.maximum(jnp.sum(z * z, -1, keepdims=True), 1e-24)) for z in kk]
    b_h = [kk[h] * a[:, sls[h]] for h in H]
    k_h = [k2[:, sl] for sl in sls]
    v_h = [v[:, sl] for sl in sls]
    lhs2 = [jnp.concatenate([-kk[h] * e_prev[:, sls[h]], r[:, sls[h]] * e_fwd[:, sls[h]]], axis=0) for h in H]
    rhs2 = [jnp.concatenate([b_h[h] * e_inv[:, sls[h]], k_h[h] * e_inv[:, sls[h]]], axis=0) for h in H]
    bk = [jnp.concatenate([b_h[h] * e_tail[:, sls[h]], k_h[h] * e_tail[:, sls[h]]], axis=0) for h in H]
    s0 = [s_ref[0, h] for h in H]
    p = [_dot(lhs2[h], rhs2[h], passes, _NT) for h in H]
    q = [_dot(lhs2[h], s0[h], passes, _NT) for h in H]
    pk = [jnp.where(tri_strict, p[h][:L, :L], 0.0) for h in H]
    a_k = [jnp.where(tri_strict, p[h][:L, L:], 0.0) for h in H]
    rbk = [jnp.concatenate([jnp.where(tri_incl, p[h][L:, :L], 0.0), jnp.where(tri_incl, p[h][L:, L:], 0.0)], axis=1)
           for h in H]
    x = [q[h][:L] + _dot(a_k[h], v_h[h], passes) for h in H]
    for d in range(n_dbl):
        x = [x[h] + _dot(pk[h], x[h], passes) for h in H]
        if d + 1 < n_dbl:
            pk = [_dot(pk[h], pk[h], passes) for h in H]
    uv = [jnp.concatenate([x[h], v_h[h]], axis=0) for h in H]
    y = [q[h][L:] + _dot(rbk[h], uv[h], passes) for h in H]
    s_new = [s0[h] * gam[:, sls[h]] + _dot(uv[h], bk[h], passes, _TN) for h in H]
    for h in H:
        s_ref[0, h] = s_new[h]
    last = lax.broadcasted_iota(jnp.int32, (L, 1), 0) == L - 1
    y = [jnp.where(last, _row_from_col(_matvec_bf16(s_new[h], r[L - 1:L, sls[h]]), eye), y[h]) for h in H]
    outs = [_wkv_out(y[h], v_h[h], rk[:, sls[h]], lg_ref[:, sls[h]], lb_ref[:, sls[h]]) for h in H]
    y_ref[...] = jnp.concatenate(outs, axis=1) * g_ref[...]


def wkv7(r, lw, k, v, a, g, k_k, k_a, r_k, lnx_g, lnx_b, s0, *, seq_len, passes):
    bsz, t = s0.shape[0], seq_len
    d = r.shape[1]
    n = RW_HEAD
    hh = d // n
    heads = _pick_tile(hh, (16, 4, 2))
    chunk = _pick_tile(t, (128, 64))
    wide = heads * n
    n_chunks = t // chunk
    seq = pl.BlockSpec((chunk, wide), lambda b, hb, c: (b * n_chunks + c, hb))
    vec = pl.BlockSpec((1, wide), lambda b, hb, c: (0, hb))
    st = pl.BlockSpec((1, heads, n, n), lambda b, hb, c: (b, hb, 0, 0))
    vecs = [z.reshape(1, d) for z in (k_k, k_a, r_k, lnx_g, lnx_b)]
    return pl.pallas_call(
        functools.partial(_wkv_kernel, chunk=chunk, heads=heads, passes=passes),
        grid=(bsz, hh // heads, n_chunks),
        in_specs=[seq] * 6 + [vec] * 5 + [st],
        out_specs=[seq, st],
        out_shape=[jax.ShapeDtypeStruct((bsz * t, d), F32), jax.ShapeDtypeStruct((bsz, hh, n, n), F32)],
        compiler_params=_cparams(3),
    )(r, lw, k, v, a, g, *vecs, s0)


def _route_kernel(x_ref, rt_ref, bias_ref, eidx_ref, w_ref, pos_ref, cnt_ref, *, n_valid):
    ne = rt_ref.shape[0]
    per = ne // N_GROUPS
    logits = _dot(rt_ref[...], x_ref[...], 1, _NT)
    s = jax.nn.sigmoid(logits)
    sb = s + bias_ref[...]
    tm = sb.shape[1]
    sub = lax.broadcasted_iota(jnp.int32, (per, tm), 0)
    gs_rows = []
    for gi in range(N_GROUPS):
        tile = sb[gi * per:(gi + 1) * per, :]
        m1 = jnp.max(tile, axis=0, keepdims=True)
        first = jnp.min(jnp.where(tile == m1, sub, per), axis=0, keepdims=True)
        m2 = jnp.max(jnp.where(sub == first, -jnp.inf, tile), axis=0, keepdims=True)
        gs_rows.append(m1 + m2)
    keep_rows = []
    for gi in range(N_GROUPS):
        cnt = jnp.zeros((1, tm), jnp.int32)
        for gj in range(N_GROUPS):
            if gj == gi:
                continue
            beats = (gs_rows[gj] > gs_rows[gi]) if gj > gi else (gs_rows[gj] >= gs_rows[gi])
            cnt = cnt + beats.astype(jnp.int32)
        keep_rows.append(jnp.broadcast_to(cnt, (per, tm)))
    masked = jnp.where(jnp.concatenate(keep_rows, axis=0) < TOPK_GROUPS, sb, -1e9)
    eidx = lax.broadcasted_iota(jnp.int32, (ne, tm), 0)
    rank = jnp.zeros((ne, tm), jnp.int32)
    for ej in range(ne):
        rowv = masked[ej:ej + 1, :]
        rank = rank + jnp.where(eidx > ej, (rowv >= masked).astype(jnp.int32), (rowv > masked).astype(jnp.int32))
    tok = pl.program_id(0) * tm + lax.broadcasted_iota(jnp.int32, (ne, tm), 1)
    rank = jnp.where(tok < n_valid, rank, ne)
    sel = rank < TOP_K
    self32 = sel.astype(F32)
    den = jnp.sum(jnp.where(sel, s, 0.0), axis=0, keepdims=True)
    wgt = jnp.where(sel, s / den * ROUTED_SCALE, 0.0)

    @pl.when(pl.program_id(0) == 0)
    def _():
        cnt_ref[...] = jnp.zeros(cnt_ref.shape, F32)

    upper = (lax.broadcasted_iota(jnp.int32, (tm, tm), 0) < lax.broadcasted_iota(jnp.int32, (tm, tm), 1)).astype(BF16)
    carry = cnt_ref[...]
    pos = jnp.dot(self32.astype(BF16), upper, preferred_element_type=F32) + carry[:, :1]
    cnt_ref[...] = carry + jnp.sum(self32, axis=1, keepdims=True)
    eidx_f = eidx.astype(F32)
    rows_e, rows_w, rows_p = [], [], []
    for kk in range(TOP_K):
        hit = rank == kk
        rows_e.append(jnp.sum(jnp.where(hit, eidx_f, 0.0), axis=0, keepdims=True))
        rows_w.append(jnp.sum(jnp.where(hit, wgt, 0.0), axis=0, keepdims=True))
        rows_p.append(jnp.sum(jnp.where(hit, pos, 0.0), axis=0, keepdims=True))
    eidx_ref[...] = jnp.concatenate(rows_e, axis=0).astype(jnp.int32)
    w_ref[...] = jnp.concatenate(rows_w, axis=0)
    pos_ref[...] = jnp.concatenate(rows_p, axis=0).astype(jnp.int32)


def route(x, router_t, bias, n_valid):
    m, d = x.shape
    ne = router_t.shape[0]
    tm = _pick_tile(m, (256, 128))
    out = pl.BlockSpec((TOP_K, tm), lambda i: (0, i))
    return pl.pallas_call(
        functools.partial(_route_kernel, n_valid=n_valid),
        grid=(m // tm,),
        in_specs=[pl.BlockSpec((tm, d), lambda i: (i, 0)), pl.BlockSpec((ne, d), lambda i: (0, 0)),
                  pl.BlockSpec((ne, 1), lambda i: (0, 0))],
        out_specs=[out, out, out, pl.BlockSpec((ne, LANES), lambda i: (0, 0))],
        out_shape=[jax.ShapeDtypeStruct((TOP_K, m), jnp.int32), jax.ShapeDtypeStruct((TOP_K, m), F32),
                   jax.ShapeDtypeStruct((TOP_K, m), jnp.int32), jax.ShapeDtypeStruct((ne, LANES), F32)],
        compiler_params=_cparams(1),
    )(x, router_t, bias.reshape(ne, 1))


def _experts_kernel(be_ref, nu_ref, xs_ref, win_ref, wout_ref, o_ref, win_s, wout_s):
    i = pl.program_id(0)
    f = wout_ref.shape[0]

    @pl.when(i < nu_ref[0])
    def _():
        changed = jnp.logical_or(i == 0, be_ref[i] != be_ref[jnp.maximum(i - 1, 0)])

        @pl.when(changed)
        def _():
            win_s[...] = win_ref[...].astype(BF16)
            wout_s[...] = wout_ref[...].astype(BF16)

        hu = jnp.dot(xs_ref[...].astype(BF16), win_s[...], preferred_element_type=F32)
        hid = jax.nn.silu(hu[:, :f]) * hu[:, f:]
        y = jnp.dot(hid.astype(BF16), wout_s[...], preferred_element_type=F32)
        o_ref[...] = y.astype(BF16)

    @pl.when(i >= nu_ref[0])
    def _():
        o_ref[...] = jnp.zeros(o_ref.shape, BF16)


def experts(xs, blk_expert, n_used, w_in, w_out, layer, tm):
    rws, d = xs.shape
    f2 = w_in.shape[3]
    f = f2 // 2
    grid_spec = pltpu.PrefetchScalarGridSpec(
        num_scalar_prefetch=2,
        grid=(rws // tm,),
        in_specs=[
            pl.BlockSpec((tm, d), lambda i, be, nu: (jnp.minimum(i, nu[0] - 1), 0)),
            pl.BlockSpec((None, None, d, f2), lambda i, be, nu: (layer, be[i], 0, 0)),
            pl.BlockSpec((None, None, f, d), lambda i, be, nu: (layer, be[i], 0, 0)),
        ],
        out_specs=pl.BlockSpec((tm, d), lambda i, be, nu: (i, 0)),
        scratch_shapes=[pltpu.VMEM((d, f2), BF16), pltpu.VMEM((f, d), BF16)],
    )
    return pl.pallas_call(
        _experts_kernel,
        grid_spec=grid_spec,
        out_shape=jax.ShapeDtypeStruct((rws, d), BF16),
        compiler_params=_cparams(1),
    )(blk_expert, n_used, xs, w_in, w_out)


def moe_ffn(x, n_valid, layer, router, bias, w_in, w_out, ws_in, ws_out):
    m, d = x.shape
    ne = router.shape[2]
    eidx, ew, posk, cnt = route(x, router[layer].T, bias[layer], n_valid)
    tm = 256 if m * TOP_K // ne >= 256 else _round_up(max(m * TOP_K // ne, SUBLANES), SUBLANES)
    nk = m * TOP_K
    counts = cnt[:, 0].astype(jnp.int32)
    padded = (counts + tm - 1) // tm * tm
    pad_end = jnp.cumsum(padded)
    pad_start = pad_end - padded
    n_blocks = (nk + tm - 1) // tm + ne
    n_rows = n_blocks * tm
    tok = jnp.arange(m, dtype=jnp.int32)
    first_row = jnp.sum(jnp.where(eidx[:, :, None] == jnp.arange(ne, dtype=jnp.int32), pad_start, 0), axis=-1)
    dest = jnp.where(tok[None, :] < n_valid, first_row + posk, n_rows - 1).astype(jnp.int32)
    token_of_row = jnp.zeros((n_rows,), jnp.int32).at[dest.reshape(-1)].set(jnp.tile(tok, TOP_K))
    blk_start = jnp.arange(n_blocks, dtype=jnp.int32) * tm
    blk_expert = jnp.minimum(jnp.sum((pad_end[None, :] <= blk_start[:, None]).astype(jnp.int32), axis=1), ne - 1)
    n_used = (pad_end[-1:] // tm).astype(jnp.int32)
    xs = jnp.take(x, token_of_row, axis=0, mode="clip")
    yb = experts(xs, blk_expert, n_used, w_in, w_out, layer, tm)
    wr = ew.astype(BF16).astype(F32)
    picked = yb.at[dest.T].get(mode="promise_in_bounds").astype(F32)
    routed = jnp.sum(picked * wr.T[:, :, None], axis=1)
    hid = linear(x, ws_in, widx=(layer,), glu=True, post=lambda a, b: jax.nn.silu(a) * b)
    return linear(hid, ws_out, widx=(layer,), fulls=(routed,), post=lambda acc, rt: acc + rt)


def _gelu_tanh(x):
    return 0.5 * x * (1.0 + jnp.tanh(0.7978845608028654 * (x + 0.044715 * x * x * x)))


def _group_tile(ref, g):
    if len(ref.shape) == 3:
        return ref[:, g, :]
    return ref[:, g * HEAD_DIM:(g + 1) * HEAD_DIM]


def _page_spec(pages, index_fn, slot_fn):
    if pages.ndim == 5:
        return pl.BlockSpec((None, PAGE, None, N_KV, HEAD_DIM), lambda *a: (index_fn(*a), 0, slot_fn(*a), 0, 0))
    return pl.BlockSpec((None, PAGE, N_KV * HEAD_DIM), lambda *a: (index_fn(*a), 0, slot_fn(*a)))


def _compress_kernel(tbl_ref, *refs, group, ppt):
    pg_refs = refs[:ppt]
    pos_ref, w1_ref, w2_ref, o_ref, w1_s = refs[ppt:ppt + 5]
    slabs = refs[ppt + 5:]
    p = pl.program_id(2)
    pin = p % (group // ppt)

    @pl.when(p == 0)
    def _():
        w1_s[...] = w1_ref[0].astype(BF16)

    for pi, pg_ref in enumerate(pg_refs):
        for g in range(N_KV):
            slabs[g][pl.ds(pl.multiple_of((pin * ppt + pi) * PAGE, PAGE), PAGE), :] = _group_tile(pg_ref, g)

    @pl.when(pin == group // ppt - 1)
    def _():
        per_page = PAGE // CMP_BLOCK
        nrow = group * per_page
        hid = w1_ref.shape[2]
        acc = jnp.zeros((N_KV * nrow, hid), F32)
        for t in range(CMP_BLOCK):
            lhs = jnp.concatenate([slabs[g][pl.ds(t, nrow, stride=CMP_BLOCK), :] for g in range(N_KV)], axis=0)
            lhs = (lhs + pos_ref[0, t:t + 1, :]).astype(BF16)
            acc = acc + jnp.dot(lhs, w1_s[t * HEAD_DIM:(t + 1) * HEAD_DIM, :], preferred_element_type=F32)
        out = _dot(_gelu_tanh(acc), w2_ref[0])
        for g in range(N_KV):
            o_ref[0, 0, :, g * HEAD_DIM:(g + 1) * HEAD_DIM] = out[g * nrow:(g + 1) * nrow]


def compress(pages, table, slot0, cmp_pos, cmp_w1, cmp_w2):
    bsz, n_pg = table.shape
    group = min(32, n_pg)
    ppt = _pick_tile(group, (4, 2, 1))
    gw = N_KV * HEAD_DIM
    per_page = PAGE // CMP_BLOCK
    hid = cmp_w1.shape[2]

    def page_of(pi):
        return lambda b, s, p, tbl: tbl[b * n_pg + p * ppt + pi]

    grid_spec = pltpu.PrefetchScalarGridSpec(
        num_scalar_prefetch=1,
        grid=(bsz, 2, n_pg // ppt),
        in_specs=[_page_spec(pages, page_of(pi), lambda b, s, p, tbl: slot0 + s) for pi in range(ppt)] + [
            pl.BlockSpec((1, CMP_BLOCK, HEAD_DIM), lambda b, s, p, tbl: (s, 0, 0)),
            pl.BlockSpec((1, CMP_BLOCK * HEAD_DIM, hid), lambda b, s, p, tbl: (s, 0, 0)),
            pl.BlockSpec((1, hid, HEAD_DIM), lambda b, s, p, tbl: (s, 0, 0)),
        ],
        out_specs=pl.BlockSpec((1, 1, group * per_page, gw), lambda b, s, p, tbl: (b, s, p // (group // ppt), 0)),
        scratch_shapes=[pltpu.VMEM((CMP_BLOCK * HEAD_DIM, hid), BF16)]
        + [pltpu.VMEM((group * PAGE, HEAD_DIM), F32) for _ in range(N_KV)],
    )
    return pl.pallas_call(
        functools.partial(_compress_kernel, group=group, ppt=ppt),
        grid_spec=grid_spec,
        out_shape=jax.ShapeDtypeStruct((bsz, 2, n_pg * per_page, gw), F32),
        compiler_params=_cparams(3),
    )(table.reshape(-1), *([pages] * ppt), cmp_pos, cmp_w1, cmp_w2)


def _nsa_cmp_kernel(q_ref, kc_ref, vc_ref, gate_ref, o_ref, sel_ref, score_s, rank_s, *, q_offset, n_sel_blocks,
                    tq, half):
    qt = pl.program_id(1)
    ncp = kc_ref.shape[1]
    nsp = sel_ref.shape[2]
    scale = HEAD_DIM ** -0.5
    rpg = N_KV
    n_heads = q_ref.shape[2] // HEAD_DIM
    rpg = n_heads // N_KV
    qpos = q_offset + qt * tq + lax.broadcasted_iota(jnp.int32, (1, tq), 1)
    cidx = lax.broadcasted_iota(jnp.int32, (ncp, 1), 0)
    corig = jnp.where(cidx < half, 2 * cidx, 2 * (cidx - half) + 1)
    cmask = ((corig + 1) * CMP_BLOCK - 1) <= qpos
    blk = lax.broadcasted_iota(jnp.int32, (nsp, 1), 0)
    cur = qpos // SEL_BLOCK
    forced = (blk == 0) | (blk == cur) | (blk == cur - 1)
    valid = (blk * SEL_BLOCK) <= qpos
    gate = gate_ref[0]
    for g in range(N_KV):
        kc = kc_ref[0, :, g * HEAD_DIM:(g + 1) * HEAD_DIM]
        vc = vc_ref[0, :, g * HEAD_DIM:(g + 1) * HEAD_DIM]
        imp = jnp.zeros((ncp, tq), F32)
        for h in range(rpg):
            hd = g * rpg + h
            qh = q_ref[0, :, hd * HEAD_DIM:(hd + 1) * HEAD_DIM]
            s = _dot(kc, qh, 1, _NT) * scale
            s = jnp.where(cmask, s, NEG)
            mx = jnp.max(s, axis=0, keepdims=True)
            e = jnp.where(cmask, jnp.exp(s - mx), 0.0)
            p = e / jnp.maximum(jnp.sum(e, axis=0, keepdims=True), 1e-30)
            imp = imp + p
            oc = _dot(p, vc, 1, _TN)
            o_ref[0, :, hd * HEAD_DIM:(hd + 1) * HEAD_DIM] = gate[:, hd * 3:hd * 3 + 1] * oc
        imp_blk = imp[:half] + imp[half:]
        if nsp > half:
            imp_blk = jnp.concatenate([imp_blk, jnp.zeros((nsp - half, tq), F32)], axis=0)
        else:
            imp_blk = imp_blk[:nsp]
        score = jnp.where(valid, imp_blk + FORCED_BONUS * forced.astype(F32), -1.0)
        score = jnp.where(blk < n_sel_blocks, score, -2.0)
        score_s[...] = score
        rank_s[...] = jnp.zeros((nsp, tq), jnp.int32)

        def body(j, carry):
            rowv = score_s[pl.ds(j, 1), :]
            rank_s[...] = rank_s[...] + jnp.where(blk > j, (rowv >= score).astype(jnp.int32),
                                                  (rowv > score).astype(jnp.int32))
            return carry

        lax.fori_loop(0, n_sel_blocks, body, 0)
        sel_ref[0, g] = (rank_s[...] < N_SEL).astype(F32)


def nsa_cmp(q, kc, vc, gates, *, q_offset, n_sel_blocks, tq):
    bsz, t, dq = q.shape
    ncp = kc.shape[1]
    half = ncp // 2
    nsp = _round_up(n_sel_blocks, SUBLANES)
    gw = N_KV * HEAD_DIM
    qspec = pl.BlockSpec((1, tq, dq), lambda b, i: (b, i, 0))
    cspec = pl.BlockSpec((1, ncp, gw), lambda b, i: (b, 0, 0))
    return pl.pallas_call(
        functools.partial(_nsa_cmp_kernel, q_offset=q_offset, n_sel_blocks=n_sel_blocks, tq=tq, half=half),
        grid=(bsz, t // tq),
        in_specs=[qspec, cspec, cspec, pl.BlockSpec((1, tq, gates.shape[2]), lambda b, i: (b, i, 0))],
        out_specs=[qspec, pl.BlockSpec((1, N_KV, nsp, tq), lambda b, i: (b, 0, 0, i))],
        out_shape=[jax.ShapeDtypeStruct((bsz, t, dq), F32), jax.ShapeDtypeStruct((bsz, N_KV, nsp, t), F32)],
        scratch_shapes=[pltpu.VMEM((nsp, tq), F32), pltpu.VMEM((nsp, tq), jnp.int32)],
        compiler_params=_cparams(2),
    )(q, kc, vc, gates)


def _nsa_attn_kernel(*refs, mode, q_offset, tq, n_kt, ppt, has_tail, tail_pos0, branch):
    tbl_ref, kp0_ref = refs[0], refs[1]
    it = iter(refs[2:])
    q_ref = next(it)
    k_refs = [next(it) for _ in range(ppt)]
    v_refs = [next(it) for _ in range(ppt)]
    tk_ref = next(it) if has_tail else None
    tv_ref = next(it) if has_tail else None
    sel_ref = next(it) if mode == "sel" else None
    gate_ref = next(it)
    prev_ref = next(it)
    o_ref = next(it)
    m_s, l_s, acc_s = next(it), next(it), next(it)

    qt = pl.program_id(1)
    j = pl.program_id(2)
    n_heads = q_ref.shape[2] // HEAD_DIM
    rpg = n_heads // N_KV
    scale = HEAD_DIM ** -0.5
    width = WINDOW if mode == "win" else (1 << 30)
    n_steps = n_kt // ppt
    G = range(N_KV)

    @pl.when(j == 0)
    def _():
        m_s[...] = jnp.full(m_s.shape, NEG, F32)
        l_s[...] = jnp.zeros(l_s.shape, F32)
        acc_s[...] = jnp.zeros(acc_s.shape, F32)

    q_first = q_offset + qt * tq
    qpos = q_first + lax.broadcasted_iota(jnp.int32, (tq, 1), 0)

    def tile(kt_refs, vt_refs, kp0s):
        lane = lax.broadcasted_iota(jnp.int32, (1, PAGE), 1)
        kpos = jnp.concatenate([jnp.where(kp >= 0, kp, 1 << 30) + lane for kp in kp0s], axis=1)
        dist = qpos - kpos
        band = (dist >= 0) & (dist < width)
        if mode == "sel":
            nsp = sel_ref.shape[2]
            blk = lax.broadcasted_iota(jnp.int32, (nsp, 1), 0)
            expand = (blk == kpos // SEL_BLOCK).astype(BF16)
            hits = [_dg(sel_ref[0, g].astype(BF16), expand, _TN) for g in G]
            masks = [band & (hits[g] > 0.5) for g in G]
        else:
            masks = [band for g in G]
        masks = [jnp.concatenate([mk] * rpg, axis=0) for mk in masks]
        qg = [jnp.concatenate([q_ref[0, :, (g * rpg + h) * HEAD_DIM:(g * rpg + h + 1) * HEAD_DIM] for h in range(rpg)],
                              axis=0) for g in G]
        kg = [jnp.concatenate([_group_tile(r, g) for r in kt_refs], axis=0) for g in G]
        vg = [jnp.concatenate([_group_tile(r, g) for r in vt_refs], axis=0) for g in G]
        s = [_dot(qg[g], kg[g], 1, _NT) * scale for g in G]
        s = [jnp.where(masks[g], s[g], NEG) for g in G]
        m_old = [m_s[g] for g in G]
        m_new = [jnp.maximum(m_old[g], jnp.max(s[g], -1, keepdims=True)) for g in G]
        alpha = [jnp.exp(m_old[g] - m_new[g]) for g in G]
        p = [jnp.where(masks[g], jnp.exp(s[g] - m_new[g]), 0.0) for g in G]
        pv = [_dot(p[g], vg[g], 1) for g in G]
        for g in G:
            l_s[g] = alpha[g] * l_s[g] + jnp.sum(p[g], -1, keepdims=True)
            acc_s[g] = alpha[g] * acc_s[g] + pv[g]
            m_s[g] = m_new[g]

    base = (pl.program_id(0) * pl.num_programs(1) + qt) * n_kt + jnp.minimum(j, n_steps - 1) * ppt
    kp0s = [kp0_ref[base + pi] for pi in range(ppt)]
    live = None
    for kp in kp0s:
        ok = (kp >= 0) & (kp <= q_first + tq - 1) & (kp + PAGE - 1 > q_first - width)
        live = ok if live is None else (live | ok)
    if has_tail:
        live = live & (j < n_steps)

    @pl.when(live)
    def _():
        tile(k_refs, v_refs, kp0s)

    if has_tail:
        @pl.when(j == n_steps)
        def _():
            tile([tk_ref], [tv_ref], [tail_pos0])

    @pl.when(j == pl.num_programs(2) - 1)
    def _():
        gate = gate_ref[0]
        for g in range(N_KV):
            o = acc_s[g] / jnp.maximum(l_s[g], 1e-30)
            for h in range(rpg):
                hd = g * rpg + h
                cs = slice(hd * HEAD_DIM, (hd + 1) * HEAD_DIM)
                o_ref[0, :, cs] = prev_ref[0, :, cs] + gate[:, hd * 3 + branch:hd * 3 + branch + 1] * o[h * tq:(h + 1) * tq]


def nsa_attn(q, pages, tbl, kp0, slot_k, slot_v, tail, sel, gates, prev, *, mode, q_offset, tq, tail_pos0, branch,
             ppt=1):
    bsz, t, dq = q.shape
    n_qt = t // tq
    n_kt = tbl.shape[2]
    assert n_kt % ppt == 0
    gw = N_KV * HEAD_DIM
    n_heads = dq // HEAD_DIM
    rows = (n_heads // N_KV) * tq
    has_tail = tail is not None
    n_steps = n_kt // ppt + (1 if has_tail else 0)

    def page_idx(pi):
        return lambda b, i, j, tb, kp: tb[(b * n_qt + i) * n_kt + jnp.minimum(j, n_kt // ppt - 1) * ppt + pi]

    qspec = pl.BlockSpec((1, tq, dq), lambda b, i, j, tb, kp: (b, i, 0))
    in_specs = [qspec]
    in_specs += [_page_spec(pages, page_idx(pi), lambda *a: slot_k) for pi in range(ppt)]
    in_specs += [_page_spec(pages, page_idx(pi), lambda *a: slot_v) for pi in range(ppt)]
    args = [q] + [pages] * (2 * ppt)
    if has_tail:
        in_specs += [pl.BlockSpec((None, PAGE, gw), lambda b, i, j, tb, kp: (b, 0, 0)),
                     pl.BlockSpec((None, PAGE, gw), lambda b, i, j, tb, kp: (b, 0, 1))]
        args += [tail, tail]
    if mode == "sel":
        nsp = sel.shape[2]
        in_specs.append(pl.BlockSpec((1, N_KV, nsp, tq), lambda b, i, j, tb, kp: (b, 0, 0, i)))
        args.append(sel)
    in_specs += [pl.BlockSpec((1, tq, gates.shape[2]), lambda b, i, j, tb, kp: (b, i, 0)), qspec]
    args += [gates, prev]
    grid_spec = pltpu.PrefetchScalarGridSpec(
        num_scalar_prefetch=2,
        grid=(bsz, n_qt, n_steps),
        in_specs=in_specs,
        out_specs=qspec,
        scratch_shapes=[pltpu.VMEM((N_KV, rows, 1), F32), pltpu.VMEM((N_KV, rows, 1), F32),
                        pltpu.VMEM((N_KV, rows, HEAD_DIM), F32)],
    )
    return pl.pallas_call(
        functools.partial(_nsa_attn_kernel, mode=mode, q_offset=q_offset, tq=tq, n_kt=n_kt, ppt=ppt,
                          has_tail=has_tail, tail_pos0=tail_pos0, branch=branch),
        grid_spec=grid_spec,
        out_shape=jax.ShapeDtypeStruct((bsz, t, dq), F32),
        compiler_params=_cparams(3),
    )(tbl.reshape(-1).astype(jnp.int32), kp0.reshape(-1).astype(jnp.int32), *args)


def _softplus(z):
    return jnp.maximum(z, 0.0) + jnp.log(1.0 + jnp.exp(-jnp.abs(z)))


def _post_logdecay(acc, w0):
    return -jnp.exp(-_softplus(-(w0 + acc)) - 0.5)


def _post_sig_bias(acc, b):
    return jax.nn.sigmoid(acc + b)


def _post_vres(acc, v0, v, vf):
    return v + (vf - v) * jax.nn.sigmoid(v0 + acc)


def _post_ple(acc, b, x, t):
    return x + jax.nn.sigmoid(acc + b) * t


def _pad_rows(x, m):
    return jnp.pad(x, ((0, m - x.shape[0]),) + ((0, 0),) * (x.ndim - 1))


def kernel(x_prompt, x_sample, cache_kv_pages, cache_win_kv, state_wkv, state_shift, page_table, p_prompt, p_sample, ln_g, ln_b, rw_mix, rw_w_rkv, rw_w0, rw_w1, rw_w2, rw_a0, rw_a1, rw_a2, rw_v0, rw_v1, rw_v2, rw_g1, rw_g2, rw_k_k, rw_k_a, rw_r_k, rw_lnx_g, rw_lnx_b, rw_w_o, kv_ln_g, kv_ln_b, w_kv, cmp_pos, cmp_w1, cmp_w2, nsa_w_qg, nsa_b_g, nsa_w_o, moe_router, moe_bias, moe_w_in, moe_w_out, moe_ws_in, moe_ws_out, ple_proj, ple_gate, ple_gate_b):
    bp, tp, d = x_prompt.shape
    bs, ts, _ = x_sample.shape
    depth = ln_g.shape[0]
    n_a = state_wkv.shape[0]
    alpha = (2 * depth) ** 0.25
    n_p, n_s = bp * tp, bs * ts
    n_tok = n_p + n_s
    m = _round_up(n_tok, 768) if n_tok > 768 else _round_up(n_tok, SUBLANES)
    ts_pad = _round_up(ts, SUBLANES)
    hh = d // RW_HEAD
    gw = N_KV * HEAD_DIM
    n_pg_s = page_table.shape[1]
    past_len = n_pg_s * cache_kv_pages.shape[1]
    win_buf = cache_win_kv.shape[1]
    assert cache_kv_pages.shape[1] == PAGE and tp % PAGE == 0 and win_buf % PAGE == 0 and ts < CMP_BLOCK

    def join(a_p, a_s):
        return _pad_rows(jnp.concatenate([a_p.reshape(n_p, -1), a_s.reshape(n_s, -1)], axis=0), m)

    def split_seq(z):
        zp = z[:n_p].reshape(bp, tp, -1)
        zs = jnp.pad(z[n_p:n_tok].reshape(bs, ts, -1), ((0, 0), (0, ts_pad - ts), (0, 0)))
        return zp, zs

    def merge_seq(zp, zs):
        return join(zp, zs[:, :ts])

    x = join(x_prompt, x_sample)
    pe = jnp.concatenate([p_prompt.reshape(depth, n_p, -1), p_sample.reshape(depth, n_s, -1)], axis=1)
    pe = jnp.pad(pe, ((0, 0), (0, m - n_tok), (0, 0)))

    shifts_p, shifts_s, wkvs_p, wkvs_s = [], [], [], []
    v_first = None
    ctx = None
    kv_rows_p = kv_rows_s = win_p = win_s = None
    for i in range(depth):
        if i < n_a:
            scan_passes = 3 if i == 0 else 1
            xp_, xs_ = x[:n_p].reshape(bp, tp, d), x[n_p:n_tok].reshape(bs, ts, d)
            shifts_p.append(xp_[:, -1])
            shifts_s.append(xs_[:, -1])
            prev = join(jnp.concatenate([jnp.zeros((bp, 1, d), F32), xp_[:, :-1]], axis=1),
                        jnp.concatenate([state_shift[i][:, None, :], xs_[:, :-1]], axis=1))
            mix = rw_mix[i]
            lin = linear
            r = lin(x, rw_w_rkv, widx=(i, 0), mix=(prev, mix[0]))
            k = lin(x, rw_w_rkv, widx=(i, 1), mix=(prev, mix[2]))
            v = lin(x, rw_w_rkv, widx=(i, 2), mix=(prev, mix[3]))
            lw = lin(lin(x, rw_w1, widx=(i,), mix=(prev, mix[1]), post=jnp.tanh), rw_w2, widx=(i,), rows=(rw_w0[i],),
                     post=_post_logdecay)
            if i == 0:
                v_first = v
            else:
                v = lin(lin(x, rw_v1, widx=(i - 1,), mix=(prev, mix[3])), rw_v2, widx=(i - 1,), rows=(rw_v0[i - 1],),
                        fulls=(v, v_first), post=_post_vres)
            a = lin(lin(x, rw_a1, widx=(i,), mix=(prev, mix[4])), rw_a2, widx=(i,), rows=(rw_a0[i],),
                    post=_post_sig_bias)
            g = lin(lin(x, rw_g1, widx=(i,), mix=(prev, mix[5]), post=jax.nn.sigmoid), rw_g2, widx=(i,))
            proj = (r, lw, k, v, a, g)
            vecs = (rw_k_k[i], rw_k_a[i], rw_r_k[i].reshape(-1), rw_lnx_g[i], rw_lnx_b[i])
            y_p, s_p = wkv7(*proj, *vecs, jnp.zeros((bp, hh, RW_HEAD, RW_HEAD), F32), seq_len=tp,
                            passes=scan_passes)
            y_s, s_s = wkv7_seq(*[split_seq(z)[1] for z in proj], *vecs, state_wkv[i], steps=ts)
            wkvs_p.append(s_p)
            wkvs_s.append(s_s)
            h = lin(merge_seq(y_p, y_s), rw_w_o, widx=(i,))
        else:
            jn = i - n_a
            if ctx is None:
                rows = linear(layer_norm(x, kv_ln_g, kv_ln_b), w_kv)
                rows_p = rows[:n_p].reshape(bp, tp, 6 * gw)
                rows_s = rows[n_p:n_tok].reshape(bs, ts, 6 * gw)
                kv_rows_p = rows_p[:, :, :4 * gw].reshape(bp, tp, 4, N_KV, HEAD_DIM)
                kv_rows_s = rows_s[:, :, :4 * gw].reshape(bs, ts, 4, N_KV, HEAD_DIM)
                win_p = rows_p[:, max(tp - WINDOW, 0):, 4 * gw:].reshape(bp, -1, 2, N_KV, HEAD_DIM)
                new_win = rows_s[:, :, 4 * gw:].reshape(bs, ts, 2, N_KV, HEAD_DIM)
                win_s = jnp.concatenate([cache_win_kv, new_win], axis=1)[:, ts:]
                n_pg_p = tp // PAGE
                pages_p = rows[:n_p].reshape(bp * n_pg_p, PAGE, 6 * gw)
                table_p = jnp.arange(bp * n_pg_p, dtype=jnp.int32).reshape(bp, n_pg_p)
                pages_s = cache_kv_pages
                pages_w = cache_win_kv.reshape(bs * (win_buf // PAGE), PAGE, 2, N_KV, HEAD_DIM)

                def perm(c):
                    return jnp.concatenate([c[:, 0::2], c[:, 1::2]], axis=1)

                cmp_p = compress(pages_p, table_p, 0, cmp_pos, cmp_w1, cmp_w2)
                cmp_s = compress(pages_s, page_table.astype(jnp.int32), 0, cmp_pos, cmp_w1, cmp_w2)
                tail_sel = jnp.pad(rows_s[:, :, 2 * gw:4 * gw], ((0, 0), (0, PAGE - ts), (0, 0)))
                tail_win = jnp.pad(rows_s[:, :, 4 * gw:6 * gw], ((0, 0), (0, PAGE - ts), (0, 0)))
                tq_p = PAGE
                n_qt = tp // tq_p
                qi = jnp.arange(n_qt, dtype=jnp.int32)[:, None]
                kj = jnp.arange(n_pg_p, dtype=jnp.int32)[None, :]
                base = (jnp.arange(bp, dtype=jnp.int32) * n_pg_p)[:, None, None]
                tbl_sel_p = base + jnp.minimum(kj, qi)[None]
                kp0_sel_p = jnp.broadcast_to(jnp.where(kj <= qi, kj * PAGE, -1)[None], tbl_sel_p.shape)
                n_wt = WINDOW // PAGE + 1
                kw = qi - (n_wt - 1) + jnp.arange(n_wt, dtype=jnp.int32)[None, :]
                tbl_win_p = base + jnp.maximum(kw, 0)[None]
                kp0_win_p = jnp.broadcast_to(jnp.where(kw >= 0, kw * PAGE, -1)[None], tbl_win_p.shape)
                tbl_sel_s = page_table.astype(jnp.int32)[:, None, :]
                kp0_sel_s = jnp.broadcast_to((jnp.arange(n_pg_s, dtype=jnp.int32) * PAGE)[None, None], tbl_sel_s.shape)
                n_ww = win_buf // PAGE
                tbl_win_s = jnp.arange(bs * n_ww, dtype=jnp.int32).reshape(bs, 1, n_ww)
                kp0_win_s = jnp.broadcast_to(
                    (past_len - win_buf + jnp.arange(n_ww, dtype=jnp.int32) * PAGE)[None, None], tbl_win_s.shape)
                ctx = dict(kc_p=perm(cmp_p[:, 0]), vc_p=perm(cmp_p[:, 1]), kc_s=perm(cmp_s[:, 0]), vc_s=perm(cmp_s[:, 1]))
            dq = nsa_w_qg.shape[2] - nsa_b_g.shape[1]
            q = linear(x, nsa_w_qg, widx=(jn,), ncols=dq)
            gates = linear(x, nsa_w_qg[jn][:, dq:], rows=(nsa_b_g[jn],), post=_post_sig_bias)
            (q_p, q_s), (g_p, g_s) = split_seq(q), split_seq(gates)
            ns_p = -(-tp // SEL_BLOCK)
            ns_s = -(-(past_len + ts) // SEL_BLOCK)
            o_p, sel_p = nsa_cmp(q_p, ctx['kc_p'], ctx['vc_p'], g_p, q_offset=0, n_sel_blocks=ns_p, tq=tq_p)
            o_p = nsa_attn(q_p, pages_p, tbl_sel_p, kp0_sel_p, 2, 3, None, sel_p, g_p, o_p, mode="sel", q_offset=0,
                           tq=tq_p, tail_pos0=0, branch=1, ppt=_pick_tile(tp // PAGE, (4, 2, 1)))
            o_p = nsa_attn(q_p, pages_p, tbl_win_p, kp0_win_p, 4, 5, None, None, g_p, o_p, mode="win", q_offset=0,
                           tq=tq_p, tail_pos0=0, branch=2, ppt=WINDOW // PAGE + 1)
            o_s, sel_s = nsa_cmp(q_s, ctx['kc_s'], ctx['vc_s'], g_s, q_offset=past_len, n_sel_blocks=ns_s, tq=ts_pad)
            o_s = nsa_attn(q_s, pages_s, tbl_sel_s, kp0_sel_s, 2, 3, tail_sel, sel_s, g_s, o_s, mode="sel",
                           q_offset=past_len, tq=ts_pad, tail_pos0=past_len, branch=1,
                           ppt=_pick_tile(n_pg_s, (4, 2, 1)))
            o_s = nsa_attn(q_s, pages_w, tbl_win_s, kp0_win_s, 0, 1, tail_win, None, g_s, o_s, mode="win",
                           q_offset=past_len, tq=ts_pad, tail_pos0=past_len, branch=2,
                           ppt=_pick_tile(win_buf // PAGE, (4, 2, 1)))
            h = linear(merge_seq(o_p, o_s), nsa_w_o, widx=(jn,))
        x = layer_norm(x, ln_g[i, 0], ln_b[i, 0], h=h, alpha=alpha)
        mo = moe_ffn(x, n_tok, i, moe_router, moe_bias, moe_w_in, moe_w_out, moe_ws_in, moe_ws_out)
        x = layer_norm(x, ln_g[i, 1], ln_b[i, 1], h=mo, alpha=alpha)
        t = linear(pe, ple_proj, xidx=(i,), widx=(i,))
        x = linear(x, ple_gate, widx=(i,), rows=(ple_gate_b[i],), fulls=(x, t), post=_post_ple)

    y_p = x[:n_p].reshape(bp, tp, d)
    y_s = x[n_p:n_tok].reshape(bs, ts, d)
    return (y_p, y_s, kv_rows_p, kv_rows_s, win_p, win_s, jnp.stack(wkvs_p), jnp.stack(wkvs_s),
            jnp.stack(shifts_p), jnp.stack(shifts_s))
```

```python
import functools

import jax
import jax.numpy as jnp
from jax import lax
from jax.experimental import pallas as pl
from jax.experimental.pallas import tpu as pltpu

F32 = jnp.float32
BF16 = jnp.bfloat16

RW_HEAD = 64
GN_EPS = 64e-5
N_KV = 4
HEAD_DIM = 128
CMP_BLOCK = 32
SEL_BLOCK = 64
N_SEL = 16
WINDOW = 512
FORCED_BONUS = 100.0
N_GROUPS = 8
TOPK_GROUPS = 4
TOP_K = 8
ROUTED_SCALE = 2.5
LN_EPS = 1e-5
NEG = -1e30

LANES = 128
SUBLANES = 8
PAGE = 128
VMEM_LIMIT = 56 * 1024 * 1024
LINEAR_W_BYTES = 16 * 1024 * 1024
LINEAR_VMEM_BYTES = 46 * 1024 * 1024


def _cparams(n_axes):
    return pltpu.CompilerParams(dimension_semantics=("arbitrary",) * n_axes, vmem_limit_bytes=VMEM_LIMIT)


def _split(x):
    hi = x.astype(BF16)
    lo = (x - hi.astype(F32)).astype(BF16)
    return hi, lo


_NN = (((1,), (0,)), ((), ()))
_NT = (((1,), (1,)), ((), ()))
_TN = (((0,), (0,)), ((), ()))


def _dg(a, b, dims):
    return lax.dot_general(a, b, dims, preferred_element_type=F32)


def _dot(a, b, passes=1, dims=_NN):
    if passes == 1:
        return _dg(a.astype(BF16), b.astype(BF16), dims)
    ah, al = _split(a)
    bh, bl = _split(b)
    return _dg(ah, bh, dims) + (_dg(ah, bl, dims) + _dg(al, bh, dims))


def _round_up(x, m):
    return (x + m - 1) // m * m


def _pick_tile(n, prefs):
    for t in prefs:
        if n % t == 0:
            return t
    return n


def _linear_kernel(*refs, passes, has_mix, glu, n_row, n_full, post):
    it = iter(refs)
    x_ref = next(it)
    xp_ref = next(it) if has_mix else None
    mix_ref = next(it) if has_mix else None
    w_refs = [next(it) for _ in range(2 if glu else 1)]
    row_refs = [next(it) for _ in range(n_row)]
    full_refs = [next(it) for _ in range(n_full)]
    o_ref = next(it)
    scr = list(it)
    per = 2 if passes == 3 else 1

    @pl.when(pl.program_id(1) == 0)
    def _():
        for wi, w_ref in enumerate(w_refs):
            w = w_ref[...]
            if passes == 3:
                hi, lo = _split(w)
                scr[wi * per][...] = hi
                scr[wi * per + 1][...] = lo
            else:
                scr[wi * per][...] = w.astype(BF16)

    x = x_ref[...]
    if has_mix:
        x = x + (xp_ref[...] - x) * mix_ref[...]
    if passes == 3:
        xh, xl = _split(x)
    else:
        xh = x.astype(BF16)
    accs = []
    for wi in range(len(w_refs)):
        wh = scr[wi * per][...]
        acc = jnp.dot(xh, wh, preferred_element_type=F32)
        if passes == 3:
            wl = scr[wi * per + 1][...]
            acc = acc + (jnp.dot(xh, wl, preferred_element_type=F32) + jnp.dot(xl, wh, preferred_element_type=F32))
        accs.append(acc)
    extras = [r[...] for r in row_refs] + [r[...] for r in full_refs]
    o_ref[...] = post(*accs, *extras)


def _identity(acc):
    return acc


def linear(x, w, *, xidx=(), widx=(), ncols=None, passes=1, mix=None, glu=False, rows=(), fulls=(), post=_identity,
           full_rows=False, tm=None, tn=None):
    m, k = x.shape[-2:]
    n = (ncols or w.shape[-1]) // (2 if glu else 1)
    lead = (None,) * len(widx)
    n_w = 2 if glu else 1
    tn = tn or next((c for c in (n, n // 2, 512, 256, 128)
                     if n % c == 0 and (c % LANES == 0 or c == n) and n_w * k * c * 4 <= LINEAR_W_BYTES), n)
    w_bytes = n_w * k * tn * (4 + 2 * (2 if passes == 3 else 1))
    per_row = (2 if mix is not None else 1) * k * 4 * 2 + (1 + len(fulls)) * tn * 4 * 2
    tm = tm or next((c for c in (768, 512, 384, 256, 128, 64, 32, 16, 8)
                     if m % c == 0 and w_bytes + c * per_row <= LINEAR_VMEM_BYTES), m)
    nb = n // tn
    assert nb == 1 or not full_rows, "a row-wise epilogue needs the whole row in one block"
    grid = (nb, m // tm)
    x_spec = pl.BlockSpec((None,) * len(xidx) + (tm, k), lambda j, i: xidx + (i, 0))
    in_specs = [x_spec]
    args = [x]
    if mix is not None:
        in_specs += [x_spec, pl.BlockSpec((1, k), lambda j, i: (0, 0))]
        args += [mix[0], mix[1].reshape(1, k)]
    in_specs.append(pl.BlockSpec(lead + (k, tn), lambda j, i: widx + (0, j), pipeline_mode=pl.Buffered(1)))
    args.append(w)
    if glu:
        in_specs.append(pl.BlockSpec(lead + (k, tn), lambda j, i: widx + (0, j + nb), pipeline_mode=pl.Buffered(1)))
        args.append(w)
    for r in rows:
        in_specs.append(pl.BlockSpec((1, tn), lambda j, i: (0, j)))
        args.append(r.reshape(1, n))
    for f in fulls:
        in_specs.append(pl.BlockSpec((tm, tn), lambda j, i: (i, j)))
        args.append(f)
    scratch = [pltpu.VMEM((k, tn), BF16) for _ in range(n_w * (2 if passes == 3 else 1))]
    kern = functools.partial(_linear_kernel, passes=passes, has_mix=mix is not None, glu=glu,
                             n_row=len(rows), n_full=len(fulls), post=post)
    return pl.pallas_call(
        kern,
        grid=grid,
        in_specs=in_specs,
        out_specs=pl.BlockSpec((tm, tn), lambda j, i: (i, j)),
        out_shape=jax.ShapeDtypeStruct((m, n), F32),
        scratch_shapes=scratch,
        compiler_params=_cparams(2),
    )(*args)


def _resid_ln(h, g, b, x, alpha):
    z = alpha * x + h
    mu = jnp.mean(z, -1, keepdims=True)
    zc = z - mu
    var = jnp.mean(zc * zc, -1, keepdims=True)
    return zc * lax.rsqrt(var + LN_EPS) * g + b


def _ln_kernel(*refs, alpha, has_h):
    if has_h:
        x_ref, h_ref, g_ref, b_ref, o_ref = refs
        x = alpha * x_ref[...] + h_ref[...]
    else:
        x_ref, g_ref, b_ref, o_ref = refs
        x = x_ref[...]
    mu = jnp.mean(x, -1, keepdims=True)
    xc = x - mu
    var = jnp.mean(xc * xc, -1, keepdims=True)
    o_ref[...] = xc * lax.rsqrt(var + LN_EPS) * g_ref[...] + b_ref[...]


def layer_norm(x, g, b, h=None, alpha=1.0):
    m, d = x.shape
    tm = _pick_tile(m, (384, 256, 128, 64, 32, 16, 8))
    spec = pl.BlockSpec((tm, d), lambda i: (i, 0))
    vspec = pl.BlockSpec((1, d), lambda i: (0, 0))
    args = [x] + ([h] if h is not None else []) + [g.reshape(1, d), b.reshape(1, d)]
    in_specs = [spec] + ([spec] if h is not None else []) + [vspec, vspec]
    return pl.pallas_call(
        functools.partial(_ln_kernel, alpha=alpha, has_h=h is not None),
        grid=(m // tm,),
        in_specs=in_specs,
        out_specs=spec,
        out_shape=jax.ShapeDtypeStruct((m, d), F32),
        compiler_params=_cparams(1),
    )(*args)


def _bf16r(x):
    return x.astype(BF16).astype(F32)


def _matvec_bf16(s, vec_row):
    return jnp.sum(_bf16r(s) * _bf16r(vec_row), axis=1, keepdims=True)


def _row_from_col(col, eye):
    return jnp.sum(eye * col, axis=0, keepdims=True)


def _col_from_row(row, eye):
    return jnp.sum(eye * row, axis=1, keepdims=True)


def _wkv_out(y, v_h, rk_h, lg, lb):
    mu = jnp.mean(y, -1, keepdims=True)
    yc = y - mu
    var = jnp.mean(yc * yc, -1, keepdims=True)
    return yc * lax.rsqrt(var + GN_EPS) * lg + lb + jnp.sum(rk_h, -1, keepdims=True) * v_h


def _wkv_seq_kernel(r_ref, lw_ref, k_ref, v_ref, a_ref, g_ref, kk_ref, ka_ref, rk_ref, lg_ref, lb_ref, s0_ref,
                    y_ref, s_ref, *, steps, heads):
    N = RW_HEAD
    r = r_ref[0]
    k = k_ref[0]
    v = v_ref[0]
    a = a_ref[0]
    w = jnp.exp(lw_ref[0])
    kkw = k * kk_ref[...]
    k2 = k * (1.0 + (a - 1.0) * ka_ref[...])
    rk = r * k2 * rk_ref[...]
    eye = (lax.broadcasted_iota(jnp.int32, (N, N), 0) == lax.broadcasted_iota(jnp.int32, (N, N), 1)).astype(F32)
    tpad = r.shape[0]
    H = range(heads)
    sls = [slice(h * N, (h + 1) * N) for h in H]
    kk = [kkw[:, sl] for sl in sls]
    kk = [z * lax.rsqrt(jnp.maximum(jnp.sum(z * z, -1, keepdims=True), 1e-24)) for z in kk]
    b_h = [kk[h] * a[:, sls[h]] for h in H]
    s = [s0_ref[0, h] for h in H]
    ys = [[] for _ in H]
    for t in range(steps):
        sa = [_matvec_bf16(s[h], -kk[h][t:t + 1]) for h in H]
        vcol = [_col_from_row(v[t:t + 1, sls[h]], eye) for h in H]
        s = [s[h] * w[t:t + 1, sls[h]] + sa[h] * b_h[h][t:t + 1] + vcol[h] * k2[t:t + 1, sls[h]] for h in H]
        yt = [_row_from_col(_matvec_bf16(s[h], r[t:t + 1, sls[h]]), eye) for h in H]
        for h in H:
            ys[h].append(yt[h])
    for h in H:
        s_ref[0, h] = s[h]
    y = [jnp.concatenate(ys[h] + [jnp.zeros((tpad - steps, N), F32)], axis=0) for h in H]
    outs = [_wkv_out(y[h], v[:, sls[h]], rk[:, sls[h]], lg_ref[:, sls[h]], lb_ref[:, sls[h]]) for h in H]
    y_ref[0] = jnp.concatenate(outs, axis=1) * g_ref[0]


def wkv7_seq(r, lw, k, v, a, g, k_k, k_a, r_k, lnx_g, lnx_b, s0, *, steps):
    bsz, t, d = r.shape
    n = RW_HEAD
    hh = d // n
    heads = _pick_tile(hh, (4, 2))
    wide = heads * n
    seq = pl.BlockSpec((1, t, wide), lambda b, hb: (b, 0, hb))
    vec = pl.BlockSpec((1, wide), lambda b, hb: (0, hb))
    st = pl.BlockSpec((1, heads, n, n), lambda b, hb: (b, hb, 0, 0))
    vecs = [z.reshape(1, d) for z in (k_k, k_a, r_k, lnx_g, lnx_b)]
    return pl.pallas_call(
        functools.partial(_wkv_seq_kernel, steps=steps, heads=heads),
        grid=(bsz, hh // heads),
        in_specs=[seq] * 6 + [vec] * 5 + [st],
        out_specs=[seq, st],
        out_shape=[jax.ShapeDtypeStruct((bsz, t, d), F32), jax.ShapeDtypeStruct((bsz, hh, n, n), F32)],
        compiler_params=_cparams(2),
    )(r, lw, k, v, a, g, *vecs, s0)


def _wkv_kernel(r_ref, lw_ref, k_ref, v_ref, a_ref, g_ref, kk_ref, ka_ref, rk_ref, lg_ref, lb_ref, s0_ref,
                y_ref, s_ref, *, chunk, heads, passes):
    L, N = chunk, RW_HEAD

    @pl.when(pl.program_id(2) == 0)
    def _():
        s_ref[...] = s0_ref[...]

    r = r_ref[...]
    lw = lw_ref[...]
    k = k_ref[...]
    v = v_ref[...]
    a = a_ref[...]
    kkw = k * kk_ref[...]
    k2 = k * (1.0 + (a - 1.0) * ka_ref[...])
    rk = r * k2 * rk_ref[...]

    row = lax.broadcasted_iota(jnp.int32, (L, L), 0)
    col = lax.broadcasted_iota(jnp.int32, (L, L), 1)
    tri_incl = (col <= row)
    tri_strict = (col < row)
    tri = tri_incl.astype(BF16)
    l1 = lw.astype(BF16)
    rem = lw - l1.astype(F32)
    l2 = rem.astype(BF16)
    l3 = (rem - l2.astype(F32)).astype(BF16)
    cs = (jnp.dot(tri, l1, preferred_element_type=F32) + jnp.dot(tri, l2, preferred_element_type=F32)
          + jnp.dot(tri, l3, preferred_element_type=F32))
    cs_last = cs[L - 1:L, :]
    e_prev = jnp.exp(cs - lw)
    e_inv = jnp.exp(-cs)
    e_fwd = jnp.exp(cs)
    e_tail = jnp.exp(cs_last - cs)
    gam = jnp.exp(cs_last)

    n_dbl = max(1, (L - 1).bit_length())
    eye = (lax.broadcasted_iota(jnp.int32, (N, N), 0) == lax.broadcasted_iota(jnp.int32, (N, N), 1)).astype(F32)
    H = range(heads)
    sls = [slice(h * N, (h + 1) * N) for h in H]
    kk = [kkw[:, sl] for sl in sls]
    kk = [z * lax.rsqrt(jnp.maximum(jnp.sum(z * z, -1, keepdims=True), 1e-24)) for z in kk]
    b_h = [kk[h] * a[:, sls[h]] for h in H]
    k_h = [k2[:, sl] for sl in sls]
    v_h = [v[:, sl] for sl in sls]
    lhs2 = [jnp.concatenate([-kk[h] * e_prev[:, sls[h]], r[:, sls[h]] * e_fwd[:, sls[h]]], axis=0) for h in H]
    rhs2 = [jnp.concatenate([b_h[h] * e_inv[:, sls[h]], k_h[h] * e_inv[:, sls[h]]], axis=0) for h in H]
    bk = [jnp.concatenate([b_h[h] * e_tail[:, sls[h]], k_h[h] * e_tail[:, sls[h]]], axis=0) for h in H]
    s0 = [s_ref[0, h] for h in H]
    p = [_dot(lhs2[h], rhs2[h], passes, _NT) for h in H]
    q = [_dot(lhs2[h], s0[h], passes, _NT) for h in H]
    pk = [jnp.where(tri_strict, p[h][:L, :L], 0.0) for h in H]
    a_k = [jnp.where(tri_strict, p[h][:L, L:], 0.0) for h in H]
    rbk = [jnp.concatenate([jnp.where(tri_incl, p[h][L:, :L], 0.0), jnp.where(tri_incl, p[h][L:, L:], 0.0)], axis=1)
           for h in H]
    x = [q[h][:L] + _dot(a_k[h], v_h[h], passes) for h in H]
    for d in range(n_dbl):
        x = [x[h] + _dot(pk[h], x[h], passes) for h in H]
        if d + 1 < n_dbl:
            pk = [_dot(pk[h], pk[h], passes) for h in H]
    uv = [jnp.concatenate([x[h], v_h[h]], axis=0) for h in H]
    y = [q[h][L:] + _dot(rbk[h], uv[h], passes) for h in H]
    s_new = [s0[h] * gam[:, sls[h]] + _dot(uv[h], bk[h], passes, _TN) for h in H]
    for h in H:
        s_ref[0, h] = s_new[h]
    last = lax.broadcasted_iota(jnp.int32, (L, 1), 0) == L - 1
    y = [jnp.where(last, _row_from_col(_matvec_bf16(s_new[h], r[L - 1:L, sls[h]]), eye), y[h]) for h in H]
    outs = [_wkv_out(y[h], v_h[h], rk[:, sls[h]], lg_ref[:, sls[h]], lb_ref[:, sls[h]]) for h in H]
    y_ref[...] = jnp.concatenate(outs, axis=1) * g_ref[...]


def wkv7(r, lw, k, v, a, g, k_k, k_a, r_k, lnx_g, lnx_b, s0, *, seq_len, passes):
    bsz, t = s0.shape[0], seq_len
    d = r.shape[1]
    n = RW_HEAD
    hh = d // n
    heads = _pick_tile(hh, (16, 4, 2))
    chunk = _pick_tile(t, (128, 64))
    wide = heads * n
    n_chunks = t // chunk
    seq = pl.BlockSpec((chunk, wide), lambda b, hb, c: (b * n_chunks + c, hb))
    vec = pl.BlockSpec((1, wide), lambda b, hb, c: (0, hb))
    st = pl.BlockSpec((1, heads, n, n), lambda b, hb, c: (b, hb, 0, 0))
    vecs = [z.reshape(1, d) for z in (k_k, k_a, r_k, lnx_g, lnx_b)]
    return pl.pallas_call(
        functools.partial(_wkv_kernel, chunk=chunk, heads=heads, passes=passes),
        grid=(bsz, hh // heads, n_chunks),
        in_specs=[seq] * 6 + [vec] * 5 + [st],
        out_specs=[seq, st],
        out_shape=[jax.ShapeDtypeStruct((bsz * t, d), F32), jax.ShapeDtypeStruct((bsz, hh, n, n), F32)],
        compiler_params=_cparams(3),
    )(r, lw, k, v, a, g, *vecs, s0)


def _route_kernel(x_ref, rt_ref, bias_ref, eidx_ref, w_ref, pos_ref, cnt_ref, *, n_valid):
    ne = rt_ref.shape[0]
    per = ne // N_GROUPS
    logits = _dot(rt_ref[...], x_ref[...], 1, _NT)
    s = jax.nn.sigmoid(logits)
    sb = s + bias_ref[...]
    tm = sb.shape[1]
    sub = lax.broadcasted_iota(jnp.int32, (per, tm), 0)
    gs_rows = []
    for gi in range(N_GROUPS):
        tile = sb[gi * per:(gi + 1) * per, :]
        m1 = jnp.max(tile, axis=0, keepdims=True)
        first = jnp.min(jnp.where(tile == m1, sub, per), axis=0, keepdims=True)
        m2 = jnp.max(jnp.where(sub == first, -jnp.inf, tile), axis=0, keepdims=True)
        gs_rows.append(m1 + m2)
    keep_rows = []
    for gi in range(N_GROUPS):
        cnt = jnp.zeros((1, tm), jnp.int32)
        for gj in range(N_GROUPS):
            if gj == gi:
                continue
            beats = (gs_rows[gj] > gs_rows[gi]) if gj > gi else (gs_rows[gj] >= gs_rows[gi])
            cnt = cnt + beats.astype(jnp.int32)
        keep_rows.append(jnp.broadcast_to(cnt, (per, tm)))
    masked = jnp.where(jnp.concatenate(keep_rows, axis=0) < TOPK_GROUPS, sb, -1e9)
    eidx = lax.broadcasted_iota(jnp.int32, (ne, tm), 0)
    rank = jnp.zeros((ne, tm), jnp.int32)
    for ej in range(ne):
        rowv = masked[ej:ej + 1, :]
        rank = rank + jnp.where(eidx > ej, (rowv >= masked).astype(jnp.int32), (rowv > masked).astype(jnp.int32))
    tok = pl.program_id(0) * tm + lax.broadcasted_iota(jnp.int32, (ne, tm), 1)
    rank = jnp.where(tok < n_valid, rank, ne)
    sel = rank < TOP_K
    self32 = sel.astype(F32)
    den = jnp.sum(jnp.where(sel, s, 0.0), axis=0, keepdims=True)
    wgt = jnp.where(sel, s / den * ROUTED_SCALE, 0.0)

    @pl.when(pl.program_id(0) == 0)
    def _():
        cnt_ref[...] = jnp.zeros(cnt_ref.shape, F32)

    upper = (lax.broadcasted_iota(jnp.int32, (tm, tm), 0) < lax.broadcasted_iota(jnp.int32, (tm, tm), 1)).astype(BF16)
    carry = cnt_ref[...]
    pos = jnp.dot(self32.astype(BF16), upper, preferred_element_type=F32) + carry[:, :1]
    cnt_ref[...] = carry + jnp.sum(self32, axis=1, keepdims=True)
    eidx_f = eidx.astype(F32)
    rows_e, rows_w, rows_p = [], [], []
    for kk in range(TOP_K):
        hit = rank == kk
        rows_e.append(jnp.sum(jnp.where(hit, eidx_f, 0.0), axis=0, keepdims=True))
        rows_w.append(jnp.sum(jnp.where(hit, wgt, 0.0), axis=0, keepdims=True))
        rows_p.append(jnp.sum(jnp.where(hit, pos, 0.0), axis=0, keepdims=True))
    eidx_ref[...] = jnp.concatenate(rows_e, axis=0).astype(jnp.int32)
    w_ref[...] = jnp.concatenate(rows_w, axis=0)
    pos_ref[...] = jnp.concatenate(rows_p, axis=0).astype(jnp.int32)


def route(x, router_t, bias, n_valid):
    m, d = x.shape
    ne = router_t.shape[0]
    tm = _pick_tile(m, (256, 128))
    out = pl.BlockSpec((TOP_K, tm), lambda i: (0, i))
    return pl.pallas_call(
        functools.partial(_route_kernel, n_valid=n_valid),
        grid=(m // tm,),
        in_specs=[pl.BlockSpec((tm, d), lambda i: (i, 0)), pl.BlockSpec((ne, d), lambda i: (0, 0)),
                  pl.BlockSpec((ne, 1), lambda i: (0, 0))],
        out_specs=[out, out, out, pl.BlockSpec((ne, LANES), lambda i: (0, 0))],
        out_shape=[jax.ShapeDtypeStruct((TOP_K, m), jnp.int32), jax.ShapeDtypeStruct((TOP_K, m), F32),
                   jax.ShapeDtypeStruct((TOP_K, m), jnp.int32), jax.ShapeDtypeStruct((ne, LANES), F32)],
        compiler_params=_cparams(1),
    )(x, router_t, bias.reshape(ne, 1))


def _experts_kernel(be_ref, nu_ref, xs_ref, win_ref, wout_ref, o_ref, win_s, wout_s):
    i = pl.program_id(0)
    f = wout_ref.shape[0]

    @pl.when(i < nu_ref[0])
    def _():
        changed = jnp.logical_or(i == 0, be_ref[i] != be_ref[jnp.maximum(i - 1, 0)])

        @pl.when(changed)
        def _():
            win_s[...] = win_ref[...].astype(BF16)
            wout_s[...] = wout_ref[...].astype(BF16)

        hu = jnp.dot(xs_ref[...].astype(BF16), win_s[...], preferred_element_type=F32)
        hid = jax.nn.silu(hu[:, :f]) * hu[:, f:]
        y = jnp.dot(hid.astype(BF16), wout_s[...], preferred_element_type=F32)
        o_ref[...] = y.astype(BF16)

    @pl.when(i >= nu_ref[0])
    def _():
        o_ref[...] = jnp.zeros(o_ref.shape, BF16)


def experts(xs, blk_expert, n_used, w_in, w_out, layer, tm):
    rws, d = xs.shape
    f2 = w_in.shape[3]
    f = f2 // 2
    grid_spec = pltpu.PrefetchScalarGridSpec(
        num_scalar_prefetch=2,
        grid=(rws // tm,),
        in_specs=[
            pl.BlockSpec((tm, d), lambda i, be, nu: (jnp.minimum(i, nu[0] - 1), 0)),
            pl.BlockSpec((None, None, d, f2), lambda i, be, nu: (layer, be[i], 0, 0)),
            pl.BlockSpec((None, None, f, d), lambda i, be, nu: (layer, be[i], 0, 0)),
        ],
        out_specs=pl.BlockSpec((tm, d), lambda i, be, nu: (i, 0)),
        scratch_shapes=[pltpu.VMEM((d, f2), BF16), pltpu.VMEM((f, d), BF16)],
    )
    return pl.pallas_call(
        _experts_kernel,
        grid_spec=grid_spec,
        out_shape=jax.ShapeDtypeStruct((rws, d), BF16),
        compiler_params=_cparams(1),
    )(blk_expert, n_used, xs, w_in, w_out)


def moe_ffn(x, n_valid, layer, router, bias, w_in, w_out, ws_in, ws_out, ln_g, ln_b, alpha):
    m, d = x.shape
    ne = router.shape[2]
    eidx, ew, posk, cnt = route(x, router[layer].T, bias[layer], n_valid)
    tm = 256 if m * TOP_K // ne >= 256 else _round_up(max(m * TOP_K // ne, SUBLANES), SUBLANES)
    nk = m * TOP_K
    counts = cnt[:, 0].astype(jnp.int32)
    padded = (counts + tm - 1) // tm * tm
    pad_end = jnp.cumsum(padded)
    pad_start = pad_end - padded
    n_blocks = (nk + tm - 1) // tm + ne
    n_rows = n_blocks * tm
    tok = jnp.arange(m, dtype=jnp.int32)
    first_row = jnp.sum(jnp.where(eidx[:, :, None] == jnp.arange(ne, dtype=jnp.int32), pad_start, 0), axis=-1)
    dest = jnp.where(tok[None, :] < n_valid, first_row + posk, n_rows - 1).astype(jnp.int32)
    token_of_row = jnp.zeros((n_rows,), jnp.int32).at[dest.reshape(-1)].set(jnp.tile(tok, TOP_K))
    blk_start = jnp.arange(n_blocks, dtype=jnp.int32) * tm
    blk_expert = jnp.minimum(jnp.sum((pad_end[None, :] <= blk_start[:, None]).astype(jnp.int32), axis=1), ne - 1)
    n_used = (pad_end[-1:] // tm).astype(jnp.int32)
    xs = jnp.take(x, token_of_row, axis=0, mode="clip")
    yb = experts(xs, blk_expert, n_used, w_in, w_out, layer, tm)
    wr = ew.astype(BF16).astype(F32)
    picked = yb.at[dest.T].get(mode="promise_in_bounds").astype(F32)
    routed = jnp.sum(picked * wr.T[:, :, None], axis=1)
    hid = linear(x, ws_in, widx=(layer,), glu=True, post=lambda a, b: jax.nn.silu(a) * b)
    return linear(hid, ws_out, widx=(layer,), rows=(ln_g, ln_b), fulls=(routed, x), full_rows=True,
                  post=lambda acc, g, b, rt, xr: _resid_ln(acc + rt, g, b, xr, alpha))


def _gelu_tanh(x):
    return 0.5 * x * (1.0 + jnp.tanh(0.7978845608028654 * (x + 0.044715 * x * x * x)))


def _group_tile(ref, g):
    if len(ref.shape) == 3:
        return ref[:, g, :]
    return ref[:, g * HEAD_DIM:(g + 1) * HEAD_DIM]


def _page_spec(pages, index_fn, slot_fn):
    if pages.ndim == 5:
        return pl.BlockSpec((None, PAGE, None, N_KV, HEAD_DIM), lambda *a: (index_fn(*a), 0, slot_fn(*a), 0, 0))
    return pl.BlockSpec((None, PAGE, N_KV * HEAD_DIM), lambda *a: (index_fn(*a), 0, slot_fn(*a)))


def _compress_kernel(tbl_ref, *refs, group, ppt):
    pg_refs = refs[:ppt]
    pos_ref, w1_ref, w2_ref, o_ref, w1_s = refs[ppt:ppt + 5]
    slabs = refs[ppt + 5:]
    p = pl.program_id(2)
    pin = p % (group // ppt)

    @pl.when(p == 0)
    def _():
        w1_s[...] = w1_ref[0].astype(BF16)

    for pi, pg_ref in enumerate(pg_refs):
        for g in range(N_KV):
            slabs[g][pl.ds(pl.multiple_of((pin * ppt + pi) * PAGE, PAGE), PAGE), :] = _group_tile(pg_ref, g)

    @pl.when(pin == group // ppt - 1)
    def _():
        per_page = PAGE // CMP_BLOCK
        nrow = group * per_page
        hid = w1_ref.shape[2]
        acc = jnp.zeros((N_KV * nrow, hid), F32)
        for t in range(CMP_BLOCK):
            lhs = jnp.concatenate([slabs[g][pl.ds(t, nrow, stride=CMP_BLOCK), :] for g in range(N_KV)], axis=0)
            lhs = (lhs + pos_ref[0, t:t + 1, :]).astype(BF16)
            acc = acc + jnp.dot(lhs, w1_s[t * HEAD_DIM:(t + 1) * HEAD_DIM, :], preferred_element_type=F32)
        out = _dot(_gelu_tanh(acc), w2_ref[0])
        for g in range(N_KV):
            o_ref[0, 0, :, g * HEAD_DIM:(g + 1) * HEAD_DIM] = out[g * nrow:(g + 1) * nrow]


def compress(pages, table, slot0, cmp_pos, cmp_w1, cmp_w2):
    bsz, n_pg = table.shape
    group = min(32, n_pg)
    ppt = _pick_tile(group, (4, 2, 1))
    gw = N_KV * HEAD_DIM
    per_page = PAGE // CMP_BLOCK
    hid = cmp_w1.shape[2]

    def page_of(pi):
        return lambda b, s, p, tbl: tbl[b * n_pg + p * ppt + pi]

    grid_spec = pltpu.PrefetchScalarGridSpec(
        num_scalar_prefetch=1,
        grid=(bsz, 2, n_pg // ppt),
        in_specs=[_page_spec(pages, page_of(pi), lambda b, s, p, tbl: slot0 + s) for pi in range(ppt)] + [
            pl.BlockSpec((1, CMP_BLOCK, HEAD_DIM), lambda b, s, p, tbl: (s, 0, 0)),
            pl.BlockSpec((1, CMP_BLOCK * HEAD_DIM, hid), lambda b, s, p, tbl: (s, 0, 0)),
            pl.BlockSpec((1, hid, HEAD_DIM), lambda b, s, p, tbl: (s, 0, 0)),
        ],
        out_specs=pl.BlockSpec((1, 1, group * per_page, gw), lambda b, s, p, tbl: (b, s, p // (group // ppt), 0)),
        scratch_shapes=[pltpu.VMEM((CMP_BLOCK * HEAD_DIM, hid), BF16)]
        + [pltpu.VMEM((group * PAGE, HEAD_DIM), F32) for _ in range(N_KV)],
    )
    return pl.pallas_call(
        functools.partial(_compress_kernel, group=group, ppt=ppt),
        grid_spec=grid_spec,
        out_shape=jax.ShapeDtypeStruct((bsz, 2, n_pg * per_page, gw), F32),
        compiler_params=_cparams(3),
    )(table.reshape(-1), *([pages] * ppt), cmp_pos, cmp_w1, cmp_w2)


def _nsa_cmp_kernel(q_ref, kc_ref, vc_ref, gate_ref, o_ref, sel_ref, score_s, rank_s, *, q_offset, n_sel_blocks,
                    tq, half):
    qt = pl.program_id(1)
    ncp = kc_ref.shape[1]
    nsp = sel_ref.shape[2]
    scale = HEAD_DIM ** -0.5
    n_heads = q_ref.shape[2] // HEAD_DIM
    rpg = n_heads // N_KV
    qpos = q_offset + qt * tq + lax.broadcasted_iota(jnp.int32, (1, tq), 1)
    cidx = lax.broadcasted_iota(jnp.int32, (ncp, 1), 0)
    corig = jnp.where(cidx < half, 2 * cidx, 2 * (cidx - half) + 1)
    cmask = ((corig + 1) * CMP_BLOCK - 1) <= qpos
    blk = lax.broadcasted_iota(jnp.int32, (nsp, 1), 0)
    cur = qpos // SEL_BLOCK
    forced = (blk == 0) | (blk == cur) | (blk == cur - 1)
    valid = (blk * SEL_BLOCK) <= qpos
    gate = gate_ref[0]
    HD = range(n_heads)
    kc = [kc_ref[0, :, g * HEAD_DIM:(g + 1) * HEAD_DIM] for g in range(N_KV)]
    vc = [vc_ref[0, :, g * HEAD_DIM:(g + 1) * HEAD_DIM] for g in range(N_KV)]
    s = [_dot(kc[hd // rpg], q_ref[0, :, hd * HEAD_DIM:(hd + 1) * HEAD_DIM], 1, _NT) * scale for hd in HD]
    s = [jnp.where(cmask, z, NEG) for z in s]
    e = [jnp.where(cmask, jnp.exp(z - jnp.max(z, axis=0, keepdims=True)), 0.0) for z in s]
    p = [z / jnp.maximum(jnp.sum(z, axis=0, keepdims=True), 1e-30) for z in e]
    oc = [_dot(p[hd], vc[hd // rpg], 1, _TN) for hd in HD]
    for hd in HD:
        o_ref[0, :, hd * HEAD_DIM:(hd + 1) * HEAD_DIM] = gate[:, hd * 3:hd * 3 + 1] * oc[hd]
    scores = []
    for g in range(N_KV):
        imp = p[g * rpg]
        for h in range(1, rpg):
            imp = imp + p[g * rpg + h]
        imp_blk = imp[:half] + imp[half:]
        if nsp > half:
            imp_blk = jnp.concatenate([imp_blk, jnp.zeros((nsp - half, tq), F32)], axis=0)
        else:
            imp_blk = imp_blk[:nsp]
        sc = jnp.where(valid, imp_blk + FORCED_BONUS * forced.astype(F32), -1.0)
        scores.append(jnp.where(blk < n_sel_blocks, sc, -2.0))
    score = jnp.concatenate(scores, axis=1)
    score_s[...] = score
    rank_s[...] = jnp.zeros(rank_s.shape, jnp.int32)

    def body(j, carry):
        rowv = score_s[pl.ds(j, 1), :]
        rank_s[...] = rank_s[...] + jnp.where(blk > j, (rowv >= score).astype(jnp.int32),
                                              (rowv > score).astype(jnp.int32))
        return carry

    lax.fori_loop(0, n_sel_blocks, body, 0)
    sel = (rank_s[...] < N_SEL).astype(F32)
    for g in range(N_KV):
        sel_ref[0, g] = sel[:, g * tq:(g + 1) * tq]


def nsa_cmp(q, kc, vc, gates, *, q_offset, n_sel_blocks, tq):
    bsz, t, dq = q.shape
    ncp = kc.shape[1]
    half = ncp // 2
    nsp = _round_up(n_sel_blocks, SUBLANES)
    gw = N_KV * HEAD_DIM
    qspec = pl.BlockSpec((1, tq, dq), lambda b, i: (b, i, 0))
    cspec = pl.BlockSpec((1, ncp, gw), lambda b, i: (b, 0, 0))
    return pl.pallas_call(
        functools.partial(_nsa_cmp_kernel, q_offset=q_offset, n_sel_blocks=n_sel_blocks, tq=tq, half=half),
        grid=(bsz, t // tq),
        in_specs=[qspec, cspec, cspec, pl.BlockSpec((1, tq, gates.shape[2]), lambda b, i: (b, i, 0))],
        out_specs=[qspec, pl.BlockSpec((1, N_KV, nsp, tq), lambda b, i: (b, 0, 0, i))],
        out_shape=[jax.ShapeDtypeStruct((bsz, t, dq), F32), jax.ShapeDtypeStruct((bsz, N_KV, nsp, t), F32)],
        scratch_shapes=[pltpu.VMEM((nsp, N_KV * tq), F32), pltpu.VMEM((nsp, N_KV * tq), jnp.int32)],
        compiler_params=_cparams(2),
    )(q, kc, vc, gates)


def _nsa_attn_kernel(*refs, mode, q_offset, tq, n_kt, ppt, has_tail, tail_pos0, branch):
    tbl_ref, kp0_ref = refs[0], refs[1]
    it = iter(refs[2:])
    q_ref = next(it)
    k_refs = [next(it) for _ in range(ppt)]
    v_refs = [next(it) for _ in range(ppt)]
    tk_ref = next(it) if has_tail else None
    tv_ref = next(it) if has_tail else None
    sel_ref = next(it) if mode == "sel" else None
    gate_ref = next(it)
    prev_ref = next(it)
    o_ref = next(it)
    m_s, l_s, acc_s = next(it), next(it), next(it)

    qt = pl.program_id(1)
    j = pl.program_id(2)
    n_heads = q_ref.shape[2] // HEAD_DIM
    rpg = n_heads // N_KV
    scale = HEAD_DIM ** -0.5
    width = WINDOW if mode == "win" else (1 << 30)
    n_steps = n_kt // ppt
    G = range(N_KV)

    @pl.when(j == 0)
    def _():
        m_s[...] = jnp.full(m_s.shape, NEG, F32)
        l_s[...] = jnp.zeros(l_s.shape, F32)
        acc_s[...] = jnp.zeros(acc_s.shape, F32)

    q_first = q_offset + qt * tq
    qpos = q_first + lax.broadcasted_iota(jnp.int32, (tq, 1), 0)

    def tile(kt_refs, vt_refs, kp0s):
        lane = lax.broadcasted_iota(jnp.int32, (1, PAGE), 1)
        kpos = jnp.concatenate([jnp.where(kp >= 0, kp, 1 << 30) + lane for kp in kp0s], axis=1)
        dist = qpos - kpos
        band = (dist >= 0) & (dist < width)
        if mode == "sel":
            nsp = sel_ref.shape[2]
            blk = lax.broadcasted_iota(jnp.int32, (nsp, 1), 0)
            expand = (blk == kpos // SEL_BLOCK).astype(BF16)
            hits = [_dg(sel_ref[0, g].astype(BF16), expand, _TN) for g in G]
            masks = [band & (hits[g] > 0.5) for g in G]
        else:
            masks = [band for g in G]
        masks = [jnp.concatenate([mk] * rpg, axis=0) for mk in masks]
        qg = [jnp.concatenate([q_ref[0, :, (g * rpg + h) * HEAD_DIM:(g * rpg + h + 1) * HEAD_DIM] for h in range(rpg)],
                              axis=0) for g in G]
        kg = [jnp.concatenate([_group_tile(r, g) for r in kt_refs], axis=0) for g in G]
        vg = [jnp.concatenate([_group_tile(r, g) for r in vt_refs], axis=0) for g in G]
        s = [_dot(qg[g], kg[g], 1, _NT) * scale for g in G]
        s = [jnp.where(masks[g], s[g], NEG) for g in G]
        m_old = [m_s[g] for g in G]
        m_new = [jnp.maximum(m_old[g], jnp.max(s[g], -1, keepdims=True)) for g in G]
        alpha = [jnp.exp(m_old[g] - m_new[g]) for g in G]
        p = [jnp.where(masks[g], jnp.exp(s[g] - m_new[g]), 0.0) for g in G]
        pv = [_dot(p[g], vg[g], 1) for g in G]
        for g in G:
            l_s[g] = alpha[g] * l_s[g] + jnp.sum(p[g], -1, keepdims=True)
            acc_s[g] = alpha[g] * acc_s[g] + pv[g]
            m_s[g] = m_new[g]

    base = (pl.program_id(0) * pl.num_programs(1) + qt) * n_kt + jnp.minimum(j, n_steps - 1) * ppt
    kp0s = [kp0_ref[base + pi] for pi in range(ppt)]
    live = None
    for kp in kp0s:
        ok = (kp >= 0) & (kp <= q_first + tq - 1) & (kp + PAGE - 1 > q_first - width)
        live = ok if live is None else (live | ok)
    if has_tail:
        live = live & (j < n_steps)

    @pl.when(live)
    def _():
        tile(k_refs, v_refs, kp0s)

    if has_tail:
        @pl.when(j == n_steps)
        def _():
            tile([tk_ref], [tv_ref], [tail_pos0])

    @pl.when(j == pl.num_programs(2) - 1)
    def _():
        gate = gate_ref[0]
        for g in range(N_KV):
            o = acc_s[g] / jnp.maximum(l_s[g], 1e-30)
            for h in range(rpg):
                hd = g * rpg + h
                cs = slice(hd * HEAD_DIM, (hd + 1) * HEAD_DIM)
                o_ref[0, :, cs] = prev_ref[0, :, cs] + gate[:, hd * 3 + branch:hd * 3 + branch + 1] * o[h * tq:(h + 1) * tq]


def nsa_attn(q, pages, tbl, kp0, slot_k, slot_v, tail, sel, gates, prev, *, mode, q_offset, tq, tail_pos0, branch,
             ppt=1):
    bsz, t, dq = q.shape
    n_qt = t // tq
    n_kt = tbl.shape[2]
    assert n_kt % ppt == 0
    gw = N_KV * HEAD_DIM
    n_heads = dq // HEAD_DIM
    rows = (n_heads // N_KV) * tq
    has_tail = tail is not None
    n_steps = n_kt // ppt + (1 if has_tail else 0)

    def page_idx(pi):
        return lambda b, i, j, tb, kp: tb[(b * n_qt + i) * n_kt + jnp.minimum(j, n_kt // ppt - 1) * ppt + pi]

    qspec = pl.BlockSpec((1, tq, dq), lambda b, i, j, tb, kp: (b, i, 0))
    in_specs = [qspec]
    in_specs += [_page_spec(pages, page_idx(pi), lambda *a: slot_k) for pi in range(ppt)]
    in_specs += [_page_spec(pages, page_idx(pi), lambda *a: slot_v) for pi in range(ppt)]
    args = [q] + [pages] * (2 * ppt)
    if has_tail:
        in_specs += [pl.BlockSpec((None, PAGE, gw), lambda b, i, j, tb, kp: (b, 0, 0)),
                     pl.BlockSpec((None, PAGE, gw), lambda b, i, j, tb, kp: (b, 0, 1))]
        args += [tail, tail]
    if mode == "sel":
        nsp = sel.shape[2]
        in_specs.append(pl.BlockSpec((1, N_KV, nsp, tq), lambda b, i, j, tb, kp: (b, 0, 0, i)))
        args.append(sel)
    in_specs += [pl.BlockSpec((1, tq, gates.shape[2]), lambda b, i, j, tb, kp: (b, i, 0)), qspec]
    args += [gates, prev]
    grid_spec = pltpu.PrefetchScalarGridSpec(
        num_scalar_prefetch=2,
        grid=(bsz, n_qt, n_steps),
        in_specs=in_specs,
        out_specs=qspec,
        scratch_shapes=[pltpu.VMEM((N_KV, rows, 1), F32), pltpu.VMEM((N_KV, rows, 1), F32),
                        pltpu.VMEM((N_KV, rows, HEAD_DIM), F32)],
    )
    return pl.pallas_call(
        functools.partial(_nsa_attn_kernel, mode=mode, q_offset=q_offset, tq=tq, n_kt=n_kt, ppt=ppt,
                          has_tail=has_tail, tail_pos0=tail_pos0, branch=branch),
        grid_spec=grid_spec,
        out_shape=jax.ShapeDtypeStruct((bsz, t, dq), F32),
        compiler_params=_cparams(3),
    )(tbl.reshape(-1).astype(jnp.int32), kp0.reshape(-1).astype(jnp.int32), *args)


def _softplus(z):
    return jnp.maximum(z, 0.0) + jnp.log(1.0 + jnp.exp(-jnp.abs(z)))


def _post_logdecay(acc, w0):
    return -jnp.exp(-_softplus(-(w0 + acc)) - 0.5)


def _post_sig_bias(acc, b):
    return jax.nn.sigmoid(acc + b)


def _post_vres(acc, v0, v, vf):
    return v + (vf - v) * jax.nn.sigmoid(v0 + acc)


def _post_ple(acc, b, x, t):
    return x + jax.nn.sigmoid(acc + b) * t


def _pad_rows(x, m):
    return jnp.pad(x, ((0, m - x.shape[0]),) + ((0, 0),) * (x.ndim - 1))


def kernel(x_prompt, x_sample, cache_kv_pages, cache_win_kv, state_wkv, state_shift, page_table, p_prompt, p_sample, ln_g, ln_b, rw_mix, rw_w_rkv, rw_w0, rw_w1, rw_w2, rw_a0, rw_a1, rw_a2, rw_v0, rw_v1, rw_v2, rw_g1, rw_g2, rw_k_k, rw_k_a, rw_r_k, rw_lnx_g, rw_lnx_b, rw_w_o, kv_ln_g, kv_ln_b, w_kv, cmp_pos, cmp_w1, cmp_w2, nsa_w_qg, nsa_b_g, nsa_w_o, moe_router, moe_bias, moe_w_in, moe_w_out, moe_ws_in, moe_ws_out, ple_proj, ple_gate, ple_gate_b):
    bp, tp, d = x_prompt.shape
    bs, ts, _ = x_sample.shape
    depth = ln_g.shape[0]
    n_a = state_wkv.shape[0]
    alpha = (2 * depth) ** 0.25

    def resid_ln(acc, g, b, x_rows):
        return _resid_ln(acc, g, b, x_rows, alpha)

    n_p, n_s = bp * tp, bs * ts
    n_tok = n_p + n_s
    m = _round_up(n_tok, 768) if n_tok > 768 else _round_up(n_tok, SUBLANES)
    ts_pad = _round_up(ts, SUBLANES)
    hh = d // RW_HEAD
    gw = N_KV * HEAD_DIM
    n_pg_s = page_table.shape[1]
    past_len = n_pg_s * cache_kv_pages.shape[1]
    win_buf = cache_win_kv.shape[1]
    assert cache_kv_pages.shape[1] == PAGE and tp % PAGE == 0 and win_buf % PAGE == 0 and ts < CMP_BLOCK

    def join(a_p, a_s):
        return _pad_rows(jnp.concatenate([a_p.reshape(n_p, -1), a_s.reshape(n_s, -1)], axis=0), m)

    def split_seq(z):
        zp = z[:n_p].reshape(bp, tp, -1)
        zs = jnp.pad(z[n_p:n_tok].reshape(bs, ts, -1), ((0, 0), (0, ts_pad - ts), (0, 0)))
        return zp, zs

    def merge_seq(zp, zs):
        return join(zp, zs[:, :ts])

    x = join(x_prompt, x_sample)
    pe = jnp.concatenate([p_prompt.reshape(depth, n_p, -1), p_sample.reshape(depth, n_s, -1)], axis=1)
    pe = jnp.pad(pe, ((0, 0), (0, m - n_tok), (0, 0)))

    shifts_p, shifts_s, wkvs_p, wkvs_s = [], [], [], []
    v_first = None
    ctx = None
    kv_rows_p = kv_rows_s = win_p = win_s = None
    for i in range(depth):
        if i < n_a:
            scan_passes = 3 if i == 0 else 1
            xp_, xs_ = x[:n_p].reshape(bp, tp, d), x[n_p:n_tok].reshape(bs, ts, d)
            shifts_p.append(xp_[:, -1])
            shifts_s.append(xs_[:, -1])
            prev = join(jnp.concatenate([jnp.zeros((bp, 1, d), F32), xp_[:, :-1]], axis=1),
                        jnp.concatenate([state_shift[i][:, None, :], xs_[:, :-1]], axis=1))
            mix = rw_mix[i]
            lin = linear
            r = lin(x, rw_w_rkv, widx=(i, 0), mix=(prev, mix[0]))
            k = lin(x, rw_w_rkv, widx=(i, 1), mix=(prev, mix[2]))
            v = lin(x, rw_w_rkv, widx=(i, 2), mix=(prev, mix[3]))
            lw = lin(lin(x, rw_w1, widx=(i,), mix=(prev, mix[1]), post=jnp.tanh), rw_w2, widx=(i,), rows=(rw_w0[i],),
                     post=_post_logdecay)
            if i == 0:
                v_first = v
            else:
                v = lin(lin(x, rw_v1, widx=(i - 1,), mix=(prev, mix[3])), rw_v2, widx=(i - 1,), rows=(rw_v0[i - 1],),
                        fulls=(v, v_first), post=_post_vres)
            a = lin(lin(x, rw_a1, widx=(i,), mix=(prev, mix[4])), rw_a2, widx=(i,), rows=(rw_a0[i],),
                    post=_post_sig_bias)
            g = lin(lin(x, rw_g1, widx=(i,), mix=(prev, mix[5]), post=jax.nn.sigmoid), rw_g2, widx=(i,))
            proj = (r, lw, k, v, a, g)
            vecs = (rw_k_k[i], rw_k_a[i], rw_r_k[i].reshape(-1), rw_lnx_g[i], rw_lnx_b[i])
            y_p, s_p = wkv7(*proj, *vecs, jnp.zeros((bp, hh, RW_HEAD, RW_HEAD), F32), seq_len=tp,
                            passes=scan_passes)
            y_s, s_s = wkv7_seq(*[split_seq(z)[1] for z in proj], *vecs, state_wkv[i], steps=ts)
            wkvs_p.append(s_p)
            wkvs_s.append(s_s)
            x = lin(merge_seq(y_p, y_s), rw_w_o, widx=(i,), rows=(ln_g[i, 0], ln_b[i, 0]), fulls=(x,), post=resid_ln,
                    full_rows=True)
        else:
            jn = i - n_a
            if ctx is None:
                rows = linear(layer_norm(x, kv_ln_g, kv_ln_b), w_kv)
                rows_p = rows[:n_p].reshape(bp, tp, 6 * gw)
                rows_s = rows[n_p:n_tok].reshape(bs, ts, 6 * gw)
                kv_rows_p = rows_p[:, :, :4 * gw].reshape(bp, tp, 4, N_KV, HEAD_DIM)
                kv_rows_s = rows_s[:, :, :4 * gw].reshape(bs, ts, 4, N_KV, HEAD_DIM)
                win_p = rows_p[:, max(tp - WINDOW, 0):, 4 * gw:].reshape(bp, -1, 2, N_KV, HEAD_DIM)
                new_win = rows_s[:, :, 4 * gw:].reshape(bs, ts, 2, N_KV, HEAD_DIM)
                win_s = jnp.concatenate([cache_win_kv, new_win], axis=1)[:, ts:]
                n_pg_p = tp // PAGE
                pages_p = rows[:n_p].reshape(bp * n_pg_p, PAGE, 6 * gw)
                table_p = jnp.arange(bp * n_pg_p, dtype=jnp.int32).reshape(bp, n_pg_p)
                pages_s = cache_kv_pages
                pages_w = cache_win_kv.reshape(bs * (win_buf // PAGE), PAGE, 2, N_KV, HEAD_DIM)

                def perm(c):
                    return jnp.concatenate([c[:, 0::2], c[:, 1::2]], axis=1)

                cmp_p = compress(pages_p, table_p, 0, cmp_pos, cmp_w1, cmp_w2)
                cmp_s = compress(pages_s, page_table.astype(jnp.int32), 0, cmp_pos, cmp_w1, cmp_w2)
                tail_sel = jnp.pad(rows_s[:, :, 2 * gw:4 * gw], ((0, 0), (0, PAGE - ts), (0, 0)))
                tail_win = jnp.pad(rows_s[:, :, 4 * gw:6 * gw], ((0, 0), (0, PAGE - ts), (0, 0)))
                tq_p = PAGE
                n_qt = tp // tq_p
                qi = jnp.arange(n_qt, dtype=jnp.int32)[:, None]
                kj = jnp.arange(n_pg_p, dtype=jnp.int32)[None, :]
                base = (jnp.arange(bp, dtype=jnp.int32) * n_pg_p)[:, None, None]
                tbl_sel_p = base + jnp.minimum(kj, qi)[None]
                kp0_sel_p = jnp.broadcast_to(jnp.where(kj <= qi, kj * PAGE, -1)[None], tbl_sel_p.shape)
                n_wt = WINDOW // PAGE + 1
                kw = qi - (n_wt - 1) + jnp.arange(n_wt, dtype=jnp.int32)[None, :]
                tbl_win_p = base + jnp.maximum(kw, 0)[None]
                kp0_win_p = jnp.broadcast_to(jnp.where(kw >= 0, kw * PAGE, -1)[None], tbl_win_p.shape)
                tbl_sel_s = page_table.astype(jnp.int32)[:, None, :]
                kp0_sel_s = jnp.broadcast_to((jnp.arange(n_pg_s, dtype=jnp.int32) * PAGE)[None, None], tbl_sel_s.shape)
                n_ww = win_buf // PAGE
                tbl_win_s = jnp.arange(bs * n_ww, dtype=jnp.int32).reshape(bs, 1, n_ww)
                kp0_win_s = jnp.broadcast_to(
                    (past_len - win_buf + jnp.arange(n_ww, dtype=jnp.int32) * PAGE)[None, None], tbl_win_s.shape)
                ctx = dict(kc_p=perm(cmp_p[:, 0]), vc_p=perm(cmp_p[:, 1]), kc_s=perm(cmp_s[:, 0]), vc_s=perm(cmp_s[:, 1]))
            dq = nsa_w_qg.shape[2] - nsa_b_g.shape[1]
            q = linear(x, nsa_w_qg, widx=(jn,), ncols=dq)
            gates = linear(x, nsa_w_qg[jn][:, dq:], rows=(nsa_b_g[jn],), post=_post_sig_bias)
            (q_p, q_s), (g_p, g_s) = split_seq(q), split_seq(gates)
            ns_p = -(-tp // SEL_BLOCK)
            ns_s = -(-(past_len + ts) // SEL_BLOCK)
            o_p, sel_p = nsa_cmp(q_p, ctx['kc_p'], ctx['vc_p'], g_p, q_offset=0, n_sel_blocks=ns_p, tq=tq_p)
            o_p = nsa_attn(q_p, pages_p, tbl_sel_p, kp0_sel_p, 2, 3, None, sel_p, g_p, o_p, mode="sel", q_offset=0,
                           tq=tq_p, tail_pos0=0, branch=1, ppt=_pick_tile(tp // PAGE, (4, 2, 1)))
            o_p = nsa_attn(q_p, pages_p, tbl_win_p, kp0_win_p, 4, 5, None, None, g_p, o_p, mode="win", q_offset=0,
                           tq=tq_p, tail_pos0=0, branch=2, ppt=WINDOW // PAGE + 1)
            o_s, sel_s = nsa_cmp(q_s, ctx['kc_s'], ctx['vc_s'], g_s, q_offset=past_len, n_sel_blocks=ns_s, tq=ts_pad)
            o_s = nsa_attn(q_s, pages_s, tbl_sel_s, kp0_sel_s, 2, 3, tail_sel, sel_s, g_s, o_s, mode="sel",
                           q_offset=past_len, tq=ts_pad, tail_pos0=past_len, branch=1,
                           ppt=_pick_tile(n_pg_s, (4, 2, 1)))
            o_s = nsa_attn(q_s, pages_w, tbl_win_s, kp0_win_s, 0, 1, tail_win, None, g_s, o_s, mode="win",
                           q_offset=past_len, tq=ts_pad, tail_pos0=past_len, branch=2,
                           ppt=_pick_tile(win_buf // PAGE, (4, 2, 1)))
            x = linear(merge_seq(o_p, o_s), nsa_w_o, widx=(jn,), rows=(ln_g[i, 0], ln_b[i, 0]), fulls=(x,),
                       post=resid_ln, full_rows=True)
        x = moe_ffn(x, n_tok, i, moe_router, moe_bias, moe_w_in, moe_w_out, moe_ws_in, moe_ws_out,
                    ln_g[i, 1], ln_b[i, 1], alpha)
        t = linear(pe, ple_proj, xidx=(i,), widx=(i,))
        x = linear(x, ple_gate, widx=(i,), rows=(ple_gate_b[i],), fulls=(x, t), post=_post_ple)

    y_p = x[:n_p].reshape(bp, tp, d)
    y_s = x[n_p:n_tok].reshape(bs, ts, d)
    return (y_p, y_s, kv_rows_p, kv_rows_s, win_p, win_s, jnp.stack(wkvs_p), jnp.stack(wkvs_s),
            jnp.stack(shifts_p), jnp.stack(shifts_s))
```

```python
import functools

import jax
import jax.numpy as jnp
from jax import lax
from jax.experimental import pallas as pl
from jax.experimental.pallas import tpu as pltpu

F32 = jnp.float32
BF16 = jnp.bfloat16

RW_HEAD = 64
GN_EPS = 64e-5
N_KV = 4
HEAD_DIM = 128
CMP_BLOCK = 32
SEL_BLOCK = 64
N_SEL = 16
WINDOW = 512
FORCED_BONUS = 100.0
N_GROUPS = 8
TOPK_GROUPS = 4
TOP_K = 8
ROUTED_SCALE = 2.5
LN_EPS = 1e-5
NEG = -1e30

LANES = 128
SUBLANES = 8
PAGE = 128
VMEM_LIMIT = 56 * 1024 * 1024
LINEAR_W_BYTES = 16 * 1024 * 1024
LINEAR_VMEM_BYTES = 46 * 1024 * 1024


def _cparams(n_axes):
    return pltpu.CompilerParams(dimension_semantics=("arbitrary",) * n_axes, vmem_limit_bytes=VMEM_LIMIT)


def _split(x):
    hi = x.astype(BF16)
    lo = (x - hi.astype(F32)).astype(BF16)
    return hi, lo


_NN = (((1,), (0,)), ((), ()))
_NT = (((1,), (1,)), ((), ()))
_TN = (((0,), (0,)), ((), ()))


def _dg(a, b, dims):
    return lax.dot_general(a, b, dims, preferred_element_type=F32)


def _dot(a, b, passes=1, dims=_NN):
    if passes == 1:
        return _dg(a.astype(BF16), b.astype(BF16), dims)
    ah, al = _split(a)
    bh, bl = _split(b)
    return _dg(ah, bh, dims) + (_dg(ah, bl, dims) + _dg(al, bh, dims))


def _round_up(x, m):
    return (x + m - 1) // m * m


def _pick_tile(n, prefs):
    for t in prefs:
        if n % t == 0:
            return t
    return n


def _linear_kernel(*refs, passes, has_mix, glu, has_second, n_row, n_full, post):
    it = iter(refs)
    x_ref = next(it)
    xp_ref = next(it) if has_mix else None
    mix_ref = next(it) if has_mix else None
    w_refs = [next(it) for _ in range(2 if glu else 1)]
    x2_ref = next(it) if has_second else None
    w2_ref = next(it) if has_second else None
    row_refs = [next(it) for _ in range(n_row)]
    full_refs = [next(it) for _ in range(n_full)]
    o_ref = next(it)
    scr = list(it)
    per = 2 if passes == 3 else 1

    @pl.when(pl.program_id(1) == 0)
    def _():
        for wi, w_ref in enumerate(w_refs):
            w = w_ref[...]
            if passes == 3:
                hi, lo = _split(w)
                scr[wi * per][...] = hi
                scr[wi * per + 1][...] = lo
            else:
                scr[wi * per][...] = w.astype(BF16)
        if has_second:
            scr[-1][...] = w2_ref[...].astype(BF16)

    x = x_ref[...]
    if has_mix:
        x = x + (xp_ref[...] - x) * mix_ref[...]
    if passes == 3:
        xh, xl = _split(x)
    else:
        xh = x.astype(BF16)
    accs = []
    for wi in range(len(w_refs)):
        wh = scr[wi * per][...]
        acc = jnp.dot(xh, wh, preferred_element_type=F32)
        if passes == 3:
            wl = scr[wi * per + 1][...]
            acc = acc + (jnp.dot(xh, wl, preferred_element_type=F32) + jnp.dot(xl, wh, preferred_element_type=F32))
        accs.append(acc)
    if has_second:
        accs.append(jnp.dot(x2_ref[...].astype(BF16), scr[-1][...], preferred_element_type=F32))
    extras = [r[...] for r in row_refs] + [r[...] for r in full_refs]
    o_ref[...] = post(*accs, *extras)


def _identity(acc):
    return acc


def linear(x, w, *, xidx=(), widx=(), ncols=None, passes=1, mix=None, glu=False, rows=(), fulls=(), post=_identity,
           full_rows=False, second=None, tm=None, tn=None):
    m, k = x.shape[-2:]
    n = (ncols or w.shape[-1]) // (2 if glu else 1)
    lead = (None,) * len(widx)
    n_w = 2 if glu else 1
    tn = tn or next((c for c in (n, n // 2, 512, 256, 128)
                     if n % c == 0 and (c % LANES == 0 or c == n) and n_w * k * c * 4 <= LINEAR_W_BYTES), n)
    k2 = second[0].shape[-1] if second is not None else 0
    w_bytes = n_w * k * tn * (4 + 2 * (2 if passes == 3 else 1)) + k2 * tn * 6
    per_row = ((2 if mix is not None else 1) * k + k2) * 4 * 2 + (1 + len(fulls)) * tn * 4 * 2
    tm = tm or next((c for c in (768, 512, 384, 256, 128, 64, 32, 16, 8)
                     if m % c == 0 and w_bytes + c * per_row <= LINEAR_VMEM_BYTES), m)
    nb = n // tn
    assert nb == 1 or not full_rows, "a row-wise epilogue needs the whole row in one block"
    grid = (nb, m // tm)
    x_spec = pl.BlockSpec((None,) * len(xidx) + (tm, k), lambda j, i: xidx + (i, 0))
    in_specs = [x_spec]
    args = [x]
    if mix is not None:
        in_specs += [x_spec, pl.BlockSpec((1, k), lambda j, i: (0, 0))]
        args += [mix[0], mix[1].reshape(1, k)]
    in_specs.append(pl.BlockSpec(lead + (k, tn), lambda j, i: widx + (0, j), pipeline_mode=pl.Buffered(1)))
    args.append(w)
    if glu:
        in_specs.append(pl.BlockSpec(lead + (k, tn), lambda j, i: widx + (0, j + nb), pipeline_mode=pl.Buffered(1)))
        args.append(w)
    if second is not None:
        x2, xidx2, w2, widx2 = second
        in_specs += [pl.BlockSpec((None,) * len(xidx2) + (tm, k2), lambda j, i: xidx2 + (i, 0)),
                     pl.BlockSpec((None,) * len(widx2) + (k2, tn), lambda j, i: widx2 + (0, j),
                                  pipeline_mode=pl.Buffered(1))]
        args += [x2, w2]
    for r in rows:
        in_specs.append(pl.BlockSpec((1, tn), lambda j, i: (0, j)))
        args.append(r.reshape(1, n))
    for f in fulls:
        in_specs.append(pl.BlockSpec((tm, tn), lambda j, i: (i, j)))
        args.append(f)
    scratch = [pltpu.VMEM((k, tn), BF16) for _ in range(n_w * (2 if passes == 3 else 1))]
    if second is not None:
        scratch.append(pltpu.VMEM((k2, tn), BF16))
    kern = functools.partial(_linear_kernel, passes=passes, has_mix=mix is not None, glu=glu,
                             has_second=second is not None, n_row=len(rows), n_full=len(fulls), post=post)
    return pl.pallas_call(
        kern,
        grid=grid,
        in_specs=in_specs,
        out_specs=pl.BlockSpec((tm, tn), lambda j, i: (i, j)),
        out_shape=jax.ShapeDtypeStruct((m, n), F32),
        scratch_shapes=scratch,
        compiler_params=_cparams(2),
    )(*args)


def _resid_ln(h, g, b, x, alpha):
    z = alpha * x + h
    mu = jnp.mean(z, -1, keepdims=True)
    zc = z - mu
    var = jnp.mean(zc * zc, -1, keepdims=True)
    return zc * lax.rsqrt(var + LN_EPS) * g + b


def _ln_kernel(*refs, alpha, has_h):
    if has_h:
        x_ref, h_ref, g_ref, b_ref, o_ref = refs
        x = alpha * x_ref[...] + h_ref[...]
    else:
        x_ref, g_ref, b_ref, o_ref = refs
        x = x_ref[...]
    mu = jnp.mean(x, -1, keepdims=True)
    xc = x - mu
    var = jnp.mean(xc * xc, -1, keepdims=True)
    o_ref[...] = xc * lax.rsqrt(var + LN_EPS) * g_ref[...] + b_ref[...]


def layer_norm(x, g, b, h=None, alpha=1.0):
    m, d = x.shape
    tm = _pick_tile(m, (384, 256, 128, 64, 32, 16, 8))
    spec = pl.BlockSpec((tm, d), lambda i: (i, 0))
    vspec = pl.BlockSpec((1, d), lambda i: (0, 0))
    args = [x] + ([h] if h is not None else []) + [g.reshape(1, d), b.reshape(1, d)]
    in_specs = [spec] + ([spec] if h is not None else []) + [vspec, vspec]
    return pl.pallas_call(
        functools.partial(_ln_kernel, alpha=alpha, has_h=h is not None),
        grid=(m // tm,),
        in_specs=in_specs,
        out_specs=spec,
        out_shape=jax.ShapeDtypeStruct((m, d), F32),
        compiler_params=_cparams(1),
    )(*args)


def _bf16r(x):
    return x.astype(BF16).astype(F32)


def _matvec_bf16(s, vec_row):
    return jnp.sum(_bf16r(s) * _bf16r(vec_row), axis=1, keepdims=True)


def _row_from_col(col, eye):
    return jnp.sum(eye * col, axis=0, keepdims=True)


def _col_from_row(row, eye):
    return jnp.sum(eye * row, axis=1, keepdims=True)


def _wkv_out(y, v_h, rk_h, lg, lb):
    mu = jnp.mean(y, -1, keepdims=True)
    yc = y - mu
    var = jnp.mean(yc * yc, -1, keepdims=True)
    return yc * lax.rsqrt(var + GN_EPS) * lg + lb + jnp.sum(rk_h, -1, keepdims=True) * v_h


def _wkv_seq_kernel(r_ref, lw_ref, k_ref, v_ref, a_ref, g_ref, kk_ref, ka_ref, rk_ref, lg_ref, lb_ref, s0_ref,
                    y_ref, s_ref, *, steps, heads):
    N = RW_HEAD
    r = r_ref[0]
    k = k_ref[0]
    v = v_ref[0]
    a = a_ref[0]
    w = jnp.exp(lw_ref[0])
    kkw = k * kk_ref[...]
    k2 = k * (1.0 + (a - 1.0) * ka_ref[...])
    rk = r * k2 * rk_ref[...]
    eye = (lax.broadcasted_iota(jnp.int32, (N, N), 0) == lax.broadcasted_iota(jnp.int32, (N, N), 1)).astype(F32)
    tpad = r.shape[0]
    H = range(heads)
    sls = [slice(h * N, (h + 1) * N) for h in H]
    kk = [kkw[:, sl] for sl in sls]
    kk = [z * lax.rsqrt(jnp.maximum(jnp.sum(z * z, -1, keepdims=True), 1e-24)) for z in kk]
    b_h = [kk[h] * a[:, sls[h]] for h in H]
    s = [s0_ref[0, h] for h in H]
    ys = [[] for _ in H]
    for t in range(steps):
        sa = [_matvec_bf16(s[h], -kk[h][t:t + 1]) for h in H]
        vcol = [_col_from_row(v[t:t + 1, sls[h]], eye) for h in H]
        s = [s[h] * w[t:t + 1, sls[h]] + sa[h] * b_h[h][t:t + 1] + vcol[h] * k2[t:t + 1, sls[h]] for h in H]
        yt = [_row_from_col(_matvec_bf16(s[h], r[t:t + 1, sls[h]]), eye) for h in H]
        for h in H:
            ys[h].append(yt[h])
    for h in H:
        s_ref[0, h] = s[h]
    y = [jnp.concatenate(ys[h] + [jnp.zeros((tpad - steps, N), F32)], axis=0) for h in H]
    outs = [_wkv_out(y[h], v[:, sls[h]], rk[:, sls[h]], lg_ref[:, sls[h]], lb_ref[:, sls[h]]) for h in H]
    y_ref[0] = jnp.concatenate(outs, axis=1) * g_ref[0]


def wkv7_seq(r, lw, k, v, a, g, k_k, k_a, r_k, lnx_g, lnx_b, s0, *, steps):
    bsz, t, d = r.shape
    n = RW_HEAD
    hh = d // n
    heads = _pick_tile(hh, (4, 2))
    wide = heads * n
    seq = pl.BlockSpec((1, t, wide), lambda b, hb: (b, 0, hb))
    vec = pl.BlockSpec((1, wide), lambda b, hb: (0, hb))
    st = pl.BlockSpec((1, heads, n, n), lambda b, hb: (b, hb, 0, 0))
    vecs = [z.reshape(1, d) for z in (k_k, k_a, r_k, lnx_g, lnx_b)]
    return pl.pallas_call(
        functools.partial(_wkv_seq_kernel, steps=steps, heads=heads),
        grid=(bsz, hh // heads),
        in_specs=[seq] * 6 + [vec] * 5 + [st],
        out_specs=[seq, st],
        out_shape=[jax.ShapeDtypeStruct((bsz, t, d), F32), jax.ShapeDtypeStruct((bsz, hh, n, n), F32)],
        compiler_params=_cparams(2),
    )(r, lw, k, v, a, g, *vecs, s0)


def _wkv_kernel(r_ref, lw_ref, k_ref, v_ref, a_ref, g_ref, kk_ref, ka_ref, rk_ref, lg_ref, lb_ref, s0_ref,
                y_ref, s_ref, *, chunk, heads, passes):
    L, N = chunk, RW_HEAD

    @pl.when(pl.program_id(2) == 0)
    def _():
        s_ref[...] = s0_ref[...]

    r = r_ref[...]
    lw = lw_ref[...]
    k = k_ref[...]
    v = v_ref[...]
    a = a_ref[...]
    kkw = k * kk_ref[...]
    k2 = k * (1.0 + (a - 1.0) * ka_ref[...])
    rk = r * k2 * rk_ref[...]

    row = lax.broadcasted_iota(jnp.int32, (L, L), 0)
    col = lax.broadcasted_iota(jnp.int32, (L, L), 1)
    tri_incl = (col <= row)
    tri_strict = (col < row)
    tri = tri_incl.astype(BF16)
    l1 = lw.astype(BF16)
    rem = lw - l1.astype(F32)
    l2 = rem.astype(BF16)
    l3 = (rem - l2.astype(F32)).astype(BF16)
    cs = (jnp.dot(tri, l1, preferred_element_type=F32) + jnp.dot(tri, l2, preferred_element_type=F32)
          + jnp.dot(tri, l3, preferred_element_type=F32))
    cs_last = cs[L - 1:L, :]
    e_prev = jnp.exp(cs - lw)
    e_inv = jnp.exp(-cs)
    e_fwd = jnp.exp(cs)
    e_tail = jnp.exp(cs_last - cs)
    gam = jnp.exp(cs_last)

    n_dbl = max(1, (L - 1).bit_length())
    eye = (lax.broadcasted_iota(jnp.int32, (N, N), 0) == lax.broadcasted_iota(jnp.int32, (N, N), 1)).astype(F32)
    H = range(heads)
    sls = [slice(h * N, (h + 1) * N) for h in H]
    kk = [kkw[:, sl] for sl in sls]
    kk = [z * lax.rsqrt(jnp.maximum(jnp.sum(z * z, -1, keepdims=True), 1e-24)) for z in kk]
    b_h = [kk[h] * a[:, sls[h]] for h in H]
    k_h = [k2[:, sl] for sl in sls]
    v_h = [v[:, sl] for sl in sls]
    lhs2 = [jnp.concatenate([-kk[h] * e_prev[:, sls[h]], r[:, sls[h]] * e_fwd[:, sls[h]]], axis=0) for h in H]
    rhs2 = [jnp.concatenate([b_h[h] * e_inv[:, sls[h]], k_h[h] * e_inv[:, sls[h]]], axis=0) for h in H]
    bk = [jnp.concatenate([b_h[h] * e_tail[:, sls[h]], k_h[h] * e_tail[:, sls[h]]], axis=0) for h in H]
    s0 = [s_ref[0, h] for h in H]
    p = [_dot(lhs2[h], rhs2[h], passes, _NT) for h in H]
    q = [_dot(lhs2[h], s0[h], passes, _NT) for h in H]
    pk = [jnp.where(tri_strict, p[h][:L, :L], 0.0) for h in H]
    a_k = [jnp.where(tri_strict, p[h][:L, L:], 0.0) for h in H]
    rbk = [jnp.concatenate([jnp.where(tri_incl, p[h][L:, :L], 0.0), jnp.where(tri_incl, p[h][L:, L:], 0.0)], axis=1)
           for h in H]
    x = [q[h][:L] + _dot(a_k[h], v_h[h], passes) for h in H]
    for d in range(n_dbl):
        x = [x[h] + _dot(pk[h], x[h], passes) for h in H]
        if d + 1 < n_dbl:
            pk = [_dot(pk[h], pk[h], passes) for h in H]
    uv = [jnp.concatenate([x[h], v_h[h]], axis=0) for h in H]
    y = [q[h][L:] + _dot(rbk[h], uv[h], passes) for h in H]
    s_new = [s0[h] * gam[:, sls[h]] + _dot(uv[h], bk[h], passes, _TN) for h in H]
    for h in H:
        s_ref[0, h] = s_new[h]
    last = lax.broadcasted_iota(jnp.int32, (L, 1), 0) == L - 1
    y = [jnp.where(last, _row_from_col(_matvec_bf16(s_new[h], r[L - 1:L, sls[h]]), eye), y[h]) for h in H]
    outs = [_wkv_out(y[h], v_h[h], rk[:, sls[h]], lg_ref[:, sls[h]], lb_ref[:, sls[h]]) for h in H]
    y_ref[...] = jnp.concatenate(outs, axis=1) * g_ref[...]


def wkv7(r, lw, k, v, a, g, k_k, k_a, r_k, lnx_g, lnx_b, s0, *, seq_len, passes):
    bsz, t = s0.shape[0], seq_len
    d = r.shape[1]
    n = RW_HEAD
    hh = d // n
    heads = _pick_tile(hh, (16, 4, 2))
    chunk = _pick_tile(t, (128, 64))
    wide = heads * n
    n_chunks = t // chunk
    seq = pl.BlockSpec((chunk, wide), lambda b, hb, c: (b * n_chunks + c, hb))
    vec = pl.BlockSpec((1, wide), lambda b, hb, c: (0, hb))
    st = pl.BlockSpec((1, heads, n, n), lambda b, hb, c: (b, hb, 0, 0))
    vecs = [z.reshape(1, d) for z in (k_k, k_a, r_k, lnx_g, lnx_b)]
    return pl.pallas_call(
        functools.partial(_wkv_kernel, chunk=chunk, heads=heads, passes=passes),
        grid=(bsz, hh // heads, n_chunks),
        in_specs=[seq] * 6 + [vec] * 5 + [st],
        out_specs=[seq, st],
        out_shape=[jax.ShapeDtypeStruct((bsz * t, d), F32), jax.ShapeDtypeStruct((bsz, hh, n, n), F32)],
        compiler_params=_cparams(3),
    )(r, lw, k, v, a, g, *vecs, s0)


def _route_kernel(x_ref, rt_ref, bias_ref, eidx_ref, w_ref, pos_ref, cnt_ref, *, n_valid):
    ne = rt_ref.shape[0]
    per = ne // N_GROUPS
    logits = _dot(rt_ref[...], x_ref[...], 1, _NT)
    s = jax.nn.sigmoid(logits)
    sb = s + bias_ref[...]
    tm = sb.shape[1]
    sub = lax.broadcasted_iota(jnp.int32, (per, tm), 0)
    gs_rows = []
    for gi in range(N_GROUPS):
        tile = sb[gi * per:(gi + 1) * per, :]
        m1 = jnp.max(tile, axis=0, keepdims=True)
        first = jnp.min(jnp.where(tile == m1, sub, per), axis=0, keepdims=True)
        m2 = jnp.max(jnp.where(sub == first, -jnp.inf, tile), axis=0, keepdims=True)
        gs_rows.append(m1 + m2)
    keep_rows = []
    for gi in range(N_GROUPS):
        cnt = jnp.zeros((1, tm), jnp.int32)
        for gj in range(N_GROUPS):
            if gj == gi:
                continue
            beats = (gs_rows[gj] > gs_rows[gi]) if gj > gi else (gs_rows[gj] >= gs_rows[gi])
            cnt = cnt + beats.astype(jnp.int32)
        keep_rows.append(jnp.broadcast_to(cnt, (per, tm)))
    masked = jnp.where(jnp.concatenate(keep_rows, axis=0) < TOPK_GROUPS, sb, -1e9)
    eidx = lax.broadcasted_iota(jnp.int32, (ne, tm), 0)
    rank = jnp.zeros((ne, tm), jnp.int32)
    for ej in range(ne):
        rowv = masked[ej:ej + 1, :]
        rank = rank + jnp.where(eidx > ej, (rowv >= masked).astype(jnp.int32), (rowv > masked).astype(jnp.int32))
    tok = pl.program_id(0) * tm + lax.broadcasted_iota(jnp.int32, (ne, tm), 1)
    rank = jnp.where(tok < n_valid, rank, ne)
    sel = rank < TOP_K
    self32 = sel.astype(F32)
    den = jnp.sum(jnp.where(sel, s, 0.0), axis=0, keepdims=True)
    wgt = jnp.where(sel, s / den * ROUTED_SCALE, 0.0)

    @pl.when(pl.program_id(0) == 0)
    def _():
        cnt_ref[...] = jnp.zeros(cnt_ref.shape, F32)

    upper = (lax.broadcasted_iota(jnp.int32, (tm, tm), 0) < lax.broadcasted_iota(jnp.int32, (tm, tm), 1)).astype(BF16)
    carry = cnt_ref[...]
    pos = jnp.dot(self32.astype(BF16), upper, preferred_element_type=F32) + carry[:, :1]
    cnt_ref[...] = carry + jnp.sum(self32, axis=1, keepdims=True)
    eidx_f = eidx.astype(F32)
    rows_e, rows_w, rows_p = [], [], []
    for kk in range(TOP_K):
        hit = rank == kk
        rows_e.append(jnp.sum(jnp.where(hit, eidx_f, 0.0), axis=0, keepdims=True))
        rows_w.append(jnp.sum(jnp.where(hit, wgt, 0.0), axis=0, keepdims=True))
        rows_p.append(jnp.sum(jnp.where(hit, pos, 0.0), axis=0, keepdims=True))
    eidx_ref[...] = jnp.concatenate(rows_e, axis=0).astype(jnp.int32)
    w_ref[...] = jnp.concatenate(rows_w, axis=0)
    pos_ref[...] = jnp.concatenate(rows_p, axis=0).astype(jnp.int32)


def route(x, router_t, bias, n_valid):
    m, d = x.shape
    ne = router_t.shape[0]
    tm = _pick_tile(m, (256, 128))
    out = pl.BlockSpec((TOP_K, tm), lambda i: (0, i))
    return pl.pallas_call(
        functools.partial(_route_kernel, n_valid=n_valid),
        grid=(m // tm,),
        in_specs=[pl.BlockSpec((tm, d), lambda i: (i, 0)), pl.BlockSpec((ne, d), lambda i: (0, 0)),
                  pl.BlockSpec((ne, 1), lambda i: (0, 0))],
        out_specs=[out, out, out, pl.BlockSpec((ne, LANES), lambda i: (0, 0))],
        out_shape=[jax.ShapeDtypeStruct((TOP_K, m), jnp.int32), jax.ShapeDtypeStruct((TOP_K, m), F32),
                   jax.ShapeDtypeStruct((TOP_K, m), jnp.int32), jax.ShapeDtypeStruct((ne, LANES), F32)],
        compiler_params=_cparams(1),
    )(x, router_t, bias.reshape(ne, 1))


def _experts_kernel(be_ref, nu_ref, xs_ref, win_ref, wout_ref, o_ref, win_s, wout_s):
    i = pl.program_id(0)
    f = wout_ref.shape[0]

    @pl.when(i < nu_ref[0])
    def _():
        changed = jnp.logical_or(i == 0, be_ref[i] != be_ref[jnp.maximum(i - 1, 0)])

        @pl.when(changed)
        def _():
            win_s[...] = win_ref[...].astype(BF16)
            wout_s[...] = wout_ref[...].astype(BF16)

        hu = jnp.dot(xs_ref[...].astype(BF16), win_s[...], preferred_element_type=F32)
        hid = jax.nn.silu(hu[:, :f]) * hu[:, f:]
        y = jnp.dot(hid.astype(BF16), wout_s[...], preferred_element_type=F32)
        o_ref[...] = y.astype(BF16)

    @pl.when(i >= nu_ref[0])
    def _():
        o_ref[...] = jnp.zeros(o_ref.shape, BF16)


def experts(xs, blk_expert, n_used, w_in, w_out, layer, tm):
    rws, d = xs.shape
    f2 = w_in.shape[3]
    f = f2 // 2
    grid_spec = pltpu.PrefetchScalarGridSpec(
        num_scalar_prefetch=2,
        grid=(rws // tm,),
        in_specs=[
            pl.BlockSpec((tm, d), lambda i, be, nu: (jnp.minimum(i, nu[0] - 1), 0)),
            pl.BlockSpec((None, None, d, f2), lambda i, be, nu: (layer, be[i], 0, 0)),
            pl.BlockSpec((None, None, f, d), lambda i, be, nu: (layer, be[i], 0, 0)),
        ],
        out_specs=pl.BlockSpec((tm, d), lambda i, be, nu: (i, 0)),
        scratch_shapes=[pltpu.VMEM((d, f2), BF16), pltpu.VMEM((f, d), BF16)],
    )
    return pl.pallas_call(
        _experts_kernel,
        grid_spec=grid_spec,
        out_shape=jax.ShapeDtypeStruct((rws, d), BF16),
        compiler_params=_cparams(1),
    )(blk_expert, n_used, xs, w_in, w_out)


def moe_ffn(x, n_valid, layer, router, bias, w_in, w_out, ws_in, ws_out, ln_g, ln_b, alpha):
    m, d = x.shape
    ne = router.shape[2]
    eidx, ew, posk, cnt = route(x, router[layer].T, bias[layer], n_valid)
    tm = 256 if m * TOP_K // ne >= 256 else _round_up(max(m * TOP_K // ne, SUBLANES), SUBLANES)
    nk = m * TOP_K
    counts = cnt[:, 0].astype(jnp.int32)
    padded = (counts + tm - 1) // tm * tm
    pad_end = jnp.cumsum(padded)
    pad_start = pad_end - padded
    n_blocks = (nk + tm - 1) // tm + ne
    n_rows = n_blocks * tm
    tok = jnp.arange(m, dtype=jnp.int32)
    first_row = jnp.sum(jnp.where(eidx[:, :, None] == jnp.arange(ne, dtype=jnp.int32), pad_start, 0), axis=-1)
    spare = n_rows - 1 - ((tok[None, :] - n_valid) * TOP_K + jnp.arange(TOP_K, dtype=jnp.int32)[:, None])
    dest = jnp.where(tok[None, :] < n_valid, first_row + posk, spare).astype(jnp.int32)
    token_of_row = jnp.zeros((n_rows,), jnp.int32).at[dest.reshape(-1)].set(jnp.tile(tok, TOP_K), unique_indices=True)
    blk_start = jnp.arange(n_blocks, dtype=jnp.int32) * tm
    blk_expert = jnp.minimum(jnp.sum((pad_end[None, :] <= blk_start[:, None]).astype(jnp.int32), axis=1), ne - 1)
    n_used = (pad_end[-1:] // tm).astype(jnp.int32)
    xs = jnp.take(x, token_of_row, axis=0, mode="clip")
    yb = experts(xs, blk_expert, n_used, w_in, w_out, layer, tm)
    wr = ew.astype(BF16).astype(F32)
    picked = yb.at[dest.T].get(mode="promise_in_bounds").astype(F32)
    routed = jnp.sum(picked * wr.T[:, :, None], axis=1)
    hid = linear(x, ws_in, widx=(layer,), glu=True, post=lambda a, b: jax.nn.silu(a) * b)
    return linear(hid, ws_out, widx=(layer,), rows=(ln_g, ln_b), fulls=(routed, x), full_rows=True,
                  post=lambda acc, g, b, rt, xr: _resid_ln(acc + rt, g, b, xr, alpha))


def _gelu_tanh(x):
    return 0.5 * x * (1.0 + jnp.tanh(0.7978845608028654 * (x + 0.044715 * x * x * x)))


def _group_tile(ref, g):
    if len(ref.shape) == 3:
        return ref[:, g, :]
    return ref[:, g * HEAD_DIM:(g + 1) * HEAD_DIM]


def _page_spec(pages, index_fn, slot_fn):
    if pages.ndim == 5:
        return pl.BlockSpec((None, PAGE, None, N_KV, HEAD_DIM), lambda *a: (index_fn(*a), 0, slot_fn(*a), 0, 0))
    return pl.BlockSpec((None, PAGE, N_KV * HEAD_DIM), lambda *a: (index_fn(*a), 0, slot_fn(*a)))


def _compress_kernel(tbl_ref, *refs, group, ppt):
    pg_refs = refs[:ppt]
    pos_ref, w1_ref, w2_ref, o_ref, w1_s = refs[ppt:ppt + 5]
    slabs = refs[ppt + 5:]
    p = pl.program_id(2)
    pin = p % (group // ppt)

    @pl.when(p == 0)
    def _():
        w1_s[...] = w1_ref[0].astype(BF16)

    for pi, pg_ref in enumerate(pg_refs):
        for g in range(N_KV):
            slabs[g][pl.ds(pl.multiple_of((pin * ppt + pi) * PAGE, PAGE), PAGE), :] = _group_tile(pg_ref, g)

    @pl.when(pin == group // ppt - 1)
    def _():
        per_page = PAGE // CMP_BLOCK
        nrow = group * per_page
        hid = w1_ref.shape[2]
        acc = jnp.zeros((N_KV * nrow, hid), F32)
        for t in range(CMP_BLOCK):
            lhs = jnp.concatenate([slabs[g][pl.ds(t, nrow, stride=CMP_BLOCK), :] for g in range(N_KV)], axis=0)
            lhs = (lhs + pos_ref[0, t:t + 1, :]).astype(BF16)
            acc = acc + jnp.dot(lhs, w1_s[t * HEAD_DIM:(t + 1) * HEAD_DIM, :], preferred_element_type=F32)
        out = _dot(_gelu_tanh(acc), w2_ref[0])
        for g in range(N_KV):
            o_ref[0, 0, :, g * HEAD_DIM:(g + 1) * HEAD_DIM] = out[g * nrow:(g + 1) * nrow]


def compress(pages, table, slot0, cmp_pos, cmp_w1, cmp_w2):
    bsz, n_pg = table.shape
    group = min(32, n_pg)
    ppt = _pick_tile(group, (4, 2, 1))
    gw = N_KV * HEAD_DIM
    per_page = PAGE // CMP_BLOCK
    hid = cmp_w1.shape[2]

    def page_of(pi):
        return lambda b, s, p, tbl: tbl[b * n_pg + p * ppt + pi]

    grid_spec = pltpu.PrefetchScalarGridSpec(
        num_scalar_prefetch=1,
        grid=(bsz, 2, n_pg // ppt),
        in_specs=[_page_spec(pages, page_of(pi), lambda b, s, p, tbl: slot0 + s) for pi in range(ppt)] + [
            pl.BlockSpec((1, CMP_BLOCK, HEAD_DIM), lambda b, s, p, tbl: (s, 0, 0)),
            pl.BlockSpec((1, CMP_BLOCK * HEAD_DIM, hid), lambda b, s, p, tbl: (s, 0, 0)),
            pl.BlockSpec((1, hid, HEAD_DIM), lambda b, s, p, tbl: (s, 0, 0)),
        ],
        out_specs=pl.BlockSpec((1, 1, group * per_page, gw), lambda b, s, p, tbl: (b, s, p // (group // ppt), 0)),
        scratch_shapes=[pltpu.VMEM((CMP_BLOCK * HEAD_DIM, hid), BF16)]
        + [pltpu.VMEM((group * PAGE, HEAD_DIM), F32) for _ in range(N_KV)],
    )
    return pl.pallas_call(
        functools.partial(_compress_kernel, group=group, ppt=ppt),
        grid_spec=grid_spec,
        out_shape=jax.ShapeDtypeStruct((bsz, 2, n_pg * per_page, gw), F32),
        compiler_params=_cparams(3),
    )(table.reshape(-1), *([pages] * ppt), cmp_pos, cmp_w1, cmp_w2)


def _nsa_cmp_kernel(q_ref, kc_ref, vc_ref, gate_ref, o_ref, sel_ref, score_s, rank_s, *, q_offset, n_sel_blocks,
                    tq, half):
    qt = pl.program_id(1)
    ncp = kc_ref.shape[1]
    nsp = sel_ref.shape[2]
    scale = HEAD_DIM ** -0.5
    n_heads = q_ref.shape[2] // HEAD_DIM
    rpg = n_heads // N_KV
    qpos = q_offset + qt * tq + lax.broadcasted_iota(jnp.int32, (1, tq), 1)
    cidx = lax.broadcasted_iota(jnp.int32, (ncp, 1), 0)
    corig = jnp.where(cidx < half, 2 * cidx, 2 * (cidx - half) + 1)
    cmask = ((corig + 1) * CMP_BLOCK - 1) <= qpos
    blk = lax.broadcasted_iota(jnp.int32, (nsp, 1), 0)
    cur = qpos // SEL_BLOCK
    forced = (blk == 0) | (blk == cur) | (blk == cur - 1)
    valid = (blk * SEL_BLOCK) <= qpos
    gate = gate_ref[0]
    HD = range(n_heads)
    kc = [kc_ref[0, :, g * HEAD_DIM:(g + 1) * HEAD_DIM] for g in range(N_KV)]
    vc = [vc_ref[0, :, g * HEAD_DIM:(g + 1) * HEAD_DIM] for g in range(N_KV)]
    s = [_dot(kc[hd // rpg], q_ref[0, :, hd * HEAD_DIM:(hd + 1) * HEAD_DIM], 1, _NT) * scale for hd in HD]
    s = [jnp.where(cmask, z, NEG) for z in s]
    e = [jnp.where(cmask, jnp.exp(z - jnp.max(z, axis=0, keepdims=True)), 0.0) for z in s]
    p = [z / jnp.maximum(jnp.sum(z, axis=0, keepdims=True), 1e-30) for z in e]
    oc = [_dot(p[hd], vc[hd // rpg], 1, _TN) for hd in HD]
    for hd in HD:
        o_ref[0, :, hd * HEAD_DIM:(hd + 1) * HEAD_DIM] = gate[:, hd * 3:hd * 3 + 1] * oc[hd]
    scores = []
    for g in range(N_KV):
        imp = p[g * rpg]
        for h in range(1, rpg):
            imp = imp + p[g * rpg + h]
        imp_blk = imp[:half] + imp[half:]
        if nsp > half:
            imp_blk = jnp.concatenate([imp_blk, jnp.zeros((nsp - half, tq), F32)], axis=0)
        else:
            imp_blk = imp_blk[:nsp]
        sc = jnp.where(valid, imp_blk + FORCED_BONUS * forced.astype(F32), -1.0)
        scores.append(jnp.where(blk < n_sel_blocks, sc, -2.0))
    score = jnp.concatenate(scores, axis=1)
    score_s[...] = score
    rank_s[...] = jnp.zeros(rank_s.shape, jnp.int32)

    def body(j, carry):
        rowv = score_s[pl.ds(j, 1), :]
        rank_s[...] = rank_s[...] + jnp.where(blk > j, (rowv >= score).astype(jnp.int32),
                                              (rowv > score).astype(jnp.int32))
        return carry

    lax.fori_loop(0, n_sel_blocks, body, 0)
    sel = (rank_s[...] < N_SEL).astype(F32)
    for g in range(N_KV):
        sel_ref[0, g] = sel[:, g * tq:(g + 1) * tq]


def nsa_cmp(q, kc, vc, gates, *, q_offset, n_sel_blocks, tq):
    bsz, t, dq = q.shape
    ncp = kc.shape[1]
    half = ncp // 2
    nsp = _round_up(n_sel_blocks, SUBLANES)
    gw = N_KV * HEAD_DIM
    qspec = pl.BlockSpec((1, tq, dq), lambda b, i: (b, i, 0))
    cspec = pl.BlockSpec((1, ncp, gw), lambda b, i: (b, 0, 0))
    return pl.pallas_call(
        functools.partial(_nsa_cmp_kernel, q_offset=q_offset, n_sel_blocks=n_sel_blocks, tq=tq, half=half),
        grid=(bsz, t // tq),
        in_specs=[qspec, cspec, cspec, pl.BlockSpec((1, tq, gates.shape[2]), lambda b, i: (b, i, 0))],
        out_specs=[qspec, pl.BlockSpec((1, N_KV, nsp, tq), lambda b, i: (b, 0, 0, i))],
        out_shape=[jax.ShapeDtypeStruct((bsz, t, dq), F32), jax.ShapeDtypeStruct((bsz, N_KV, nsp, t), F32)],
        scratch_shapes=[pltpu.VMEM((nsp, N_KV * tq), F32), pltpu.VMEM((nsp, N_KV * tq), jnp.int32)],
        compiler_params=_cparams(2),
    )(q, kc, vc, gates)


def _nsa_attn_kernel(*refs, mode, q_offset, tq, n_kt, ppt, has_tail, tail_pos0, branch):
    tbl_ref, kp0_ref = refs[0], refs[1]
    it = iter(refs[2:])
    q_ref = next(it)
    k_refs = [next(it) for _ in range(ppt)]
    v_refs = [next(it) for _ in range(ppt)]
    tk_ref = next(it) if has_tail else None
    tv_ref = next(it) if has_tail else None
    sel_ref = next(it) if mode == "sel" else None
    gate_ref = next(it)
    prev_ref = next(it)
    o_ref = next(it)
    m_s, l_s, acc_s = next(it), next(it), next(it)

    qt = pl.program_id(1)
    j = pl.program_id(2)
    n_heads = q_ref.shape[2] // HEAD_DIM
    rpg = n_heads // N_KV
    scale = HEAD_DIM ** -0.5
    width = WINDOW if mode == "win" else (1 << 30)
    n_steps = n_kt // ppt
    G = range(N_KV)

    @pl.when(j == 0)
    def _():
        m_s[...] = jnp.full(m_s.shape, NEG, F32)
        l_s[...] = jnp.zeros(l_s.shape, F32)
        acc_s[...] = jnp.zeros(acc_s.shape, F32)

    q_first = q_offset + qt * tq
    qpos = q_first + lax.broadcasted_iota(jnp.int32, (tq, 1), 0)

    def tile(kt_refs, vt_refs, kp0s):
        lane = lax.broadcasted_iota(jnp.int32, (1, PAGE), 1)
        kpos = jnp.concatenate([jnp.where(kp >= 0, kp, 1 << 30) + lane for kp in kp0s], axis=1)
        dist = qpos - kpos
        band = (dist >= 0) & (dist < width)
        if mode == "sel":
            nsp = sel_ref.shape[2]
            blk = lax.broadcasted_iota(jnp.int32, (nsp, 1), 0)
            expand = (blk == kpos // SEL_BLOCK).astype(BF16)
            hits = [_dg(sel_ref[0, g].astype(BF16), expand, _TN) for g in G]
            masks = [band & (hits[g] > 0.5) for g in G]
        else:
            masks = [band for g in G]
        masks = [jnp.concatenate([mk] * rpg, axis=0) for mk in masks]
        qg = [jnp.concatenate([q_ref[0, :, (g * rpg + h) * HEAD_DIM:(g * rpg + h + 1) * HEAD_DIM] for h in range(rpg)],
                              axis=0) for g in G]
        kg = [jnp.concatenate([_group_tile(r, g) for r in kt_refs], axis=0) for g in G]
        vg = [jnp.concatenate([_group_tile(r, g) for r in vt_refs], axis=0) for g in G]
        s = [_dot(qg[g], kg[g], 1, _NT) * scale for g in G]
        s = [jnp.where(masks[g], s[g], NEG) for g in G]
        m_old = [m_s[g] for g in G]
        m_new = [jnp.maximum(m_old[g], jnp.max(s[g], -1, keepdims=True)) for g in G]
        alpha = [jnp.exp(m_old[g] - m_new[g]) for g in G]
        p = [jnp.where(masks[g], jnp.exp(s[g] - m_new[g]), 0.0) for g in G]
        pv = [_dot(p[g], vg[g], 1) for g in G]
        for g in G:
            l_s[g] = alpha[g] * l_s[g] + jnp.sum(p[g], -1, keepdims=True)
            acc_s[g] = alpha[g] * acc_s[g] + pv[g]
            m_s[g] = m_new[g]

    base = (pl.program_id(0) * pl.num_programs(1) + qt) * n_kt + jnp.minimum(j, n_steps - 1) * ppt
    kp0s = [kp0_ref[base + pi] for pi in range(ppt)]
    live = None
    for kp in kp0s:
        ok = (kp >= 0) & (kp <= q_first + tq - 1) & (kp + PAGE - 1 > q_first - width)
        live = ok if live is None else (live | ok)
    if has_tail:
        live = live & (j < n_steps)

    @pl.when(live)
    def _():
        tile(k_refs, v_refs, kp0s)

    if has_tail:
        @pl.when(j == n_steps)
        def _():
            tile([tk_ref], [tv_ref], [tail_pos0])

    @pl.when(j == pl.num_programs(2) - 1)
    def _():
        gate = gate_ref[0]
        for g in range(N_KV):
            o = acc_s[g] / jnp.maximum(l_s[g], 1e-30)
            for h in range(rpg):
                hd = g * rpg + h
                cs = slice(hd * HEAD_DIM, (hd + 1) * HEAD_DIM)
                o_ref[0, :, cs] = prev_ref[0, :, cs] + gate[:, hd * 3 + branch:hd * 3 + branch + 1] * o[h * tq:(h + 1) * tq]


def nsa_attn(q, pages, tbl, kp0, slot_k, slot_v, tail, sel, gates, prev, *, mode, q_offset, tq, tail_pos0, branch,
             ppt=1):
    bsz, t, dq = q.shape
    n_qt = t // tq
    n_kt = tbl.shape[2]
    assert n_kt % ppt == 0
    gw = N_KV * HEAD_DIM
    n_heads = dq // HEAD_DIM
    rows = (n_heads // N_KV) * tq
    has_tail = tail is not None
    n_steps = n_kt // ppt + (1 if has_tail else 0)

    def page_idx(pi):
        return lambda b, i, j, tb, kp: tb[(b * n_qt + i) * n_kt + jnp.minimum(j, n_kt // ppt - 1) * ppt + pi]

    qspec = pl.BlockSpec((1, tq, dq), lambda b, i, j, tb, kp: (b, i, 0))
    in_specs = [qspec]
    in_specs += [_page_spec(pages, page_idx(pi), lambda *a: slot_k) for pi in range(ppt)]
    in_specs += [_page_spec(pages, page_idx(pi), lambda *a: slot_v) for pi in range(ppt)]
    args = [q] + [pages] * (2 * ppt)
    if has_tail:
        in_specs += [pl.BlockSpec((None, PAGE, gw), lambda b, i, j, tb, kp: (b, 0, 0)),
                     pl.BlockSpec((None, PAGE, gw), lambda b, i, j, tb, kp: (b, 0, 1))]
        args += [tail, tail]
    if mode == "sel":
        nsp = sel.shape[2]
        in_specs.append(pl.BlockSpec((1, N_KV, nsp, tq), lambda b, i, j, tb, kp: (b, 0, 0, i)))
        args.append(sel)
    in_specs += [pl.BlockSpec((1, tq, gates.shape[2]), lambda b, i, j, tb, kp: (b, i, 0)), qspec]
    args += [gates, prev]
    grid_spec = pltpu.PrefetchScalarGridSpec(
        num_scalar_prefetch=2,
        grid=(bsz, n_qt, n_steps),
        in_specs=in_specs,
        out_specs=qspec,
        scratch_shapes=[pltpu.VMEM((N_KV, rows, 1), F32), pltpu.VMEM((N_KV, rows, 1), F32),
                        pltpu.VMEM((N_KV, rows, HEAD_DIM), F32)],
    )
    return pl.pallas_call(
        functools.partial(_nsa_attn_kernel, mode=mode, q_offset=q_offset, tq=tq, n_kt=n_kt, ppt=ppt,
                          has_tail=has_tail, tail_pos0=tail_pos0, branch=branch),
        grid_spec=grid_spec,
        out_shape=jax.ShapeDtypeStruct((bsz, t, dq), F32),
        compiler_params=_cparams(3),
    )(tbl.reshape(-1).astype(jnp.int32), kp0.reshape(-1).astype(jnp.int32), *args)


def _softplus(z):
    return jnp.maximum(z, 0.0) + jnp.log(1.0 + jnp.exp(-jnp.abs(z)))


def _post_logdecay(acc, w0):
    return -jnp.exp(-_softplus(-(w0 + acc)) - 0.5)


def _post_sig_bias(acc, b):
    return jax.nn.sigmoid(acc + b)


def _post_vres(acc, v0, v, vf):
    return v + (vf - v) * jax.nn.sigmoid(v0 + acc)


def _post_ple(acc, t, b, x):
    return x + jax.nn.sigmoid(acc + b) * t


def _pad_rows(x, m):
    return jnp.pad(x, ((0, m - x.shape[0]),) + ((0, 0),) * (x.ndim - 1))


def kernel(x_prompt, x_sample, cache_kv_pages, cache_win_kv, state_wkv, state_shift, page_table, p_prompt, p_sample, ln_g, ln_b, rw_mix, rw_w_rkv, rw_w0, rw_w1, rw_w2, rw_a0, rw_a1, rw_a2, rw_v0, rw_v1, rw_v2, rw_g1, rw_g2, rw_k_k, rw_k_a, rw_r_k, rw_lnx_g, rw_lnx_b, rw_w_o, kv_ln_g, kv_ln_b, w_kv, cmp_pos, cmp_w1, cmp_w2, nsa_w_qg, nsa_b_g, nsa_w_o, moe_router, moe_bias, moe_w_in, moe_w_out, moe_ws_in, moe_ws_out, ple_proj, ple_gate, ple_gate_b):
    bp, tp, d = x_prompt.shape
    bs, ts, _ = x_sample.shape
    depth = ln_g.shape[0]
    n_a = state_wkv.shape[0]
    alpha = (2 * depth) ** 0.25

    def resid_ln(acc, g, b, x_rows):
        return _resid_ln(acc, g, b, x_rows, alpha)

    n_p, n_s = bp * tp, bs * ts
    n_tok = n_p + n_s
    m = _round_up(n_tok, 768) if n_tok > 768 else _round_up(n_tok, SUBLANES)
    ts_pad = _round_up(ts, SUBLANES)
    hh = d // RW_HEAD
    gw = N_KV * HEAD_DIM
    n_pg_s = page_table.shape[1]
    past_len = n_pg_s * cache_kv_pages.shape[1]
    win_buf = cache_win_kv.shape[1]
    assert cache_kv_pages.shape[1] == PAGE and tp % PAGE == 0 and win_buf % PAGE == 0 and ts < CMP_BLOCK

    def join(a_p, a_s):
        return _pad_rows(jnp.concatenate([a_p.reshape(n_p, -1), a_s.reshape(n_s, -1)], axis=0), m)

    def split_seq(z):
        zp = z[:n_p].reshape(bp, tp, -1)
        zs = jnp.pad(z[n_p:n_tok].reshape(bs, ts, -1), ((0, 0), (0, ts_pad - ts), (0, 0)))
        return zp, zs

    def merge_seq(zp, zs):
        return join(zp, zs[:, :ts])

    x = join(x_prompt, x_sample)
    pe = jnp.concatenate([p_prompt.reshape(depth, n_p, -1), p_sample.reshape(depth, n_s, -1)], axis=1)
    pe = jnp.pad(pe, ((0, 0), (0, m - n_tok), (0, 0)))

    shifts_p, shifts_s, wkvs_p, wkvs_s = [], [], [], []
    v_first = None
    ctx = None
    kv_rows_p = kv_rows_s = win_p = win_s = None
    for i in range(depth):
        if i < n_a:
            scan_passes = 3 if i == 0 else 1
            xp_, xs_ = x[:n_p].reshape(bp, tp, d), x[n_p:n_tok].reshape(bs, ts, d)
            shifts_p.append(xp_[:, -1])
            shifts_s.append(xs_[:, -1])
            prev = join(jnp.concatenate([jnp.zeros((bp, 1, d), F32), xp_[:, :-1]], axis=1),
                        jnp.concatenate([state_shift[i][:, None, :], xs_[:, :-1]], axis=1))
            mix = rw_mix[i]
            lin = linear
            r = lin(x, rw_w_rkv, widx=(i, 0), mix=(prev, mix[0]))
            k = lin(x, rw_w_rkv, widx=(i, 1), mix=(prev, mix[2]))
            v = lin(x, rw_w_rkv, widx=(i, 2), mix=(prev, mix[3]))
            lw = lin(lin(x, rw_w1, widx=(i,), mix=(prev, mix[1]), post=jnp.tanh), rw_w2, widx=(i,), rows=(rw_w0[i],),
                     post=_post_logdecay)
            if i == 0:
                v_first = v
            else:
                v = lin(lin(x, rw_v1, widx=(i - 1,), mix=(prev, mix[3])), rw_v2, widx=(i - 1,), rows=(rw_v0[i - 1],),
                        fulls=(v, v_first), post=_post_vres)
            a = lin(lin(x, rw_a1, widx=(i,), mix=(prev, mix[4])), rw_a2, widx=(i,), rows=(rw_a0[i],),
                    post=_post_sig_bias)
            g = lin(lin(x, rw_g1, widx=(i,), mix=(prev, mix[5]), post=jax.nn.sigmoid), rw_g2, widx=(i,))
            proj = (r, lw, k, v, a, g)
            vecs = (rw_k_k[i], rw_k_a[i], rw_r_k[i].reshape(-1), rw_lnx_g[i], rw_lnx_b[i])
            y_p, s_p = wkv7(*proj, *vecs, jnp.zeros((bp, hh, RW_HEAD, RW_HEAD), F32), seq_len=tp,
                            passes=scan_passes)
            y_s, s_s = wkv7_seq(*[split_seq(z)[1] for z in proj], *vecs, state_wkv[i], steps=ts)
            wkvs_p.append(s_p)
            wkvs_s.append(s_s)
            x = lin(merge_seq(y_p, y_s), rw_w_o, widx=(i,), rows=(ln_g[i, 0], ln_b[i, 0]), fulls=(x,), post=resid_ln,
                    full_rows=True)
        else:
            jn = i - n_a
            if ctx is None:
                rows = linear(layer_norm(x, kv_ln_g, kv_ln_b), w_kv)
                rows_p = rows[:n_p].reshape(bp, tp, 6 * gw)
                rows_s = rows[n_p:n_tok].reshape(bs, ts, 6 * gw)
                kv_rows_p = rows_p[:, :, :4 * gw].reshape(bp, tp, 4, N_KV, HEAD_DIM)
                kv_rows_s = rows_s[:, :, :4 * gw].reshape(bs, ts, 4, N_KV, HEAD_DIM)
                win_p = rows_p[:, max(tp - WINDOW, 0):, 4 * gw:].reshape(bp, -1, 2, N_KV, HEAD_DIM)
                new_win = rows_s[:, :, 4 * gw:].reshape(bs, ts, 2, N_KV, HEAD_DIM)
                win_s = jnp.concatenate([cache_win_kv, new_win], axis=1)[:, ts:]
                n_pg_p = tp // PAGE
                pages_p = rows[:n_p].reshape(bp * n_pg_p, PAGE, 6 * gw)
                table_p = jnp.arange(bp * n_pg_p, dtype=jnp.int32).reshape(bp, n_pg_p)
                pages_s = cache_kv_pages
                pages_w = cache_win_kv.reshape(bs * (win_buf // PAGE), PAGE, 2, N_KV, HEAD_DIM)

                def perm(c):
                    return jnp.concatenate([c[:, 0::2], c[:, 1::2]], axis=1)

                cmp_p = compress(pages_p, table_p, 0, cmp_pos, cmp_w1, cmp_w2)
                cmp_s = compress(pages_s, page_table.astype(jnp.int32), 0, cmp_pos, cmp_w1, cmp_w2)
                tail_sel = jnp.pad(rows_s[:, :, 2 * gw:4 * gw], ((0, 0), (0, PAGE - ts), (0, 0)))
                tail_win = jnp.pad(rows_s[:, :, 4 * gw:6 * gw], ((0, 0), (0, PAGE - ts), (0, 0)))
                tq_p = PAGE
                n_qt = tp // tq_p
                qi = jnp.arange(n_qt, dtype=jnp.int32)[:, None]
                kj = jnp.arange(n_pg_p, dtype=jnp.int32)[None, :]
                base = (jnp.arange(bp, dtype=jnp.int32) * n_pg_p)[:, None, None]
                tbl_sel_p = base + jnp.minimum(kj, qi)[None]
                kp0_sel_p = jnp.broadcast_to(jnp.where(kj <= qi, kj * PAGE, -1)[None], tbl_sel_p.shape)
                n_wt = WINDOW // PAGE + 1
                kw = qi - (n_wt - 1) + jnp.arange(n_wt, dtype=jnp.int32)[None, :]
                tbl_win_p = base + jnp.maximum(kw, 0)[None]
                kp0_win_p = jnp.broadcast_to(jnp.where(kw >= 0, kw * PAGE, -1)[None], tbl_win_p.shape)
                tbl_sel_s = page_table.astype(jnp.int32)[:, None, :]
                kp0_sel_s = jnp.broadcast_to((jnp.arange(n_pg_s, dtype=jnp.int32) * PAGE)[None, None], tbl_sel_s.shape)
                n_ww = win_buf // PAGE
                tbl_win_s = jnp.arange(bs * n_ww, dtype=jnp.int32).reshape(bs, 1, n_ww)
                kp0_win_s = jnp.broadcast_to(
                    (past_len - win_buf + jnp.arange(n_ww, dtype=jnp.int32) * PAGE)[None, None], tbl_win_s.shape)
                ctx = dict(kc_p=perm(cmp_p[:, 0]), vc_p=perm(cmp_p[:, 1]), kc_s=perm(cmp_s[:, 0]), vc_s=perm(cmp_s[:, 1]))
            dq = nsa_w_qg.shape[2] - nsa_b_g.shape[1]
            q = linear(x, nsa_w_qg, widx=(jn,), ncols=dq)
            gates = linear(x, nsa_w_qg[jn][:, dq:], rows=(nsa_b_g[jn],), post=_post_sig_bias)
            (q_p, q_s), (g_p, g_s) = split_seq(q), split_seq(gates)
            ns_p = -(-tp // SEL_BLOCK)
            ns_s = -(-(past_len + ts) // SEL_BLOCK)
            o_p, sel_p = nsa_cmp(q_p, ctx['kc_p'], ctx['vc_p'], g_p, q_offset=0, n_sel_blocks=ns_p, tq=tq_p)
            o_p = nsa_attn(q_p, pages_p, tbl_sel_p, kp0_sel_p, 2, 3, None, sel_p, g_p, o_p, mode="sel", q_offset=0,
                           tq=tq_p, tail_pos0=0, branch=1, ppt=_pick_tile(tp // PAGE, (4, 2, 1)))
            o_p = nsa_attn(q_p, pages_p, tbl_win_p, kp0_win_p, 4, 5, None, None, g_p, o_p, mode="win", q_offset=0,
                           tq=tq_p, tail_pos0=0, branch=2, ppt=WINDOW // PAGE + 1)
            o_s, sel_s = nsa_cmp(q_s, ctx['kc_s'], ctx['vc_s'], g_s, q_offset=past_len, n_sel_blocks=ns_s, tq=ts_pad)
            o_s = nsa_attn(q_s, pages_s, tbl_sel_s, kp0_sel_s, 2, 3, tail_sel, sel_s, g_s, o_s, mode="sel",
                           q_offset=past_len, tq=ts_pad, tail_pos0=past_len, branch=1,
                           ppt=_pick_tile(n_pg_s, (4, 2, 1)))
            o_s = nsa_attn(q_s, pages_w, tbl_win_s, kp0_win_s, 0, 1, tail_win, None, g_s, o_s, mode="win",
                           q_offset=past_len, tq=ts_pad, tail_pos0=past_len, branch=2,
                           ppt=_pick_tile(win_buf // PAGE, (4, 2, 1)))
            x = linear(merge_seq(o_p, o_s), nsa_w_o, widx=(jn,), rows=(ln_g[i, 0], ln_b[i, 0]), fulls=(x,),
                       post=resid_ln, full_rows=True)
        x = moe_ffn(x, n_tok, i, moe_router, moe_bias, moe_w_in, moe_w_out, moe_ws_in, moe_ws_out,
                    ln_g[i, 1], ln_b[i, 1], alpha)
        x = linear(x, ple_gate, widx=(i,), second=(pe, (i,), ple_proj, (i,)), rows=(ple_gate_b[i],), fulls=(x,),
                   post=_post_ple)

    y_p = x[:n_p].reshape(bp, tp, d)
    y_s = x[n_p:n_tok].reshape(bs, ts, d)
    return (y_p, y_s, kv_rows_p, kv_rows_s, win_p, win_s, jnp.stack(wkvs_p), jnp.stack(wkvs_s),
            jnp.stack(shifts_p), jnp.stack(shifts_s))
```

```python
import functools

import jax
import jax.numpy as jnp
from jax import lax
from jax.experimental import pallas as pl
from jax.experimental.pallas import tpu as pltpu

F32 = jnp.float32
BF16 = jnp.bfloat16

RW_HEAD = 64
GN_EPS = 64e-5
N_KV = 4
HEAD_DIM = 128
CMP_BLOCK = 32
SEL_BLOCK = 64
N_SEL = 16
WINDOW = 512
FORCED_BONUS = 100.0
N_GROUPS = 8
TOPK_GROUPS = 4
TOP_K = 8
ROUTED_SCALE = 2.5
LN_EPS = 1e-5
NEG = -1e30

LANES = 128
SUBLANES = 8
PAGE = 128
VMEM_LIMIT = 56 * 1024 * 1024
LINEAR_W_BYTES = 16 * 1024 * 1024
LINEAR_VMEM_BYTES = 46 * 1024 * 1024


def _cparams(n_axes):
    return pltpu.CompilerParams(dimension_semantics=("arbitrary",) * n_axes, vmem_limit_bytes=VMEM_LIMIT)


def _split(x):
    hi = x.astype(BF16)
    lo = (x - hi.astype(F32)).astype(BF16)
    return hi, lo


_NN = (((1,), (0,)), ((), ()))
_NT = (((1,), (1,)), ((), ()))
_TN = (((0,), (0,)), ((), ()))


def _dg(a, b, dims):
    return lax.dot_general(a, b, dims, preferred_element_type=F32)


def _dot(a, b, passes=1, dims=_NN):
    if passes == 1:
        return _dg(a.astype(BF16), b.astype(BF16), dims)
    ah, al = _split(a)
    bh, bl = _split(b)
    return _dg(ah, bh, dims) + (_dg(ah, bl, dims) + _dg(al, bh, dims))


def _round_up(x, m):
    return (x + m - 1) // m * m


def _pick_tile(n, prefs):
    for t in prefs:
        if n % t == 0:
            return t
    return n


def _linear_kernel(*refs, passes, has_mix, glu, has_second, n_row, n_full, post):
    it = iter(refs)
    x_ref = next(it)
    xp_ref = next(it) if has_mix else None
    mix_ref = next(it) if has_mix else None
    w_refs = [next(it) for _ in range(2 if glu else 1)]
    x2_ref = next(it) if has_second else None
    w2_ref = next(it) if has_second else None
    row_refs = [next(it) for _ in range(n_row)]
    full_refs = [next(it) for _ in range(n_full)]
    o_ref = next(it)
    scr = list(it)
    per = 2 if passes == 3 else 1

    @pl.when(pl.program_id(1) == 0)
    def _():
        for wi, w_ref in enumerate(w_refs):
            w = w_ref[...]
            if passes == 3:
                hi, lo = _split(w)
                scr[wi * per][...] = hi
                scr[wi * per + 1][...] = lo
            else:
                scr[wi * per][...] = w.astype(BF16)
        if has_second:
            scr[-1][...] = w2_ref[...].astype(BF16)

    x = x_ref[...]
    if has_mix:
        x = x + (xp_ref[...] - x) * mix_ref[...]
    if passes == 3:
        xh, xl = _split(x)
    else:
        xh = x.astype(BF16)
    accs = []
    for wi in range(len(w_refs)):
        wh = scr[wi * per][...]
        acc = jnp.dot(xh, wh, preferred_element_type=F32)
        if passes == 3:
            wl = scr[wi * per + 1][...]
            acc = acc + (jnp.dot(xh, wl, preferred_element_type=F32) + jnp.dot(xl, wh, preferred_element_type=F32))
        accs.append(acc)
    if has_second:
        accs.append(jnp.dot(x2_ref[...].astype(BF16), scr[-1][...], preferred_element_type=F32))
    extras = [r[...] for r in row_refs] + [r[...] for r in full_refs]
    o_ref[...] = post(*accs, *extras)


def _identity(acc):
    return acc


def linear(x, w, *, xidx=(), widx=(), ncols=None, passes=1, mix=None, glu=False, rows=(), fulls=(), post=_identity,
           full_rows=False, second=None, tm=None, tn=None):
    m, k = x.shape[-2:]
    n = (ncols or w.shape[-1]) // (2 if glu else 1)
    lead = (None,) * len(widx)
    n_w = 2 if glu else 1
    tn = tn or next((c for c in (n, n // 2, 512, 256, 128)
                     if n % c == 0 and (c % LANES == 0 or c == n) and n_w * k * c * 4 <= LINEAR_W_BYTES), n)
    k2 = second[0].shape[-1] if second is not None else 0
    w_bytes = n_w * k * tn * (4 + 2 * (2 if passes == 3 else 1)) + k2 * tn * 6
    per_row = ((2 if mix is not None else 1) * k + k2) * 4 * 2 + (1 + len(fulls)) * tn * 4 * 2
    tm = tm or next((c for c in (768, 512, 384, 256, 128, 64, 32, 16, 8)
                     if m % c == 0 and w_bytes + c * per_row <= LINEAR_VMEM_BYTES), m)
    nb = n // tn
    assert nb == 1 or not full_rows, "a row-wise epilogue needs the whole row in one block"
    grid = (nb, m // tm)
    x_spec = pl.BlockSpec((None,) * len(xidx) + (tm, k), lambda j, i: xidx + (i, 0))
    in_specs = [x_spec]
    args = [x]
    if mix is not None:
        in_specs += [x_spec, pl.BlockSpec((1, k), lambda j, i: (0, 0))]
        args += [mix[0], mix[1].reshape(1, k)]
    in_specs.append(pl.BlockSpec(lead + (k, tn), lambda j, i: widx + (0, j), pipeline_mode=pl.Buffered(1)))
    args.append(w)
    if glu:
        in_specs.append(pl.BlockSpec(lead + (k, tn), lambda j, i: widx + (0, j + nb), pipeline_mode=pl.Buffered(1)))
        args.append(w)
    if second is not None:
        x2, xidx2, w2, widx2 = second
        in_specs += [pl.BlockSpec((None,) * len(xidx2) + (tm, k2), lambda j, i: xidx2 + (i, 0)),
                     pl.BlockSpec((None,) * len(widx2) + (k2, tn), lambda j, i: widx2 + (0, j),
                                  pipeline_mode=pl.Buffered(1))]
        args += [x2, w2]
    for r in rows:
        in_specs.append(pl.BlockSpec((1, tn), lambda j, i: (0, j)))
        args.append(r.reshape(1, n))
    for f in fulls:
        in_specs.append(pl.BlockSpec((tm, tn), lambda j, i: (i, j)))
        args.append(f)
    scratch = [pltpu.VMEM((k, tn), BF16) for _ in range(n_w * (2 if passes == 3 else 1))]
    if second is not None:
        scratch.append(pltpu.VMEM((k2, tn), BF16))
    kern = functools.partial(_linear_kernel, passes=passes, has_mix=mix is not None, glu=glu,
                             has_second=second is not None, n_row=len(rows), n_full=len(fulls), post=post)
    return pl.pallas_call(
        kern,
        grid=grid,
        in_specs=in_specs,
        out_specs=pl.BlockSpec((tm, tn), lambda j, i: (i, j)),
        out_shape=jax.ShapeDtypeStruct((m, n), F32),
        scratch_shapes=scratch,
        compiler_params=_cparams(2),
    )(*args)


def _resid_ln(h, g, b, x, alpha):
    z = alpha * x + h
    mu = jnp.mean(z, -1, keepdims=True)
    zc = z - mu
    var = jnp.mean(zc * zc, -1, keepdims=True)
    return zc * lax.rsqrt(var + LN_EPS) * g + b


def _ln_kernel(*refs, alpha, has_h):
    if has_h:
        x_ref, h_ref, g_ref, b_ref, o_ref = refs
        x = alpha * x_ref[...] + h_ref[...]
    else:
        x_ref, g_ref, b_ref, o_ref = refs
        x = x_ref[...]
    mu = jnp.mean(x, -1, keepdims=True)
    xc = x - mu
    var = jnp.mean(xc * xc, -1, keepdims=True)
    o_ref[...] = xc * lax.rsqrt(var + LN_EPS) * g_ref[...] + b_ref[...]


def layer_norm(x, g, b, h=None, alpha=1.0):
    m, d = x.shape
    tm = _pick_tile(m, (384, 256, 128, 64, 32, 16, 8))
    spec = pl.BlockSpec((tm, d), lambda i: (i, 0))
    vspec = pl.BlockSpec((1, d), lambda i: (0, 0))
    args = [x] + ([h] if h is not None else []) + [g.reshape(1, d), b.reshape(1, d)]
    in_specs = [spec] + ([spec] if h is not None else []) + [vspec, vspec]
    return pl.pallas_call(
        functools.partial(_ln_kernel, alpha=alpha, has_h=h is not None),
        grid=(m // tm,),
        in_specs=in_specs,
        out_specs=spec,
        out_shape=jax.ShapeDtypeStruct((m, d), F32),
        compiler_params=_cparams(1),
    )(*args)


def _bf16r(x):
    return x.astype(BF16).astype(F32)


def _matvec_bf16(s, vec_row):
    return jnp.sum(_bf16r(s) * _bf16r(vec_row), axis=1, keepdims=True)


def _row_from_col(col, eye):
    return jnp.sum(eye * col, axis=0, keepdims=True)


def _col_from_row(row, eye):
    return jnp.sum(eye * row, axis=1, keepdims=True)


def _wkv_out(y, v_h, rk_h, lg, lb):
    mu = jnp.mean(y, -1, keepdims=True)
    yc = y - mu
    var = jnp.mean(yc * yc, -1, keepdims=True)
    return yc * lax.rsqrt(var + GN_EPS) * lg + lb + jnp.sum(rk_h, -1, keepdims=True) * v_h


def _wkv_seq_kernel(r_ref, lw_ref, k_ref, v_ref, a_ref, g_ref, kk_ref, ka_ref, rk_ref, lg_ref, lb_ref, s0_ref,
                    y_ref, s_ref, *, steps, heads):
    N = RW_HEAD
    r = r_ref[0]
    k = k_ref[0]
    v = v_ref[0]
    a = a_ref[0]
    w = jnp.exp(lw_ref[0])
    kkw = k * kk_ref[...]
    k2 = k * (1.0 + (a - 1.0) * ka_ref[...])
    rk = r * k2 * rk_ref[...]
    eye = (lax.broadcasted_iota(jnp.int32, (N, N), 0) == lax.broadcasted_iota(jnp.int32, (N, N), 1)).astype(F32)
    tpad = r.shape[0]
    H = range(heads)
    sls = [slice(h * N, (h + 1) * N) for h in H]
    kk = [kkw[:, sl] for sl in sls]
    kk = [z * lax.rsqrt(jnp.maximum(jnp.sum(z * z, -1, keepdims=True), 1e-24)) for z in kk]
    b_h = [kk[h] * a[:, sls[h]] for h in H]
    s = [s0_ref[0, h] for h in H]
    ys = [[] for _ in H]
    for t in range(steps):
        sa = [_matvec_bf16(s[h], -kk[h][t:t + 1]) for h in H]
        vcol = [_col_from_row(v[t:t + 1, sls[h]], eye) for h in H]
        s = [s[h] * w[t:t + 1, sls[h]] + sa[h] * b_h[h][t:t + 1] + vcol[h] * k2[t:t + 1, sls[h]] for h in H]
        yt = [_row_from_col(_matvec_bf16(s[h], r[t:t + 1, sls[h]]), eye) for h in H]
        for h in H:
            ys[h].append(yt[h])
    for h in H:
        s_ref[0, h] = s[h]
    y = [jnp.concatenate(ys[h] + [jnp.zeros((tpad - steps, N), F32)], axis=0) for h in H]
    outs = [_wkv_out(y[h], v[:, sls[h]], rk[:, sls[h]], lg_ref[:, sls[h]], lb_ref[:, sls[h]]) for h in H]
    y_ref[0] = jnp.concatenate(outs, axis=1) * g_ref[0]


def wkv7_seq(r, lw, k, v, a, g, k_k, k_a, r_k, lnx_g, lnx_b, s0, *, steps):
    bsz, t, d = r.shape
    n = RW_HEAD
    hh = d // n
    heads = _pick_tile(hh, (4, 2))
    wide = heads * n
    seq = pl.BlockSpec((1, t, wide), lambda b, hb: (b, 0, hb))
    vec = pl.BlockSpec((1, wide), lambda b, hb: (0, hb))
    st = pl.BlockSpec((1, heads, n, n), lambda b, hb: (b, hb, 0, 0))
    vecs = [z.reshape(1, d) for z in (k_k, k_a, r_k, lnx_g, lnx_b)]
    return pl.pallas_call(
        functools.partial(_wkv_seq_kernel, steps=steps, heads=heads),
        grid=(bsz, hh // heads),
        in_specs=[seq] * 6 + [vec] * 5 + [st],
        out_specs=[seq, st],
        out_shape=[jax.ShapeDtypeStruct((bsz, t, d), F32), jax.ShapeDtypeStruct((bsz, hh, n, n), F32)],
        compiler_params=_cparams(2),
    )(r, lw, k, v, a, g, *vecs, s0)


def _wkv_kernel(r_ref, lw_ref, k_ref, v_ref, a_ref, g_ref, kk_ref, ka_ref, rk_ref, lg_ref, lb_ref, s0_ref,
                y_ref, s_ref, *, chunk, heads, passes):
    L, N = chunk, RW_HEAD

    @pl.when(pl.program_id(2) == 0)
    def _():
        s_ref[...] = s0_ref[...]

    r = r_ref[...]
    lw = lw_ref[...]
    k = k_ref[...]
    v = v_ref[...]
    a = a_ref[...]
    kkw = k * kk_ref[...]
    k2 = k * (1.0 + (a - 1.0) * ka_ref[...])
    rk = r * k2 * rk_ref[...]

    row = lax.broadcasted_iota(jnp.int32, (L, L), 0)
    col = lax.broadcasted_iota(jnp.int32, (L, L), 1)
    tri_incl = (col <= row)
    tri_strict = (col < row)
    tri = tri_incl.astype(BF16)
    l1 = lw.astype(BF16)
    rem = lw - l1.astype(F32)
    l2 = rem.astype(BF16)
    l3 = (rem - l2.astype(F32)).astype(BF16)
    cs = (jnp.dot(tri, l1, preferred_element_type=F32) + jnp.dot(tri, l2, preferred_element_type=F32)
          + jnp.dot(tri, l3, preferred_element_type=F32))
    cs_last = cs[L - 1:L, :]
    e_prev = jnp.exp(cs - lw)
    e_inv = jnp.exp(-cs)
    e_fwd = jnp.exp(cs)
    e_tail = jnp.exp(cs_last - cs)
    gam = jnp.exp(cs_last)

    n_dbl = max(1, (L - 1).bit_length())
    eye = (lax.broadcasted_iota(jnp.int32, (N, N), 0) == lax.broadcasted_iota(jnp.int32, (N, N), 1)).astype(F32)
    H = range(heads)
    sls = [slice(h * N, (h + 1) * N) for h in H]
    kk = [kkw[:, sl] for sl in sls]
    kk = [z * lax.rsqrt(jnp.maximum(jnp.sum(z * z, -1, keepdims=True), 1e-24)) for z in kk]
    b_h = [kk[h] * a[:, sls[h]] for h in H]
    k_h = [k2[:, sl] for sl in sls]
    v_h = [v[:, sl] for sl in sls]
    lhs2 = [jnp.concatenate([-kk[h] * e_prev[:, sls[h]], r[:, sls[h]] * e_fwd[:, sls[h]]], axis=0) for h in H]
    rhs2 = [jnp.concatenate([b_h[h] * e_inv[:, sls[h]], k_h[h] * e_inv[:, sls[h]]], axis=0) for h in H]
    bk = [jnp.concatenate([b_h[h] * e_tail[:, sls[h]], k_h[h] * e_tail[:, sls[h]]], axis=0) for h in H]
    s0 = [s_ref[0, h] for h in H]
    p = [_dot(lhs2[h], rhs2[h], passes, _NT) for h in H]
    q = [_dot(lhs2[h], s0[h], passes, _NT) for h in H]
    pk = [jnp.where(tri_strict, p[h][:L, :L], 0.0) for h in H]
    a_k = [jnp.where(tri_strict, p[h][:L, L:], 0.0) for h in H]
    rbk = [jnp.concatenate([jnp.where(tri_incl, p[h][L:, :L], 0.0), jnp.where(tri_incl, p[h][L:, L:], 0.0)], axis=1)
           for h in H]
    x = [q[h][:L] + _dot(a_k[h], v_h[h], passes) for h in H]
    for d in range(n_dbl):
        x = [x[h] + _dot(pk[h], x[h], passes) for h in H]
        if d + 1 < n_dbl:
            pk = [_dot(pk[h], pk[h], passes) for h in H]
    uv = [jnp.concatenate([x[h], v_h[h]], axis=0) for h in H]
    y = [q[h][L:] + _dot(rbk[h], uv[h], passes) for h in H]
    s_new = [s0[h] * gam[:, sls[h]] + _dot(uv[h], bk[h], passes, _TN) for h in H]
    for h in H:
        s_ref[0, h] = s_new[h]
    last = lax.broadcasted_iota(jnp.int32, (L, 1), 0) == L - 1
    y = [jnp.where(last, _row_from_col(_matvec_bf16(s_new[h], r[L - 1:L, sls[h]]), eye), y[h]) for h in H]
    outs = [_wkv_out(y[h], v_h[h], rk[:, sls[h]], lg_ref[:, sls[h]], lb_ref[:, sls[h]]) for h in H]
    y_ref[...] = jnp.concatenate(outs, axis=1) * g_ref[...]


def wkv7(r, lw, k, v, a, g, k_k, k_a, r_k, lnx_g, lnx_b, s0, *, seq_len, passes):
    bsz, t = s0.shape[0], seq_len
    d = r.shape[1]
    n = RW_HEAD
    hh = d // n
    heads = _pick_tile(hh, (16, 4, 2))
    chunk = _pick_tile(t, (128, 64))
    wide = heads * n
    n_chunks = t // chunk
    seq = pl.BlockSpec((chunk, wide), lambda b, hb, c: (b * n_chunks + c, hb))
    vec = pl.BlockSpec((1, wide), lambda b, hb, c: (0, hb))
    st = pl.BlockSpec((1, heads, n, n), lambda b, hb, c: (b, hb, 0, 0))
    vecs = [z.reshape(1, d) for z in (k_k, k_a, r_k, lnx_g, lnx_b)]
    return pl.pallas_call(
        functools.partial(_wkv_kernel, chunk=chunk, heads=heads, passes=passes),
        grid=(bsz, hh // heads, n_chunks),
        in_specs=[seq] * 6 + [vec] * 5 + [st],
        out_specs=[seq, st],
        out_shape=[jax.ShapeDtypeStruct((bsz * t, d), F32), jax.ShapeDtypeStruct((bsz, hh, n, n), F32)],
        compiler_params=_cparams(3),
    )(r, lw, k, v, a, g, *vecs, s0)


def _route_kernel(x_ref, rt_ref, bias_ref, eidx_ref, w_ref, pos_ref, cnt_ref, *, n_valid):
    ne = rt_ref.shape[0]
    per = ne // N_GROUPS
    logits = _dot(rt_ref[...], x_ref[...], 1, _NT)
    s = jax.nn.sigmoid(logits)
    sb = s + bias_ref[...]
    tm = sb.shape[1]
    sub = lax.broadcasted_iota(jnp.int32, (per, tm), 0)
    gs_rows = []
    for gi in range(N_GROUPS):
        tile = sb[gi * per:(gi + 1) * per, :]
        m1 = jnp.max(tile, axis=0, keepdims=True)
        first = jnp.min(jnp.where(tile == m1, sub, per), axis=0, keepdims=True)
        m2 = jnp.max(jnp.where(sub == first, -jnp.inf, tile), axis=0, keepdims=True)
        gs_rows.append(m1 + m2)
    keep_rows = []
    for gi in range(N_GROUPS):
        cnt = jnp.zeros((1, tm), jnp.int32)
        for gj in range(N_GROUPS):
            if gj == gi:
                continue
            beats = (gs_rows[gj] > gs_rows[gi]) if gj > gi else (gs_rows[gj] >= gs_rows[gi])
            cnt = cnt + beats.astype(jnp.int32)
        keep_rows.append(jnp.broadcast_to(cnt, (per, tm)))
    masked = jnp.where(jnp.concatenate(keep_rows, axis=0) < TOPK_GROUPS, sb, -1e9)
    eidx = lax.broadcasted_iota(jnp.int32, (ne, tm), 0)
    rank = jnp.zeros((ne, tm), jnp.int32)
    for ej in range(ne):
        rowv = masked[ej:ej + 1, :]
        rank = rank + jnp.where(eidx > ej, (rowv >= masked).astype(jnp.int32), (rowv > masked).astype(jnp.int32))
    tok = pl.program_id(0) * tm + lax.broadcasted_iota(jnp.int32, (ne, tm), 1)
    rank = jnp.where(tok < n_valid, rank, ne)
    sel = rank < TOP_K
    self32 = sel.astype(F32)
    den = jnp.sum(jnp.where(sel, s, 0.0), axis=0, keepdims=True)
    wgt = jnp.where(sel, s / den * ROUTED_SCALE, 0.0)

    @pl.when(pl.program_id(0) == 0)
    def _():
        cnt_ref[...] = jnp.zeros(cnt_ref.shape, F32)

    upper = (lax.broadcasted_iota(jnp.int32, (tm, tm), 0) < lax.broadcasted_iota(jnp.int32, (tm, tm), 1)).astype(BF16)
    carry = cnt_ref[...]
    pos = jnp.dot(self32.astype(BF16), upper, preferred_element_type=F32) + carry[:, :1]
    cnt_ref[...] = carry + jnp.sum(self32, axis=1, keepdims=True)
    eidx_f = eidx.astype(F32)
    rows_e, rows_w, rows_p = [], [], []
    for kk in range(TOP_K):
        hit = rank == kk
        rows_e.append(jnp.sum(jnp.where(hit, eidx_f, 0.0), axis=0, keepdims=True))
        rows_w.append(jnp.sum(jnp.where(hit, wgt, 0.0), axis=0, keepdims=True))
        rows_p.append(jnp.sum(jnp.where(hit, pos, 0.0), axis=0, keepdims=True))
    eidx_ref[...] = jnp.concatenate(rows_e, axis=0).astype(jnp.int32)
    w_ref[...] = jnp.concatenate(rows_w, axis=0)
    pos_ref[...] = jnp.concatenate(rows_p, axis=0).astype(jnp.int32)


def route(x, router_t, bias, n_valid):
    m, d = x.shape
    ne = router_t.shape[0]
    tm = _pick_tile(m, (256, 128))
    out = pl.BlockSpec((TOP_K, tm), lambda i: (0, i))
    return pl.pallas_call(
        functools.partial(_route_kernel, n_valid=n_valid),
        grid=(m // tm,),
        in_specs=[pl.BlockSpec((tm, d), lambda i: (i, 0)), pl.BlockSpec((ne, d), lambda i: (0, 0)),
                  pl.BlockSpec((ne, 1), lambda i: (0, 0))],
        out_specs=[out, out, out, pl.BlockSpec((ne, LANES), lambda i: (0, 0))],
        out_shape=[jax.ShapeDtypeStruct((TOP_K, m), jnp.int32), jax.ShapeDtypeStruct((TOP_K, m), F32),
                   jax.ShapeDtypeStruct((TOP_K, m), jnp.int32), jax.ShapeDtypeStruct((ne, LANES), F32)],
        compiler_params=_cparams(1),
    )(x, router_t, bias.reshape(ne, 1))


def _experts_kernel(be_ref, nu_ref, xs_ref, win_ref, wout_ref, o_ref, win_s, wout_s):
    i = pl.program_id(0)
    f = wout_ref.shape[0]

    @pl.when(i < nu_ref[0])
    def _():
        changed = jnp.logical_or(i == 0, be_ref[i] != be_ref[jnp.maximum(i - 1, 0)])

        @pl.when(changed)
        def _():
            win_s[...] = win_ref[...].astype(BF16)
            wout_s[...] = wout_ref[...].astype(BF16)

        hu = jnp.dot(xs_ref[...].astype(BF16), win_s[...], preferred_element_type=F32)
        hid = jax.nn.silu(hu[:, :f]) * hu[:, f:]
        y = jnp.dot(hid.astype(BF16), wout_s[...], preferred_element_type=F32)
        o_ref[...] = y.astype(BF16)

    @pl.when(i >= nu_ref[0])
    def _():
        o_ref[...] = jnp.zeros(o_ref.shape, BF16)


def experts(xs, blk_expert, n_used, w_in, w_out, layer, tm):
    rws, d = xs.shape
    f2 = w_in.shape[3]
    f = f2 // 2
    grid_spec = pltpu.PrefetchScalarGridSpec(
        num_scalar_prefetch=2,
        grid=(rws // tm,),
        in_specs=[
            pl.BlockSpec((tm, d), lambda i, be, nu: (jnp.minimum(i, nu[0] - 1), 0)),
            pl.BlockSpec((None, None, d, f2), lambda i, be, nu: (layer, be[i], 0, 0)),
            pl.BlockSpec((None, None, f, d), lambda i, be, nu: (layer, be[i], 0, 0)),
        ],
        out_specs=pl.BlockSpec((tm, d), lambda i, be, nu: (i, 0)),
        scratch_shapes=[pltpu.VMEM((d, f2), BF16), pltpu.VMEM((f, d), BF16)],
    )
    return pl.pallas_call(
        _experts_kernel,
        grid_spec=grid_spec,
        out_shape=jax.ShapeDtypeStruct((rws, d), BF16),
        compiler_params=_cparams(1),
    )(blk_expert, n_used, xs, w_in, w_out)


def moe_ffn(x, n_valid, layer, router, bias, w_in, w_out, ws_in, ws_out, ln_g, ln_b, alpha):
    m, d = x.shape
    ne = router.shape[2]
    eidx, ew, posk, cnt = route(x, router[layer].T, bias[layer], n_valid)
    tm = 256 if m * TOP_K // ne >= 256 else _round_up(max(m * TOP_K // ne, SUBLANES), SUBLANES)
    nk = m * TOP_K
    counts = cnt[:, 0].astype(jnp.int32)
    padded = (counts + tm - 1) // tm * tm
    pad_end = jnp.cumsum(padded)
    pad_start = pad_end - padded
    n_blocks = (nk + tm - 1) // tm + ne
    n_rows = n_blocks * tm
    tok = jnp.arange(m, dtype=jnp.int32)
    first_row = jnp.sum(jnp.where(eidx[:, :, None] == jnp.arange(ne, dtype=jnp.int32), pad_start, 0), axis=-1)
    spare = n_rows - 1 - ((tok[None, :] - n_valid) * TOP_K + jnp.arange(TOP_K, dtype=jnp.int32)[:, None])
    dest = jnp.where(tok[None, :] < n_valid, first_row + posk, spare).astype(jnp.int32)
    token_of_row = jnp.zeros((n_rows,), jnp.int32).at[dest.reshape(-1)].set(jnp.tile(tok, TOP_K), unique_indices=True)
    blk_start = jnp.arange(n_blocks, dtype=jnp.int32) * tm
    blk_expert = jnp.minimum(jnp.sum((pad_end[None, :] <= blk_start[:, None]).astype(jnp.int32), axis=1), ne - 1)
    n_used = (pad_end[-1:] // tm).astype(jnp.int32)
    xs = jnp.take(x, token_of_row, axis=0, mode="clip")
    yb = experts(xs, blk_expert, n_used, w_in, w_out, layer, tm)
    wr = ew.astype(BF16).astype(F32)
    picked = yb.at[dest.T].get(mode="promise_in_bounds").astype(F32)
    routed = jnp.sum(picked * wr.T[:, :, None], axis=1)
    hid = linear(x, ws_in, widx=(layer,), glu=True, post=lambda a, b: jax.nn.silu(a) * b)
    return linear(hid, ws_out, widx=(layer,), rows=(ln_g, ln_b), fulls=(routed, x), full_rows=True,
                  post=lambda acc, g, b, rt, xr: _resid_ln(acc + rt, g, b, xr, alpha))


def _gelu_tanh(x):
    return 0.5 * x * (1.0 + jnp.tanh(0.7978845608028654 * (x + 0.044715 * x * x * x)))


def _group_tile(ref, g):
    if len(ref.shape) == 3:
        return ref[:, g, :]
    return ref[:, g * HEAD_DIM:(g + 1) * HEAD_DIM]


def _page_spec(pages, index_fn, slot_fn):
    if pages.ndim == 5:
        return pl.BlockSpec((None, PAGE, None, N_KV, HEAD_DIM), lambda *a: (index_fn(*a), 0, slot_fn(*a), 0, 0))
    return pl.BlockSpec((None, PAGE, N_KV * HEAD_DIM), lambda *a: (index_fn(*a), 0, slot_fn(*a)))


def _compress_kernel(tbl_ref, *refs, group, ppt):
    pg_refs = refs[:ppt]
    pos_ref, w1_ref, w2_ref, o_ref, w1_s = refs[ppt:ppt + 5]
    slabs = refs[ppt + 5:]
    p = pl.program_id(2)
    pin = p % (group // ppt)

    @pl.when(p == 0)
    def _():
        w1_s[...] = w1_ref[0].astype(BF16)

    for pi, pg_ref in enumerate(pg_refs):
        for g in range(N_KV):
            slabs[g][pl.ds(pl.multiple_of((pin * ppt + pi) * PAGE, PAGE), PAGE), :] = _group_tile(pg_ref, g)

    @pl.when(pin == group // ppt - 1)
    def _():
        per_page = PAGE // CMP_BLOCK
        nrow = group * per_page
        hid = w1_ref.shape[2]
        acc = jnp.zeros((N_KV * nrow, hid), F32)
        for t in range(CMP_BLOCK):
            lhs = jnp.concatenate([slabs[g][pl.ds(t, nrow, stride=CMP_BLOCK), :] for g in range(N_KV)], axis=0)
            lhs = (lhs + pos_ref[0, t:t + 1, :]).astype(BF16)
            acc = acc + jnp.dot(lhs, w1_s[t * HEAD_DIM:(t + 1) * HEAD_DIM, :], preferred_element_type=F32)
        out = _dot(_gelu_tanh(acc), w2_ref[0])
        for g in range(N_KV):
            o_ref[0, 0, :, g * HEAD_DIM:(g + 1) * HEAD_DIM] = out[g * nrow:(g + 1) * nrow]


def compress(pages, table, slot0, cmp_pos, cmp_w1, cmp_w2):
    bsz, n_pg = table.shape
    group = min(32, n_pg)
    ppt = _pick_tile(group, (4, 2, 1))
    gw = N_KV * HEAD_DIM
    per_page = PAGE // CMP_BLOCK
    hid = cmp_w1.shape[2]

    def page_of(pi):
        return lambda b, s, p, tbl: tbl[b * n_pg + p * ppt + pi]

    grid_spec = pltpu.PrefetchScalarGridSpec(
        num_scalar_prefetch=1,
        grid=(bsz, 2, n_pg // ppt),
        in_specs=[_page_spec(pages, page_of(pi), lambda b, s, p, tbl: slot0 + s) for pi in range(ppt)] + [
            pl.BlockSpec((1, CMP_BLOCK, HEAD_DIM), lambda b, s, p, tbl: (s, 0, 0)),
            pl.BlockSpec((1, CMP_BLOCK * HEAD_DIM, hid), lambda b, s, p, tbl: (s, 0, 0)),
            pl.BlockSpec((1, hid, HEAD_DIM), lambda b, s, p, tbl: (s, 0, 0)),
        ],
        out_specs=pl.BlockSpec((1, 1, group * per_page, gw), lambda b, s, p, tbl: (b, s, p // (group // ppt), 0)),
        scratch_shapes=[pltpu.VMEM((CMP_BLOCK * HEAD_DIM, hid), BF16)]
        + [pltpu.VMEM((group * PAGE, HEAD_DIM), F32) for _ in range(N_KV)],
    )
    return pl.pallas_call(
        functools.partial(_compress_kernel, group=group, ppt=ppt),
        grid_spec=grid_spec,
        out_shape=jax.ShapeDtypeStruct((bsz, 2, n_pg * per_page, gw), F32),
        compiler_params=_cparams(3),
    )(table.reshape(-1), *([pages] * ppt), cmp_pos, cmp_w1, cmp_w2)


def _nsa_cmp_kernel(q_ref, kc_ref, vc_ref, gate_ref, o_ref, sel_ref, score_s, rank_s, *, q_offset, n_sel_blocks,
                    tq, half):
    qt = pl.program_id(1)
    ncp = kc_ref.shape[1]
    nsp = sel_ref.shape[2]
    scale = HEAD_DIM ** -0.5
    n_heads = q_ref.shape[2] // HEAD_DIM
    rpg = n_heads // N_KV
    qpos = q_offset + qt * tq + lax.broadcasted_iota(jnp.int32, (1, tq), 1)
    cidx = lax.broadcasted_iota(jnp.int32, (ncp, 1), 0)
    corig = jnp.where(cidx < half, 2 * cidx, 2 * (cidx - half) + 1)
    cmask = ((corig + 1) * CMP_BLOCK - 1) <= qpos
    blk = lax.broadcasted_iota(jnp.int32, (nsp, 1), 0)
    cur = qpos // SEL_BLOCK
    forced = (blk == 0) | (blk == cur) | (blk == cur - 1)
    valid = (blk * SEL_BLOCK) <= qpos
    gate = gate_ref[0]
    HD = range(n_heads)
    kc = [kc_ref[0, :, g * HEAD_DIM:(g + 1) * HEAD_DIM] for g in range(N_KV)]
    vc = [vc_ref[0, :, g * HEAD_DIM:(g + 1) * HEAD_DIM] for g in range(N_KV)]
    s = [_dot(kc[hd // rpg], q_ref[0, :, hd * HEAD_DIM:(hd + 1) * HEAD_DIM], 1, _NT) * scale for hd in HD]
    s = [jnp.where(cmask, z, NEG) for z in s]
    e = [jnp.where(cmask, jnp.exp(z - jnp.max(z, axis=0, keepdims=True)), 0.0) for z in s]
    p = [z / jnp.maximum(jnp.sum(z, axis=0, keepdims=True), 1e-30) for z in e]
    oc = [_dot(p[hd], vc[hd // rpg], 1, _TN) for hd in HD]
    for hd in HD:
        o_ref[0, :, hd * HEAD_DIM:(hd + 1) * HEAD_DIM] = gate[:, hd * 3:hd * 3 + 1] * oc[hd]
    scores = []
    for g in range(N_KV):
        imp = p[g * rpg]
        for h in range(1, rpg):
            imp = imp + p[g * rpg + h]
        imp_blk = imp[:half] + imp[half:]
        if nsp > half:
            imp_blk = jnp.concatenate([imp_blk, jnp.zeros((nsp - half, tq), F32)], axis=0)
        else:
            imp_blk = imp_blk[:nsp]
        sc = jnp.where(valid, imp_blk + FORCED_BONUS * forced.astype(F32), -1.0)
        scores.append(jnp.where(blk < n_sel_blocks, sc, -2.0))
    score = jnp.concatenate(scores, axis=1)
    score_s[...] = score
    rank_s[...] = jnp.zeros(rank_s.shape, jnp.int32)

    def body(j, carry):
        rowv = score_s[pl.ds(j, 1), :]
        rank_s[...] = rank_s[...] + jnp.where(blk > j, (rowv >= score).astype(jnp.int32),
                                              (rowv > score).astype(jnp.int32))
        return carry

    lax.fori_loop(0, n_sel_blocks, body, 0)
    sel = (rank_s[...] < N_SEL).astype(F32)
    for g in range(N_KV):
        sel_ref[0, g] = sel[:, g * tq:(g + 1) * tq]


def nsa_cmp(q, kc, vc, gates, *, q_offset, n_sel_blocks, tq):
    bsz, t, dq = q.shape
    ncp = kc.shape[1]
    half = ncp // 2
    nsp = _round_up(n_sel_blocks, SUBLANES)
    gw = N_KV * HEAD_DIM
    qspec = pl.BlockSpec((1, tq, dq), lambda b, i: (b, i, 0))
    cspec = pl.BlockSpec((1, ncp, gw), lambda b, i: (b, 0, 0))
    return pl.pallas_call(
        functools.partial(_nsa_cmp_kernel, q_offset=q_offset, n_sel_blocks=n_sel_blocks, tq=tq, half=half),
        grid=(bsz, t // tq),
        in_specs=[qspec, cspec, cspec, pl.BlockSpec((1, tq, gates.shape[2]), lambda b, i: (b, i, 0))],
        out_specs=[qspec, pl.BlockSpec((1, N_KV, nsp, tq), lambda b, i: (b, 0, 0, i))],
        out_shape=[jax.ShapeDtypeStruct((bsz, t, dq), F32), jax.ShapeDtypeStruct((bsz, N_KV, nsp, t), F32)],
        scratch_shapes=[pltpu.VMEM((nsp, N_KV * tq), F32), pltpu.VMEM((nsp, N_KV * tq), jnp.int32)],
        compiler_params=_cparams(2),
    )(q, kc, vc, gates)


def _nsa_attn_kernel(*refs, mode, q_offset, tq, n_kt, ppt, has_tail, tail_pos0, branch):
    tbl_ref, kp0_ref = refs[0], refs[1]
    it = iter(refs[2:])
    q_ref = next(it)
    k_refs = [next(it) for _ in range(ppt)]
    v_refs = [next(it) for _ in range(ppt)]
    tk_ref = next(it) if has_tail else None
    tv_ref = next(it) if has_tail else None
    sel_ref = next(it) if mode == "sel" else None
    gate_ref = next(it)
    prev_ref = next(it)
    o_ref = next(it)
    m_s, l_s, acc_s = next(it), next(it), next(it)

    qt = pl.program_id(1)
    j = pl.program_id(2)
    n_heads = q_ref.shape[2] // HEAD_DIM
    rpg = n_heads // N_KV
    scale = HEAD_DIM ** -0.5
    width = WINDOW if mode == "win" else (1 << 30)
    n_steps = n_kt // ppt
    G = range(N_KV)

    @pl.when(j == 0)
    def _():
        m_s[...] = jnp.full(m_s.shape, NEG, F32)
        l_s[...] = jnp.zeros(l_s.shape, F32)
        acc_s[...] = jnp.zeros(acc_s.shape, F32)

    q_first = q_offset + qt * tq
    qpos = q_first + lax.broadcasted_iota(jnp.int32, (tq, 1), 0)

    def tile(kt_refs, vt_refs, kp0s):
        lane = lax.broadcasted_iota(jnp.int32, (1, PAGE), 1)
        kpos = jnp.concatenate([jnp.where(kp >= 0, kp, 1 << 30) + lane for kp in kp0s], axis=1)
        dist = qpos - kpos
        band = (dist >= 0) & (dist < width)
        if mode == "sel":
            nsp = sel_ref.shape[2]
            blk = lax.broadcasted_iota(jnp.int32, (nsp, 1), 0)
            expand = (blk == kpos // SEL_BLOCK).astype(BF16)
            hits = [_dg(sel_ref[0, g].astype(BF16), expand, _TN) for g in G]
            masks = [band & (hits[g] > 0.5) for g in G]
        else:
            masks = [band for g in G]
        masks = [jnp.concatenate([mk] * rpg, axis=0) for mk in masks]
        qg = [jnp.concatenate([q_ref[0, :, (g * rpg + h) * HEAD_DIM:(g * rpg + h + 1) * HEAD_DIM] for h in range(rpg)],
                              axis=0) for g in G]
        kg = [jnp.concatenate([_group_tile(r, g) for r in kt_refs], axis=0) for g in G]
        vg = [jnp.concatenate([_group_tile(r, g) for r in vt_refs], axis=0) for g in G]
        s = [_dot(qg[g], kg[g], 1, _NT) * scale for g in G]
        s = [jnp.where(masks[g], s[g], NEG) for g in G]
        m_old = [m_s[g] for g in G]
        m_new = [jnp.maximum(m_old[g], jnp.max(s[g], -1, keepdims=True)) for g in G]
        alpha = [jnp.exp(m_old[g] - m_new[g]) for g in G]
        p = [jnp.where(masks[g], jnp.exp(s[g] - m_new[g]), 0.0) for g in G]
        pv = [_dot(p[g], vg[g], 1) for g in G]
        for g in G:
            l_s[g] = alpha[g] * l_s[g] + jnp.sum(p[g], -1, keepdims=True)
            acc_s[g] = alpha[g] * acc_s[g] + pv[g]
            m_s[g] = m_new[g]

    base = (pl.program_id(0) * pl.num_programs(1) + qt) * n_kt + jnp.minimum(j, n_steps - 1) * ppt
    kp0s = [kp0_ref[base + pi] for pi in range(ppt)]
    live = None
    for kp in kp0s:
        ok = (kp >= 0) & (kp <= q_first + tq - 1) & (kp + PAGE - 1 > q_first - width)
        live = ok if live is None else (live | ok)
    if has_tail:
        live = live & (j < n_steps)

    @pl.when(live)
    def _():
        tile(k_refs, v_refs, kp0s)

    if has_tail:
        @pl.when(j == n_steps)
        def _():
            tile([tk_ref], [tv_ref], [tail_pos0])

    @pl.when(j == pl.num_programs(2) - 1)
    def _():
        gate = gate_ref[0]
        for g in range(N_KV):
            o = acc_s[g] / jnp.maximum(l_s[g], 1e-30)
            for h in range(rpg):
                hd = g * rpg + h
                cs = slice(hd * HEAD_DIM, (hd + 1) * HEAD_DIM)
                o_ref[0, :, cs] = prev_ref[0, :, cs] + gate[:, hd * 3 + branch:hd * 3 + branch + 1] * o[h * tq:(h + 1) * tq]


def nsa_attn(q, pages, tbl, kp0, slot_k, slot_v, tail, sel, gates, prev, *, mode, q_offset, tq, tail_pos0, branch,
             ppt=1):
    bsz, t, dq = q.shape
    n_qt = t // tq
    n_kt = tbl.shape[2]
    assert n_kt % ppt == 0
    gw = N_KV * HEAD_DIM
    n_heads = dq // HEAD_DIM
    rows = (n_heads // N_KV) * tq
    has_tail = tail is not None
    n_steps = n_kt // ppt + (1 if has_tail else 0)

    def page_idx(pi):
        return lambda b, i, j, tb, kp: tb[(b * n_qt + i) * n_kt + jnp.minimum(j, n_kt // ppt - 1) * ppt + pi]

    qspec = pl.BlockSpec((1, tq, dq), lambda b, i, j, tb, kp: (b, i, 0))
    in_specs = [qspec]
    in_specs += [_page_spec(pages, page_idx(pi), lambda *a: slot_k) for pi in range(ppt)]
    in_specs += [_page_spec(pages, page_idx(pi), lambda *a: slot_v) for pi in range(ppt)]
    args = [q] + [pages] * (2 * ppt)
    if has_tail:
        in_specs += [pl.BlockSpec((None, PAGE, gw), lambda b, i, j, tb, kp: (b, 0, 0)),
                     pl.BlockSpec((None, PAGE, gw), lambda b, i, j, tb, kp: (b, 0, 1))]
        args += [tail, tail]
    if mode == "sel":
        nsp = sel.shape[2]
        in_specs.append(pl.BlockSpec((1, N_KV, nsp, tq), lambda b, i, j, tb, kp: (b, 0, 0, i)))
        args.append(sel)
    in_specs += [pl.BlockSpec((1, tq, gates.shape[2]), lambda b, i, j, tb, kp: (b, i, 0)), qspec]
    args += [gates, prev]
    grid_spec = pltpu.PrefetchScalarGridSpec(
        num_scalar_prefetch=2,
        grid=(bsz, n_qt, n_steps),
        in_specs=in_specs,
        out_specs=qspec,
        scratch_shapes=[pltpu.VMEM((N_KV, rows, 1), F32), pltpu.VMEM((N_KV, rows, 1), F32),
                        pltpu.VMEM((N_KV, rows, HEAD_DIM), F32)],
    )
    return pl.pallas_call(
        functools.partial(_nsa_attn_kernel, mode=mode, q_offset=q_offset, tq=tq, n_kt=n_kt, ppt=ppt,
                          has_tail=has_tail, tail_pos0=tail_pos0, branch=branch),
        grid_spec=grid_spec,
        out_shape=jax.ShapeDtypeStruct((bsz, t, dq), F32),
        compiler_params=_cparams(3),
    )(tbl.reshape(-1).astype(jnp.int32), kp0.reshape(-1).astype(jnp.int32), *args)


def _softplus(z):
    return jnp.maximum(z, 0.0) + jnp.log(1.0 + jnp.exp(-jnp.abs(z)))


def _post_logdecay(acc, w0):
    return -jnp.exp(-_softplus(-(w0 + acc)) - 0.5)


def _post_sig_bias(acc, b):
    return jax.nn.sigmoid(acc + b)


def _post_vres(acc, v0, v, vf):
    return v + (vf - v) * jax.nn.sigmoid(v0 + acc)


def _post_ple(acc, t, b, x):
    return x + jax.nn.sigmoid(acc + b) * t


def _pad_rows(x, m):
    return jnp.pad(x, ((0, m - x.shape[0]),) + ((0, 0),) * (x.ndim - 1))


def kernel(x_prompt, x_sample, cache_kv_pages, cache_win_kv, state_wkv, state_shift, page_table, p_prompt, p_sample, ln_g, ln_b, rw_mix, rw_w_rkv, rw_w0, rw_w1, rw_w2, rw_a0, rw_a1, rw_a2, rw_v0, rw_v1, rw_v2, rw_g1, rw_g2, rw_k_k, rw_k_a, rw_r_k, rw_lnx_g, rw_lnx_b, rw_w_o, kv_ln_g, kv_ln_b, w_kv, cmp_pos, cmp_w1, cmp_w2, nsa_w_qg, nsa_b_g, nsa_w_o, moe_router, moe_bias, moe_w_in, moe_w_out, moe_ws_in, moe_ws_out, ple_proj, ple_gate, ple_gate_b):
    bp, tp, d = x_prompt.shape
    bs, ts, _ = x_sample.shape
    depth = ln_g.shape[0]
    n_a = state_wkv.shape[0]
    alpha = (2 * depth) ** 0.25

    def resid_ln(acc, g, b, x_rows):
        return _resid_ln(acc, g, b, x_rows, alpha)

    n_p, n_s = bp * tp, bs * ts
    n_tok = n_p + n_s
    m = _round_up(n_tok, 768) if n_tok > 768 else _round_up(n_tok, SUBLANES)
    ts_pad = _round_up(ts, SUBLANES)
    hh = d // RW_HEAD
    gw = N_KV * HEAD_DIM
    n_pg_s = page_table.shape[1]
    past_len = n_pg_s * cache_kv_pages.shape[1]
    win_buf = cache_win_kv.shape[1]
    assert cache_kv_pages.shape[1] == PAGE and tp % PAGE == 0 and win_buf % PAGE == 0 and ts < CMP_BLOCK

    def join(a_p, a_s):
        return _pad_rows(jnp.concatenate([a_p.reshape(n_p, -1), a_s.reshape(n_s, -1)], axis=0), m)

    def split_seq(z):
        zp = z[:n_p].reshape(bp, tp, -1)
        zs = jnp.pad(z[n_p:n_tok].reshape(bs, ts, -1), ((0, 0), (0, ts_pad - ts), (0, 0)))
        return zp, zs

    def merge_seq(zp, zs):
        return join(zp, zs[:, :ts])

    x = join(x_prompt, x_sample)
    pe = jnp.concatenate([p_prompt.reshape(depth, n_p, -1), p_sample.reshape(depth, n_s, -1)], axis=1)
    pe = jnp.pad(pe, ((0, 0), (0, m - n_tok), (0, 0)))

    shifts_p, shifts_s, wkvs_p, wkvs_s = [], [], [], []
    v_first = None
    ctx = None
    kv_rows_p = kv_rows_s = win_p = win_s = None
    for i in range(depth):
        if i < n_a:
            scan_passes = 1
            xp_, xs_ = x[:n_p].reshape(bp, tp, d), x[n_p:n_tok].reshape(bs, ts, d)
            shifts_p.append(xp_[:, -1])
            shifts_s.append(xs_[:, -1])
            prev = join(jnp.concatenate([jnp.zeros((bp, 1, d), F32), xp_[:, :-1]], axis=1),
                        jnp.concatenate([state_shift[i][:, None, :], xs_[:, :-1]], axis=1))
            mix = rw_mix[i]
            lin = linear
            r = lin(x, rw_w_rkv, widx=(i, 0), mix=(prev, mix[0]))
            k = lin(x, rw_w_rkv, widx=(i, 1), mix=(prev, mix[2]))
            v = lin(x, rw_w_rkv, widx=(i, 2), mix=(prev, mix[3]))
            lw = lin(lin(x, rw_w1, widx=(i,), mix=(prev, mix[1]), post=jnp.tanh), rw_w2, widx=(i,), rows=(rw_w0[i],),
                     post=_post_logdecay)
            if i == 0:
                v_first = v
            else:
                v = lin(lin(x, rw_v1, widx=(i - 1,), mix=(prev, mix[3])), rw_v2, widx=(i - 1,), rows=(rw_v0[i - 1],),
                        fulls=(v, v_first), post=_post_vres)
            a = lin(lin(x, rw_a1, widx=(i,), mix=(prev, mix[4])), rw_a2, widx=(i,), rows=(rw_a0[i],),
                    post=_post_sig_bias)
            g = lin(lin(x, rw_g1, widx=(i,), mix=(prev, mix[5]), post=jax.nn.sigmoid), rw_g2, widx=(i,))
            proj = (r, lw, k, v, a, g)
            vecs = (rw_k_k[i], rw_k_a[i], rw_r_k[i].reshape(-1), rw_lnx_g[i], rw_lnx_b[i])
            y_p, s_p = wkv7(*proj, *vecs, jnp.zeros((bp, hh, RW_HEAD, RW_HEAD), F32), seq_len=tp,
                            passes=scan_passes)
            y_s, s_s = wkv7_seq(*[split_seq(z)[1] for z in proj], *vecs, state_wkv[i], steps=ts)
            wkvs_p.append(s_p)
            wkvs_s.append(s_s)
            x = lin(merge_seq(y_p, y_s), rw_w_o, widx=(i,), rows=(ln_g[i, 0], ln_b[i, 0]), fulls=(x,), post=resid_ln,
                    full_rows=True)
        else:
            jn = i - n_a
            if ctx is None:
                rows = linear(layer_norm(x, kv_ln_g, kv_ln_b), w_kv)
                rows_p = rows[:n_p].reshape(bp, tp, 6 * gw)
                rows_s = rows[n_p:n_tok].reshape(bs, ts, 6 * gw)
                kv_rows_p = rows_p[:, :, :4 * gw].reshape(bp, tp, 4, N_KV, HEAD_DIM)
                kv_rows_s = rows_s[:, :, :4 * gw].reshape(bs, ts, 4, N_KV, HEAD_DIM)
                win_p = rows_p[:, max(tp - WINDOW, 0):, 4 * gw:].reshape(bp, -1, 2, N_KV, HEAD_DIM)
                new_win = rows_s[:, :, 4 * gw:].reshape(bs, ts, 2, N_KV, HEAD_DIM)
                win_s = jnp.concatenate([cache_win_kv, new_win], axis=1)[:, ts:]
                n_pg_p = tp // PAGE
                pages_p = rows[:n_p].reshape(bp * n_pg_p, PAGE, 6 * gw)
                table_p = jnp.arange(bp * n_pg_p, dtype=jnp.int32).reshape(bp, n_pg_p)
                pages_s = cache_kv_pages
                pages_w = cache_win_kv.reshape(bs * (win_buf // PAGE), PAGE, 2, N_KV, HEAD_DIM)

                def perm(c):
                    return jnp.concatenate([c[:, 0::2], c[:, 1::2]], axis=1)

                cmp_p = compress(pages_p, table_p, 0, cmp_pos, cmp_w1, cmp_w2)
                cmp_s = compress(pages_s, page_table.astype(jnp.int32), 0, cmp_pos, cmp_w1, cmp_w2)
                tail_sel = jnp.pad(rows_s[:, :, 2 * gw:4 * gw], ((0, 0), (0, PAGE - ts), (0, 0)))
                tail_win = jnp.pad(rows_s[:, :, 4 * gw:6 * gw], ((0, 0), (0, PAGE - ts), (0, 0)))
                tq_p = PAGE
                n_qt = tp // tq_p
                qi = jnp.arange(n_qt, dtype=jnp.int32)[:, None]
                kj = jnp.arange(n_pg_p, dtype=jnp.int32)[None, :]
                base = (jnp.arange(bp, dtype=jnp.int32) * n_pg_p)[:, None, None]
                tbl_sel_p = base + jnp.minimum(kj, qi)[None]
                kp0_sel_p = jnp.broadcast_to(jnp.where(kj <= qi, kj * PAGE, -1)[None], tbl_sel_p.shape)
                n_wt = WINDOW // PAGE + 1
                kw = qi - (n_wt - 1) + jnp.arange(n_wt, dtype=jnp.int32)[None, :]
                tbl_win_p = base + jnp.maximum(kw, 0)[None]
                kp0_win_p = jnp.broadcast_to(jnp.where(kw >= 0, kw * PAGE, -1)[None], tbl_win_p.shape)
                tbl_sel_s = page_table.astype(jnp.int32)[:, None, :]
                kp0_sel_s = jnp.broadcast_to((jnp.arange(n_pg_s, dtype=jnp.int32) * PAGE)[None, None], tbl_sel_s.shape)
                n_ww = win_buf // PAGE
                tbl_win_s = jnp.arange(bs * n_ww, dtype=jnp.int32).reshape(bs, 1, n_ww)
                kp0_win_s = jnp.broadcast_to(
                    (past_len - win_buf + jnp.arange(n_ww, dtype=jnp.int32) * PAGE)[None, None], tbl_win_s.shape)
                ctx = dict(kc_p=perm(cmp_p[:, 0]), vc_p=perm(cmp_p[:, 1]), kc_s=perm(cmp_s[:, 0]), vc_s=perm(cmp_s[:, 1]))
            dq = nsa_w_qg.shape[2] - nsa_b_g.shape[1]
            q = linear(x, nsa_w_qg, widx=(jn,), ncols=dq)
            gates = linear(x, nsa_w_qg[jn][:, dq:], rows=(nsa_b_g[jn],), post=_post_sig_bias)
            (q_p, q_s), (g_p, g_s) = split_seq(q), split_seq(gates)
            ns_p = -(-tp // SEL_BLOCK)
            ns_s = -(-(past_len + ts) // SEL_BLOCK)
            o_p, sel_p = nsa_cmp(q_p, ctx['kc_p'], ctx['vc_p'], g_p, q_offset=0, n_sel_blocks=ns_p, tq=tq_p)
            o_p = nsa_attn(q_p, pages_p, tbl_sel_p, kp0_sel_p, 2, 3, None, sel_p, g_p, o_p, mode="sel", q_offset=0,
                           tq=tq_p, tail_pos0=0, branch=1, ppt=_pick_tile(tp // PAGE, (4, 2, 1)))
            o_p = nsa_attn(q_p, pages_p, tbl_win_p, kp0_win_p, 4, 5, None, None, g_p, o_p, mode="win", q_offset=0,
                           tq=tq_p, tail_pos0=0, branch=2, ppt=WINDOW // PAGE + 1)
            o_s, sel_s = nsa_cmp(q_s, ctx['kc_s'], ctx['vc_s'], g_s, q_offset=past_len, n_sel_blocks=ns_s, tq=ts_pad)
            o_s = nsa_attn(q_s, pages_s, tbl_sel_s, kp0_sel_s, 2, 3, tail_sel, sel_s, g_s, o_s, mode="sel",
                           q_offset=past_len, tq=ts_pad, tail_pos0=past_len, branch=1,
                           ppt=_pick_tile(n_pg_s, (4, 2, 1)))
            o_s = nsa_attn(q_s, pages_w, tbl_win_s, kp0_win_s, 0, 1, tail_win, None, g_s, o_s, mode="win",
                           q_offset=past_len, tq=ts_pad, tail_pos0=past_len, branch=2,
                           ppt=_pick_tile(win_buf // PAGE, (4, 2, 1)))
            x = linear(merge_seq(o_p, o_s), nsa_w_o, widx=(jn,), rows=(ln_g[i, 0], ln_b[i, 0]), fulls=(x,),
                       post=resid_ln, full_rows=True)
        x = moe_ffn(x, n_tok, i, moe_router, moe_bias, moe_w_in, moe_w_out, moe_ws_in, moe_ws_out,
                    ln_g[i, 1], ln_b[i, 1], alpha)
        x = linear(x, ple_gate, widx=(i,), second=(pe, (i,), ple_proj, (i,)), rows=(ple_gate_b[i],), fulls=(x,),
                   post=_post_ple)

    y_p = x[:n_p].reshape(bp, tp, d)
    y_s = x[n_p:n_tok].reshape(bs, ts, d)
    return (y_p, y_s, kv_rows_p, kv_rows_s, win_p, win_s, jnp.stack(wkvs_p), jnp.stack(wkvs_s),
            jnp.stack(shifts_p), jnp.stack(shifts_s))
```
